```python
import math
import jax, jax.numpy as jnp
from jax import lax
import numpy as np

D_MODEL = 1024
BATCH = 2
SEQ = 8192
DEPTH = 4
DEC_BATCH = 32
DEC_SEQ = 1
PAST_LEN = 8192
PAGE_SIZE = 128

N_A_LAYERS = (DEPTH + 1) // 2
N_C_LAYERS = DEPTH // 2
NORM_EPS = 1e-6
N_MOD = 6
SSD_D_INNER = D_MODEL
SSD_HEAD_DIM = 64
SSD_HEADS = SSD_D_INNER // SSD_HEAD_DIM
SSD_GROUPS = 2
SSD_STATE = 128
SSD_CONV = 4
SSD_CHUNK = 128
SSD_CONV_DIM = SSD_D_INNER + 2 * SSD_GROUPS * SSD_STATE
RWKV_DIM = D_MODEL
RWKV_HEAD_DIM = 64
RWKV_HEADS = RWKV_DIM // RWKV_HEAD_DIM
RWKV_W_RANK = 64
RWKV_A_RANK = 64
RWKV_G_RANK = 128
RWKV_COLS = 3 * RWKV_DIM + RWKV_W_RANK + RWKV_A_RANK + RWKV_G_RANK
RWKV_LN_EPS = 64e-5
HYB_IN_COLS = SSD_D_INNER + SSD_CONV_DIM + SSD_HEADS + RWKV_COLS
HYB_OUT_DIM = SSD_D_INNER + RWKV_DIM
ATT_GROUPS = ((128, 1), (512, 4), (2048, 16))
ATT_HEADS = 8
ATT_HEAD_DIM = 64
ATT_DIM = ATT_HEADS * ATT_HEAD_DIM
ATT_QKV_COLS = len(ATT_GROUPS) * 3 * ATT_DIM
ATT_Q_BLOCK = 128
ROPE_THETA = 500000.0
ROPE_DIM = ATT_HEAD_DIM // 4
MLP_HIDDEN = 4 * D_MODEL

kernel_name = 'hybrid_ssd_rwkv7_dilated_adaln_step'


def rmsnorm(x, g, eps=NORM_EPS):
    xf = x.astype(jnp.float32)
    y = xf * lax.rsqrt(jnp.mean(xf * xf, axis=-1, keepdims=True) + eps)
    return (y * g.astype(jnp.float32)).astype(x.dtype)


def modulate(x, g, shift, scale):
    return rmsnorm(x, g) * (1.0 + scale[:, None, :]) + shift[:, None, :]


def block_size(L, pref):
    return pref if L % pref == 0 else L


def causal_conv(u, buf, w, b):
    L = u.shape[1]
    K = w.shape[0]
    ext = jnp.concatenate([buf.astype(u.dtype), u], axis=1)
    y = b + sum(ext[:, j:j + L] * w[j] for j in range(K))
    return y, ext[:, L:]


def ssd_scan(x, dt, A, B, C, s0, chunk):
    f32 = jnp.float32
    b, L, H, P = x.shape
    G, N = B.shape[2], B.shape[3]
    Hg = H // G
    nc = L // chunk
    xc = x.astype(f32).reshape(b, nc, chunk, G, Hg, P)
    dtc = dt.astype(f32).reshape(b, nc, chunk, G, Hg)
    Bc = B.astype(f32).reshape(b, nc, chunk, G, N)
    Cc = C.astype(f32).reshape(b, nc, chunk, G, N)
    acs = jnp.cumsum(dtc * A.reshape(G, Hg), axis=2)
    causal = np.tril(np.ones((chunk, chunk), dtype=bool))
    seg = acs[:, :, :, None] - acs[:, :, None, :]
    decay_ls = jnp.exp(jnp.where(causal[:, :, None, None], seg, -jnp.inf))
    cb = jnp.einsum('bclgn,bcsgn->bclsg', Cc, Bc)
    scores = cb[..., None] * decay_ls
    y_diag = jnp.einsum('bclsgh,bcsghp->bclghp', scores, dtc[..., None] * xc)
    decay_end = jnp.exp(acs[:, :, -1:] - acs)
    chunk_states = jnp.einsum('bclgn,bclgh,bclghp->bcghpn', Bc, decay_end * dtc, xc)
    chunk_decay = jnp.exp(acs[:, :, -1])

    def step(S, inp):
        dec, st = inp
        return S * dec[..., None, None] + st, S

    s_final, s_in = lax.scan(step, s0.astype(f32).reshape(b, G, Hg, P, N),
                             (jnp.moveaxis(chunk_decay, 1, 0), jnp.moveaxis(chunk_states, 1, 0)))
    s_in = jnp.moveaxis(s_in, 0, 1)
    y_off = jnp.einsum('bclgn,bcghpn,bclgh->bclghp', Cc, s_in, jnp.exp(acs))
    y = (y_diag + y_off).reshape(b, L, H, P)
    return y, s_final.reshape(b, H, P, N)


def wkv7_scan(r, w, k, v, kk, a, s0):
    def step(S, inp):
        r_t, w_t, k_t, v_t, kk_t, a_t = inp
        sa = jnp.einsum('bhvk,bhk->bhv', S, -kk_t)
        S = (S * w_t[:, :, None, :] + sa[..., None] * (kk_t * a_t)[:, :, None, :]
             + v_t[..., None] * k_t[:, :, None, :])
        return S, jnp.einsum('bhvk,bhk->bhv', S, r_t)

    xs = tuple(jnp.moveaxis(t, 1, 0) for t in (r, w, k, v, kk, a))
    S, out = lax.scan(step, s0, xs)
    return jnp.moveaxis(out, 0, 1), S


def hybrid_mixer(h, ssd_state, conv_buf, wkv_state, shift_buf, w_in, w_out, conv_w, conv_b,
                 dt_bias, a_log, d_skip, ssd_norm_w, mu, w0, w2, a0, a2, g2, k_k, k_a, r_k,
                 ln_w, ln_b):
    f32 = jnp.float32
    b, L, _ = h.shape
    u = h @ w_in
    z, xbc, dt_raw, rw = jnp.split(
        u, [SSD_D_INNER, SSD_D_INNER + SSD_CONV_DIM, SSD_D_INNER + SSD_CONV_DIM + SSD_HEADS], axis=-1)
    xbc, new_conv = causal_conv(xbc, conv_buf, conv_w, conv_b)
    xbc = jax.nn.silu(xbc)
    xs, Bm, Cm = jnp.split(xbc, [SSD_D_INNER, SSD_D_INNER + SSD_GROUPS * SSD_STATE], axis=-1)
    xs = xs.reshape(b, L, SSD_HEADS, SSD_HEAD_DIM)
    Bm = Bm.reshape(b, L, SSD_GROUPS, SSD_STATE)
    Cm = Cm.reshape(b, L, SSD_GROUPS, SSD_STATE)
    dt = jax.nn.softplus((dt_raw + dt_bias).astype(f32))
    A = -jnp.exp(a_log.astype(f32))
    y, new_ssd = ssd_scan(xs, dt, A, Bm, Cm, ssd_state, block_size(L, SSD_CHUNK))
    y = y + d_skip.astype(f32)[:, None] * xs.astype(f32)
    y = y.reshape(b, L, SSD_D_INNER) * jax.nn.silu(z.astype(f32))
    yg = y.reshape(b, L, SSD_GROUPS, SSD_D_INNER // SSD_GROUPS)
    yg = yg * lax.rsqrt(jnp.mean(yg * yg, axis=-1, keepdims=True) + NORM_EPS)
    y_ssd = (yg.reshape(b, L, SSD_D_INNER) * ssd_norm_w.astype(f32)).astype(h.dtype)
    rw_prev = jnp.concatenate([shift_buf[:, None].astype(rw.dtype), rw[:, :-1]], axis=1)
    rw_mix = rw + (rw_prev - rw) * mu
    new_shift = rw[:, -1]
    r, k, v, xw, xa, xg = jnp.split(
        rw_mix, [RWKV_DIM, 2 * RWKV_DIM, 3 * RWKV_DIM, 3 * RWKV_DIM + RWKV_W_RANK,
                 3 * RWKV_DIM + RWKV_W_RANK + RWKV_A_RANK], axis=-1)
    wlog = -jax.nn.softplus(-(w0 + jnp.tanh(xw) @ w2).astype(f32)) - 0.5
    decay = jnp.exp(-jnp.exp(wlog))
    a = jax.nn.sigmoid((a0 + xa @ a2).astype(f32))
    g = jax.nn.sigmoid(xg) @ g2

    def heads(t):
        return t.astype(f32).reshape(b, L, RWKV_HEADS, RWKV_HEAD_DIM)

    r, k, v, decay, a = heads(r), heads(k), heads(v), heads(decay), heads(a)
    kk = k * k_k.astype(f32).reshape(RWKV_HEADS, RWKV_HEAD_DIM)
    kk = kk / jnp.maximum(jnp.linalg.norm(kk, axis=-1, keepdims=True), 1e-12)
    k = k * (1.0 + (a - 1.0) * k_a.astype(f32).reshape(RWKV_HEADS, RWKV_HEAD_DIM))
    o, new_wkv = wkv7_scan(r, decay, k, v, kk, a, wkv_state.astype(f32))
    mean = jnp.mean(o, axis=-1, keepdims=True)
    var = jnp.mean(jnp.square(o - mean), axis=-1, keepdims=True)
    o = ((o - mean) * lax.rsqrt(var + RWKV_LN_EPS)).reshape(b, L, RWKV_DIM)
    o = o * ln_w.astype(f32) + ln_b.astype(f32)
    bonus = jnp.sum(r * k * r_k.astype(f32), axis=-1, keepdims=True) * v
    o = o + bonus.reshape(b, L, RWKV_DIM)
    y_rwkv = (o * g.astype(f32)).astype(h.dtype)
    out = jnp.concatenate([y_ssd, y_rwkv], axis=-1) @ w_out
    return out, new_ssd, new_conv, new_wkv, new_shift


def rope(x, pos):
    f32 = jnp.float32
    half = ROPE_DIM // 2
    inv = ROPE_THETA ** (-jnp.arange(half, dtype=f32) * 2.0 / ROPE_DIM)
    ang = pos.astype(f32)[:, None] * inv
    cos = jnp.cos(ang)[None, :, None, :]
    sin = jnp.sin(ang)[None, :, None, :]
    xf = x.astype(f32)
    x1, x2 = xf[..., :half], xf[..., half:ROPE_DIM]
    rot = jnp.concatenate([x1 * cos - x2 * sin, x2 * cos + x1 * sin], axis=-1).astype(x.dtype)
    return jnp.concatenate([rot, x[..., ROPE_DIM:]], axis=-1)


def dilated_attention(q, kv_ext, window, dilation, n_invalid, qb):
    b, L, H, hd = q.shape
    nk = window // dilation + 1
    rel = window + np.arange(qb)[:, None] - dilation * np.arange(nk)[None, :]
    scale = hd ** -0.5

    def block(i):
        start = i * qb
        q_blk = lax.dynamic_slice_in_dim(q, start, qb, axis=1)
        span = lax.dynamic_slice_in_dim(kv_ext, start, window + qb, axis=1)
        kv = jnp.take(span, rel, axis=1)
        s = jnp.einsum('bqhd,bqkhd->bhqk', q_blk, kv[:, :, :, 0]).astype(jnp.float32) * scale
        valid = (start + rel) >= n_invalid
        s = jnp.where(valid, s, -jnp.inf)
        lse = jax.nn.logsumexp(s, axis=-1)
        p = jnp.exp(s - lse[..., None]).astype(q.dtype)
        o = jnp.einsum('bhqk,bqkhd->bqhd', p, kv[:, :, :, 1])
        return o, jnp.moveaxis(lse, 1, 2)

    o, lse = lax.map(block, jnp.arange(L // qb))
    o = jnp.moveaxis(o, 0, 1).reshape(b, L, H, hd)
    lse = jnp.moveaxis(lse, 0, 1).reshape(b, L, H)
    return o, lse


def dilated_mixer(h, pos0, bufs, keep_lens, w_qkv, w_out):
    b, L, _ = h.shape
    qkv = (h @ w_qkv).reshape(b, L, len(ATT_GROUPS), 3, ATT_HEADS, ATT_HEAD_DIM)
    pos = pos0 + jnp.arange(L)
    outs, lses, new_bufs = [], [], []
    for gi, (window, dilation) in enumerate(ATT_GROUPS):
        q = rope(qkv[:, :, gi, 0], pos)
        k = rope(qkv[:, :, gi, 1], pos)
        kv_new = jnp.stack([k, qkv[:, :, gi, 2]], axis=2)
        past = bufs[gi].astype(kv_new.dtype)
        n_past = past.shape[1]
        kv_all = jnp.concatenate([past, kv_new], axis=1)
        new_bufs.append(kv_all[:, kv_all.shape[1] - keep_lens[gi]:])
        pad = jnp.zeros((b, window - n_past) + kv_new.shape[2:], kv_new.dtype)
        kv_ext = jnp.concatenate([pad, kv_all], axis=1)
        o, lse = dilated_attention(q, kv_ext, window, dilation, window - n_past,
                                   block_size(L, ATT_Q_BLOCK))
        outs.append(o)
        lses.append(lse)
    alpha = jax.nn.softmax(jnp.stack(lses), axis=0)
    o = jnp.einsum('gblh,gblhd->blhd', alpha.astype(h.dtype), jnp.stack(outs))
    return o.reshape(b, L, ATT_DIM) @ w_out, new_bufs


def setup_inputs(seed: int = 0) -> dict:
    key = jax.random.key(seed)
    ks = list(jax.random.split(key, 48))
    f32 = jnp.float32

    def nrm(shape, scale):
        return scale * jax.random.normal(ks.pop(), shape, f32)

    def unif(shape, lo, hi):
        return jax.random.uniform(ks.pop(), shape, f32, minval=lo, maxval=hi)

    NA, NC = N_A_LAYERS, N_C_LAYERS
    inp = {}
    inp['x_prompt'] = nrm((BATCH, SEQ, D_MODEL), 1.0)
    inp['x_sample'] = nrm((DEC_BATCH, DEC_SEQ, D_MODEL), 1.0)
    inp['state_ssd'] = nrm((NA, DEC_BATCH, SSD_HEADS, SSD_HEAD_DIM, SSD_STATE), 0.1)
    inp['state_ssd_conv'] = nrm((NA, DEC_BATCH, SSD_CONV - 1, SSD_CONV_DIM), 1.0)
    inp['state_wkv'] = nrm((NA, DEC_BATCH, RWKV_HEADS, RWKV_HEAD_DIM, RWKV_HEAD_DIM), 0.1)
    inp['state_wkv_shift'] = nrm((NA, DEC_BATCH, RWKV_COLS), 1.0)
    for gi, (window, _) in enumerate(ATT_GROUPS):
        inp['cache_win%d' % gi] = nrm((NC, DEC_BATCH, min(window, PAST_LEN), 2, ATT_HEADS, ATT_HEAD_DIM), 1.0)
    inp['c_prompt'] = nrm((BATCH, D_MODEL), 1.0)
    inp['c_sample'] = nrm((DEC_BATCH, D_MODEL), 1.0)
    inp['norm_mix'] = 1.0 + nrm((DEPTH, D_MODEL), 0.1)
    inp['norm_mlp'] = 1.0 + nrm((DEPTH, D_MODEL), 0.1)
    inp['norm_final'] = 1.0 + nrm((D_MODEL,), 0.1)
    inp['ada_w'] = nrm((DEPTH, D_MODEL, N_MOD * D_MODEL), 0.5 * D_MODEL ** -0.5)
    inp['ada_b'] = nrm((DEPTH, N_MOD * D_MODEL), 0.02)
    inp['mlp_w1'] = nrm((DEPTH, D_MODEL, MLP_HIDDEN), D_MODEL ** -0.5)
    inp['mlp_w2'] = nrm((DEPTH, MLP_HIDDEN, D_MODEL), MLP_HIDDEN ** -0.5)
    inp['hyb_w_in'] = nrm((NA, D_MODEL, HYB_IN_COLS), D_MODEL ** -0.5)
    inp['hyb_w_out'] = nrm((NA, HYB_OUT_DIM, D_MODEL), HYB_OUT_DIM ** -0.5)
    inp['ssd_conv_w'] = nrm((NA, SSD_CONV, SSD_CONV_DIM), SSD_CONV ** -0.5)
    inp['ssd_conv_b'] = nrm((NA, SSD_CONV_DIM), 0.02)
    dt0 = jnp.exp(unif((NA, SSD_HEADS), math.log(1e-3), math.log(1e-1)))
    inp['ssd_dt_bias'] = dt0 + jnp.log(-jnp.expm1(-dt0))
    inp['ssd_a_log'] = jnp.log(unif((NA, SSD_HEADS), 1.0, 16.0))
    inp['ssd_d'] = 1.0 + nrm((NA, SSD_HEADS), 0.1)
    inp['ssd_norm_w'] = 1.0 + nrm((NA, SSD_D_INNER), 0.1)
    inp['rwkv_mu'] = unif((NA, RWKV_COLS), 0.0, 1.0)
    inp['rwkv_w0'] = nrm((NA, RWKV_DIM), 0.5)
    inp['rwkv_w2'] = nrm((NA, RWKV_W_RANK, RWKV_DIM), RWKV_W_RANK ** -0.5)
    inp['rwkv_a0'] = nrm((NA, RWKV_DIM), 0.1)
    inp['rwkv_a2'] = nrm((NA, RWKV_A_RANK, RWKV_DIM), RWKV_A_RANK ** -0.5)
    inp['rwkv_g2'] = nrm((NA, RWKV_G_RANK, RWKV_DIM), RWKV_G_RANK ** -0.5)
    inp['rwkv_k_k'] = 0.85 + nrm((NA, RWKV_DIM), 0.05)
    inp['rwkv_k_a'] = 1.0 + nrm((NA, RWKV_DIM), 0.05)
    inp['rwkv_r_k'] = nrm((NA, RWKV_HEADS, RWKV_HEAD_DIM), 0.1)
    inp['rwkv_ln_w'] = 1.0 + nrm((NA, RWKV_DIM), 0.1)
    inp['rwkv_ln_b'] = nrm((NA, RWKV_DIM), 0.02)
    inp['att_w_qkv'] = nrm((NC, D_MODEL, ATT_QKV_COLS), D_MODEL ** -0.5)
    inp['att_w_out'] = nrm((NC, ATT_DIM, D_MODEL), ATT_DIM ** -0.5)
    return inp


def reference(x_prompt, x_sample, state_ssd, state_ssd_conv, state_wkv, state_wkv_shift,
              cache_win0, cache_win1, cache_win2, c_prompt, c_sample,
              norm_mix, norm_mlp, norm_final, ada_w, ada_b, mlp_w1, mlp_w2,
              hyb_w_in, hyb_w_out, ssd_conv_w, ssd_conv_b, ssd_dt_bias, ssd_a_log, ssd_d,
              ssd_norm_w, rwkv_mu, rwkv_w0, rwkv_w2, rwkv_a0, rwkv_a2, rwkv_g2, rwkv_k_k,
              rwkv_k_a, rwkv_r_k, rwkv_ln_w, rwkv_ln_b, att_w_qkv, att_w_out):

    def run_group(x, c, ssd_st, conv_st, wkv_st, shift_st, wins, pos0, keep_lens):
        ssd_new, conv_new, wkv_new, shift_new = [], [], [], []
        win_new = [[] for _ in ATT_GROUPS]
        for l in range(DEPTH):
            mod = jax.nn.silu(c) @ ada_w[l] + ada_b[l]
            sh1, sc1, gt1, sh2, sc2, gt2 = jnp.split(mod, N_MOD, axis=-1)
            h = modulate(x, norm_mix[l], sh1, sc1)
            i = l // 2
            if l % 2 == 0:
                out, s_ssd, s_conv, s_wkv, s_shift = hybrid_mixer(
                    h, ssd_st[i], conv_st[i], wkv_st[i], shift_st[i], hyb_w_in[i], hyb_w_out[i],
                    ssd_conv_w[i], ssd_conv_b[i], ssd_dt_bias[i], ssd_a_log[i], ssd_d[i],
                    ssd_norm_w[i], rwkv_mu[i], rwkv_w0[i], rwkv_w2[i], rwkv_a0[i], rwkv_a2[i],
                    rwkv_g2[i], rwkv_k_k[i], rwkv_k_a[i], rwkv_r_k[i], rwkv_ln_w[i], rwkv_ln_b[i])
                ssd_new.append(s_ssd)
                conv_new.append(s_conv)
                wkv_new.append(s_wkv)
                shift_new.append(s_shift)
            else:
                out, bufs = dilated_mixer(h, pos0, [w[i] for w in wins], keep_lens,
                                          att_w_qkv[i], att_w_out[i])
                for g, bnew in enumerate(bufs):
                    win_new[g].append(bnew)
            x = x + gt1[:, None, :] * out
            h = modulate(x, norm_mlp[l], sh2, sc2)
            x = x + gt2[:, None, :] * (jnp.square(jax.nn.relu(h @ mlp_w1[l])) @ mlp_w2[l])
        y = rmsnorm(x, norm_final)
        return (y, jnp.stack(ssd_new), jnp.stack(conv_new), jnp.stack(wkv_new),
                jnp.stack(shift_new), jnp.stack(win_new[0]), jnp.stack(win_new[1]),
                jnp.stack(win_new[2]))

    bp, lp = x_prompt.shape[0], x_prompt.shape[1]
    dtp = x_prompt.dtype
    zs_ssd = jnp.zeros((N_A_LAYERS, bp, SSD_HEADS, SSD_HEAD_DIM, SSD_STATE), dtp)
    zs_conv = jnp.zeros((N_A_LAYERS, bp, SSD_CONV - 1, SSD_CONV_DIM), dtp)
    zs_wkv = jnp.zeros((N_A_LAYERS, bp, RWKV_HEADS, RWKV_HEAD_DIM, RWKV_HEAD_DIM), dtp)
    zs_shift = jnp.zeros((N_A_LAYERS, bp, RWKV_COLS), dtp)
    zs_wins = [jnp.zeros((N_C_LAYERS, bp, 0, 2, ATT_HEADS, ATT_HEAD_DIM), dtp) for _ in ATT_GROUPS]
    keep_p = tuple(min(w, lp) for w, _ in ATT_GROUPS)
    keep_s = (cache_win0.shape[2], cache_win1.shape[2], cache_win2.shape[2])

    y_p, ssd_p, conv_p, wkv_p, shift_p, win0_p, win1_p, win2_p = run_group(
        x_prompt, c_prompt, zs_ssd, zs_conv, zs_wkv, zs_shift, zs_wins, 0, keep_p)
    y_s, ssd_s, conv_s, wkv_s, shift_s, win0_s, win1_s, win2_s = run_group(
        x_sample, c_sample, state_ssd, state_ssd_conv, state_wkv, state_wkv_shift,
        [cache_win0, cache_win1, cache_win2], PAST_LEN, keep_s)
    return (y_p, y_s, ssd_p, ssd_s, conv_p, conv_s, wkv_p, wkv_s, shift_p, shift_s,
            win0_p, win0_s, win1_p, win1_s, win2_p, win2_s)
```

```python
import functools
import math

import numpy as np
import jax
import jax.numpy as jnp
from jax import lax
from jax.experimental import pallas as pl
from jax.experimental.pallas import tpu as pltpu

F32 = jnp.float32
BF16 = jnp.bfloat16
HIGHEST = lax.Precision.HIGHEST

D_MODEL = 1024
DEPTH = 4
PAST_LEN = 8192
NORM_EPS = 1e-6
N_MOD = 6
SSD_HEADS = 16
SSD_HEAD_DIM = 64
SSD_GROUPS = 2
SSD_STATE = 128
SSD_CONV = 4
SSD_CHUNK = 128
RWKV_HEADS = 16
RWKV_HEAD_DIM = 64
RWKV_LN_EPS = 64e-5
ATT_GROUPS = ((128, 1), (512, 4), (2048, 16))
ATT_HEADS = 8
ATT_HEAD_DIM = 64
ATT_DIM = ATT_HEADS * ATT_HEAD_DIM
ATT_Q_BLOCK = 128
ROPE_THETA = 500000.0
ROPE_DIM = ATT_HEAD_DIM // 4
MLP_HIDDEN = 4 * D_MODEL

U_COLS = 6144
U_Z, U_XS, U_R, U_K, U_V, U_BC, U_LW, U_DT = 0, 1024, 2048, 3072, 4096, 5120, 5632, 5888

LANES = 128
VMEM_LIMIT = 48 * 1024 * 1024


def _cparams(*sem):
    return pltpu.CompilerParams(dimension_semantics=sem, vmem_limit_bytes=VMEM_LIMIT)


def _dot(a, b):
    return jnp.dot(a, b, preferred_element_type=F32)


def _dot_exact(a, b):
    return jnp.dot(a, b, preferred_element_type=F32, precision=HIGHEST)


def _dot_nt(a, b):
    return lax.dot_general(a, b, (((1,), (1,)), ((), ())), preferred_element_type=F32)


def _silu(x):
    return x * jax.nn.sigmoid(x)


def _softplus(x):
    return jnp.maximum(x, 0.0) + jnp.log1p(jnp.exp(-jnp.abs(x)))


def _block_ones():
    i = np.arange(LANES)
    return jnp.asarray((i[:, None] // 64 == i[None, :] // 64).astype(np.float32))


def _pair_eye():
    i = np.arange(64)
    j = np.arange(LANES)
    return jnp.asarray((i[:, None] == (j[None, :] % 64)).astype(np.float32))


def _head_expand(nheads, width):
    e = np.zeros((LANES, nheads * width), np.float32)
    for h in range(nheads):
        e[h, h * width:(h + 1) * width] = 1.0
    return jnp.asarray(e)


def _tril_ones(n):
    return jnp.asarray(np.tril(np.ones((n, n), np.float32)))


def _linear_kernel(*refs, pro, epi):
    refs = list(refs)
    x_ref = refs.pop(0)
    if pro == 'normmod':
        g_ref, sc_ref, sh_ref = refs.pop(0), refs.pop(0), refs.pop(0)
    w_ref = refs.pop(0)
    if epi == 'bias':
        b_ref = refs.pop(0)
    if epi == 'resgate':
        res_ref, gate_ref = refs.pop(0), refs.pop(0)
    o_ref = refs.pop(0)

    if pro == 'cast':
        h = x_ref[...].astype(BF16)
    else:
        h_ref = refs.pop(0)

        @pl.when(pl.program_id(1) == 0)
        def _():
            x = x_ref[...].astype(F32)
            if pro == 'silu':
                hh = _silu(x)
            else:
                ms = jnp.mean(x * x, axis=-1, keepdims=True)
                y = (x * lax.rsqrt(ms + NORM_EPS)) * g_ref[...]
                hh = y * (1.0 + sc_ref[...]) + sh_ref[...]
            h_ref[...] = hh.astype(BF16)

        h = h_ref[...]
    acc = _dot(h, w_ref[...])
    if epi == 'bias':
        acc = acc + b_ref[...]
    elif epi == 'relu2':
        acc = jnp.square(jnp.maximum(acc, 0.0))
    elif epi == 'resgate':
        acc = res_ref[...] + gate_ref[...] * acc
    o_ref[...] = acc.astype(o_ref.dtype)


def _linear(x, w, *, tm, tn, pro='cast', epi='none', norm=None, bias=None, res=None, gate=None,
            bpb=1, out_dtype=F32):
    M, K = x.shape
    N = w.shape[1]
    assert M % tm == 0 and N % tn == 0
    in_specs = [pl.BlockSpec((tm, K), lambda i, j: (i, 0))]
    args = [x]
    scratch = []
    if pro == 'normmod':
        g, mod, ksc, ksh = norm
        r = mod.shape[1]
        in_specs += [pl.BlockSpec((1, K), lambda i, j: (0, 0)),
                     pl.BlockSpec((None, r, K), lambda i, j: (i // bpb, 0, ksc)),
                     pl.BlockSpec((None, r, K), lambda i, j: (i // bpb, 0, ksh))]
        args += [g, mod, mod]
    if pro != 'cast':
        scratch = [pltpu.VMEM((tm, K), BF16)]
    in_specs.append(pl.BlockSpec((K, tn), lambda i, j: (0, j)))
    args.append(w)
    if epi == 'bias':
        in_specs.append(pl.BlockSpec((1, tn), lambda i, j: (0, j)))
        args.append(bias)
    if epi == 'resgate':
        mod, kg = gate
        r = mod.shape[1]
        nj = N // tn
        in_specs += [pl.BlockSpec((tm, tn), lambda i, j: (i, j)),
                     pl.BlockSpec((None, r, tn), lambda i, j: (i // bpb, 0, kg * nj + j))]
        args += [res, mod]
    return pl.pallas_call(
        functools.partial(_linear_kernel, pro=pro, epi=epi),
        grid=(M // tm, N // tn),
        in_specs=in_specs,
        out_specs=pl.BlockSpec((tm, tn), lambda i, j: (i, j)),
        out_shape=jax.ShapeDtypeStruct((M, N), out_dtype),
        scratch_shapes=scratch,
        compiler_params=_cparams("parallel", "arbitrary"),
    )(*args)


def _rmsnorm_kernel(x_ref, g_ref, o_ref):
    x = x_ref[...]
    ms = jnp.mean(x * x, axis=-1, keepdims=True)
    o_ref[...] = (x * lax.rsqrt(ms + NORM_EPS)) * g_ref[...]


def _rmsnorm(x, g, tm):
    M, K = x.shape
    return pl.pallas_call(
        _rmsnorm_kernel,
        grid=(M // tm,),
        in_specs=[pl.BlockSpec((tm, K), lambda i: (i, 0)), pl.BlockSpec((1, K), lambda i: (0, 0))],
        out_specs=pl.BlockSpec((tm, K), lambda i: (i, 0)),
        out_shape=jax.ShapeDtypeStruct((M, K), F32),
        compiler_params=_cparams("parallel"),
    )(x, g)


def _ssd_tail(y, xs, z, d_exp, norm_w):
    y = (y + d_exp * xs) * _silu(z)
    half = y.shape[1] // SSD_GROUPS
    outs = []
    for g in range(SSD_GROUPS):
        yg = y[:, g * half:(g + 1) * half]
        ms = jnp.mean(yg * yg, axis=-1, keepdims=True)
        outs.append(yg * lax.rsqrt(ms + NORM_EPS))
    return jnp.concatenate(outs, axis=1) * norm_w


def _ssd_prompt_kernel(z_ref, xs_ref, bc_ref, dt_ref, cw_ref, cb_ref, dtb_ref, alog_ref, dexp_ref,
                       nw_ref, tril_ref, e16_ref, y_ref, st_ref, extx, extbc, state, ybuf):
    c = pl.program_id(1)
    Q = SSD_CHUNK
    NX = SSD_HEADS * SSD_HEAD_DIM

    @pl.when(c == 0)
    def _():
        extx[0:8, :] = jnp.zeros((8, NX), F32)
        extbc[0:8, :] = jnp.zeros((8, 512), F32)
        state[...] = jnp.zeros_like(state)

    extx[8:8 + Q, :] = xs_ref[...]
    extbc[8:8 + Q, :] = bc_ref[...]
    cw = cw_ref[...]
    cb = cb_ref[...]
    xc = cb[:, 0:NX]
    bcc = cb[:, NX:NX + 512]
    for j in range(SSD_CONV):
        xc = xc + extx[pl.ds(5 + j, Q), :] * cw[j:j + 1, 0:NX]
        bcc = bcc + extbc[pl.ds(5 + j, Q), :] * cw[j:j + 1, NX:NX + 512]
    extx[0:8, :] = extx[Q:Q + 8, :]
    extbc[0:8, :] = extbc[Q:Q + 8, :]
    xs = _silu(xc)
    bcs = _silu(bcc)

    dt = _softplus(dt_ref[...] + dtb_ref[...])
    a_neg = -jnp.exp(alog_ref[...])
    acs = _dot_exact(tril_ref[...], dt * a_neg)
    acs_t = acs.T
    e16 = e16_ref[...]
    eacs = jnp.exp(acs)
    dt_exp = _dot_exact(dt, e16)
    eacs_exp = _dot_exact(eacs, e16)
    wend_exp = _dot_exact(jnp.exp(acs[Q - 1:Q, :] - acs) * dt, e16)
    xdt = (xs * dt_exp).astype(BF16)
    xw = (xs * wend_exp).astype(BF16)
    row = lax.broadcasted_iota(jnp.int32, (Q, Q), 0)
    col = lax.broadcasted_iota(jnp.int32, (Q, Q), 1)
    causal = row >= col
    HG = SSD_HEADS // SSD_GROUPS
    GW = HG * SSD_HEAD_DIM
    for g in range(SSD_GROUPS):
        b_g = bcs[:, g * SSD_STATE:(g + 1) * SSD_STATE]
        c_g = bcs[:, 256 + g * SSD_STATE:256 + (g + 1) * SSD_STATE].astype(BF16)
        cb_g = _dot_nt(c_g, b_g.astype(BF16))
        bt_g = b_g.T.astype(BF16)
        for hg in range(HG):
            h = g * HG + hg
            seg = acs[:, h:h + 1] - acs_t[h:h + 1, :]
            decay = jnp.where(causal, jnp.exp(seg), 0.0)
            scores = (cb_g * decay).astype(BF16)
            ybuf[:, h * 64:(h + 1) * 64] = _dot(scores, xdt[:, h * 64:(h + 1) * 64])
        st_g = state[g]
        y_off = _dot(c_g, st_g.astype(BF16)) * eacs_exp[:, g * GW:(g + 1) * GW]
        ybuf[:, g * GW:(g + 1) * GW] = ybuf[:, g * GW:(g + 1) * GW] + y_off
        state[g] = st_g * eacs_exp[Q - 1:Q, g * GW:(g + 1) * GW] + _dot(bt_g, xw[:, g * GW:(g + 1) * GW])

    y_ref[...] = _ssd_tail(ybuf[...], xs, z_ref[...], dexp_ref[...], nw_ref[...]).astype(y_ref.dtype)

    @pl.when(c == pl.num_programs(1) - 1)
    def _():
        st_ref[...] = state[...]


def _ssd_prompt(u, B, L, cw, cb, dtb, alog, dexp, nw):
    Q = SSD_CHUNK
    nc = L // Q
    row = lambda b, c: b * nc + c
    const = lambda shape: pl.BlockSpec(shape, lambda b, c: (0,) * len(shape))
    y, st = pl.pallas_call(
        _ssd_prompt_kernel,
        grid=(B, nc),
        in_specs=[pl.BlockSpec((Q, 1024), lambda b, c: (row(b, c), U_Z // 1024)),
                  pl.BlockSpec((Q, 1024), lambda b, c: (row(b, c), U_XS // 1024)),
                  pl.BlockSpec((Q, 512), lambda b, c: (row(b, c), U_BC // 512)),
                  pl.BlockSpec((Q, 128), lambda b, c: (row(b, c), U_DT // 128)),
                  const((SSD_CONV, 1536)), const((1, 1536)), const((1, 128)), const((1, 128)),
                  const((1, 1024)), const((1, 1024)), const((Q, Q)), const((128, 1024))],
        out_specs=[pl.BlockSpec((Q, 1024), lambda b, c: (row(b, c), 0)),
                   pl.BlockSpec((None, SSD_GROUPS, SSD_STATE, 512), lambda b, c: (b, 0, 0, 0))],
        out_shape=[jax.ShapeDtypeStruct((B * L, 1024), BF16),
                   jax.ShapeDtypeStruct((B, SSD_GROUPS, SSD_STATE, 512), F32)],
        scratch_shapes=[pltpu.VMEM((Q + 8, 1024), F32), pltpu.VMEM((Q + 8, 512), F32),
                        pltpu.VMEM((SSD_GROUPS, SSD_STATE, 512), F32), pltpu.VMEM((Q, 1024), F32)],
        compiler_params=_cparams("parallel", "arbitrary"),
    )(u, u, u, u, cw, cb, dtb, alog, dexp, nw, _tril_ones(Q), _head_expand(SSD_HEADS, 64))
    st = st.reshape(B, SSD_GROUPS, SSD_STATE, SSD_HEADS // SSD_GROUPS, SSD_HEAD_DIM)
    st = jnp.transpose(st, (0, 1, 3, 4, 2)).reshape(B, SSD_HEADS, SSD_HEAD_DIM, SSD_STATE)
    return y, st


def _ssd_step_kernel(z_ref, xs_ref, bc_ref, dt_ref, cx_ref, cbc_ref, s_ref, cw_ref, cb_ref, dtb_ref,
                     alog_ref, dexp_ref, nw_ref, e2_ref, y_ref, so_ref, ybuf):
    NX = SSD_HEADS * SSD_HEAD_DIM
    cw = cw_ref[...]
    cb = cb_ref[...]
    cx = cx_ref[...]
    cbc = cbc_ref[...]
    xc = cb[:, 0:NX] + xs_ref[...] * cw[3:4, 0:NX]
    bcc = cb[:, NX:NX + 512] + bc_ref[...] * cw[3:4, NX:NX + 512]
    for j in range(SSD_CONV - 1):
        xc = xc + cx[j:j + 1, :] * cw[j:j + 1, 0:NX]
        bcc = bcc + cbc[j:j + 1, :] * cw[j:j + 1, NX:NX + 512]
    xs = _silu(xc)
    bcs = _silu(bcc)
    dt = _softplus(dt_ref[...] + dtb_ref[...])
    da = jnp.exp(dt * (-jnp.exp(alog_ref[...])))
    e2 = e2_ref[...]
    lane = lax.broadcasted_iota(jnp.int32, (64, LANES), 1)
    first = lane < 64
    HG = SSD_HEADS // SSD_GROUPS
    for q in range(SSD_HEADS // 2):
        xrow = xs[:, q * LANES:(q + 1) * LANES]
        diag = e2 * xrow
        ycols = []
        for s in range(2):
            h = 2 * q + s
            g = h // HG
            xcol = jnp.sum(jnp.where(first == (s == 0), diag, 0.0), axis=1, keepdims=True)
            b_row = bcs[:, g * SSD_STATE:(g + 1) * SSD_STATE]
            c_row = bcs[:, 256 + g * SSD_STATE:256 + (g + 1) * SSD_STATE]
            s_new = s_ref[h] * da[:, h:h + 1] + (xcol * dt[:, h:h + 1]) * b_row
            so_ref[h] = s_new
            ycols.append(jnp.sum(s_new * c_row, axis=1, keepdims=True))
        ypair = jnp.where(first, ycols[0], ycols[1])
        ybuf[:, q * LANES:(q + 1) * LANES] = jnp.sum(e2 * ypair, axis=0, keepdims=True)
    y_ref[...] = _ssd_tail(ybuf[...], xs, z_ref[...], dexp_ref[...], nw_ref[...]).astype(y_ref.dtype)


def _ssd_step(u, conv_x, conv_bc, s0, cw, cb, dtb, alog, dexp, nw):
    B = u.shape[0]
    const = lambda shape: pl.BlockSpec(shape, lambda b: (0,) * len(shape))
    return pl.pallas_call(
        _ssd_step_kernel,
        grid=(B,),
        in_specs=[pl.BlockSpec((None, 1, 1024), lambda b: (b, 0, U_Z // 1024)),
                  pl.BlockSpec((None, 1, 1024), lambda b: (b, 0, U_XS // 1024)),
                  pl.BlockSpec((None, 1, 512), lambda b: (b, 0, U_BC // 512)),
                  pl.BlockSpec((None, 1, 128), lambda b: (b, 0, U_DT // 128)),
                  pl.BlockSpec((None, 3, 1024), lambda b: (b, 0, 0)),
                  pl.BlockSpec((None, 3, 512), lambda b: (b, 0, 0)),
                  pl.BlockSpec((None, SSD_HEADS, 64, 128), lambda b: (b, 0, 0, 0)),
                  const((SSD_CONV, 1536)), const((1, 1536)), const((1, 128)), const((1, 128)),
                  const((1, 1024)), const((1, 1024)), const((64, 128))],
        out_specs=[pl.BlockSpec((None, 1, 1024), lambda b: (b, 0, 0)),
                   pl.BlockSpec((None, SSD_HEADS, 64, 128), lambda b: (b, 0, 0, 0))],
        out_shape=[jax.ShapeDtypeStruct((B, 1, 1024), BF16),
                   jax.ShapeDtypeStruct(s0.shape, F32)],
        scratch_shapes=[pltpu.VMEM((1, 1024), F32)],
        compiler_params=_cparams("parallel"),
    )(u, u, u, u, conv_x, conv_bc, s0, cw, cb, dtb, alog, dexp, nw, _pair_eye())


def _rwkv_prep_kernel(*refs, shifted, bpb):
    refs = list(refs)
    cur = [refs.pop(0) for _ in range(4)]
    prev = [refs.pop(0) for _ in range(4)]
    if shifted:
        first = [refs.pop(0) for _ in range(4)]
    mu = [refs.pop(0) for _ in range(4)]
    (w0_ref, w2_ref, a0_ref, a2_ref, g2_ref, kk_ref, ka_ref, bo_ref) = [refs.pop(0) for _ in range(8)]
    (r_o, w_o, k_o, v_o, kn_o, kka_o, g_o) = refs
    i = pl.program_id(0)

    def mixed(n):
        x = cur[n][...]
        if shifted:
            rolled = pltpu.roll(x, 1, 0)
            before = jnp.where(i % bpb == 0, first[n][...], prev[n][7:8, :])
            rid = lax.broadcasted_iota(jnp.int32, x.shape, 0)
            p = jnp.where(rid == 0, before, rolled)
        else:
            p = prev[n][...]
        return x + (p - x) * mu[n][...]

    r, k, v, lw = mixed(0), mixed(1), mixed(2), mixed(3)
    blk = lw[:, 0:LANES]
    lane = lax.broadcasted_iota(jnp.int32, blk.shape, 1)
    tw = jnp.where(lane < 64, jnp.tanh(blk), blk).astype(BF16)
    wpre = w0_ref[...] + _dot(tw, w2_ref[...])
    apre = a0_ref[...] + _dot(tw, a2_ref[...])
    wlog = -_softplus(-wpre) - 0.5
    decay = jnp.exp(-jnp.exp(wlog))
    a = jax.nn.sigmoid(apre)
    g = _dot(jax.nn.sigmoid(lw[:, LANES:2 * LANES]).astype(BF16), g2_ref[...])
    kk = k * kk_ref[...]
    kk2 = kk * kk
    bo = bo_ref[...]
    for q in range(RWKV_HEADS // 2):
        sl = slice(q * LANES, (q + 1) * LANES)
        n2 = _dot_exact(kk2[:, sl], bo)
        kn = kk[:, sl] / jnp.maximum(jnp.sqrt(n2), 1e-12)
        kn_o[:, sl] = -kn
        kka_o[:, sl] = kn * a[:, sl]
    r_o[...] = r
    w_o[...] = decay
    k_o[...] = k * (1.0 + (a - 1.0) * ka_ref[...])
    v_o[...] = v
    g_o[...] = g


def _rwkv_prep(u, prev, first, mus, ws, *, tm, bpb, shifted):
    M = u.shape[0]
    cols = [(1024, U_R // 1024), (1024, U_K // 1024), (1024, U_V // 1024), (256, U_LW // 256)]
    in_specs = [pl.BlockSpec((tm, c), functools.partial(lambda i, kb: (i, kb), kb=kb)) for c, kb in cols]
    args = [u] * 4
    if shifted:
        in_specs += [pl.BlockSpec((8, c), functools.partial(
            lambda i, kb: (jnp.maximum(i * (tm // 8) - 1, 0), kb), kb=kb)) for c, kb in cols]
        args += [u] * 4
        in_specs += [pl.BlockSpec((None, 1, c), lambda i: (i // bpb, 0, 0)) for c, _ in cols]
        args += list(first)
    else:
        in_specs += [pl.BlockSpec((tm, c), lambda i: (i, 0)) for c, _ in cols]
        args += list(prev)
    in_specs += [pl.BlockSpec((1, c), lambda i: (0, 0)) for c, _ in cols]
    args += list(mus)
    wshapes = [(1, 1024), (128, 1024), (1, 1024), (128, 1024), (128, 1024), (1, 1024), (1, 1024), (128, 128)]
    in_specs += [pl.BlockSpec(s, lambda i: (0, 0)) for s in wshapes]
    args += list(ws)
    return pl.pallas_call(
        functools.partial(_rwkv_prep_kernel, shifted=shifted, bpb=bpb),
        grid=(M // tm,),
        in_specs=in_specs,
        out_specs=[pl.BlockSpec((tm, 1024), lambda i: (i, 0))] * 7,
        out_shape=[jax.ShapeDtypeStruct((M, 1024), F32)] * 7,
        compiler_params=_cparams("parallel"),
    )(*args)


def _wkv_kernel(r_ref, w_ref, k_ref, v_ref, kn_ref, ka_ref, s0_ref, e2_ref, bo_ref, o_ref, sT_ref,
                S_ref, *, tc, nb):
    c = pl.program_id(1)

    @pl.when(c == 0)
    def _():
        S_ref[...] = s0_ref[...]

    e2 = e2_ref[...]
    bo = bo_ref[...]
    first = lax.broadcasted_iota(jnp.int32, (64, LANES), 1) < 64

    def pair_sum(x):
        sa = jnp.sum(jnp.where(first, x, 0.0), axis=1, keepdims=True)
        sb = jnp.sum(jnp.where(first, 0.0, x), axis=1, keepdims=True)
        return jnp.where(first, sa, sb)

    def steps(rows, n):
        for b in range(nb):
            for p in range(RWKV_HEADS // 2):
                sl = slice(p * LANES, (p + 1) * LANES)
                r_, w_, k_, v_, kn_, ka_ = (ref[b, rows, sl] for ref in
                                            (r_ref, w_ref, k_ref, v_ref, kn_ref, ka_ref))
                outs = []
                for j in range(n):
                    row = lambda x: x[j:j + 1, :]
                    S = S_ref[b, p]
                    sa = pair_sum(S * row(kn_))
                    vcol = _dot_exact(e2 * row(v_), bo)
                    S = S * row(w_) + sa * row(ka_) + vcol * row(k_)
                    S_ref[b, p] = S
                    o = pair_sum(S * row(r_))
                    outs.append(jnp.sum(e2 * o, axis=0, keepdims=True))
                o_ref[b, rows, sl] = outs[0] if n == 1 else jnp.concatenate(outs, axis=0)

    if tc % 8 == 0:
        def group(t8, carry):
            steps(pl.ds(pl.multiple_of(t8 * 8, 8), 8), 8)
            return carry

        lax.fori_loop(0, tc // 8, group, 0)
    else:
        assert tc == 1
        steps(pl.ds(0, 1), 1)

    @pl.when(c == pl.num_programs(1) - 1)
    def _():
        sT_ref[...] = S_ref[...]


def _wkv_scan(r, w, k, v, kn, ka, s0, *, tc, nb):
    B, L, _ = r.shape
    s0p = s0.reshape(B, 8, 2, 64, 64).transpose(0, 1, 3, 2, 4).reshape(B, 8, 64, 128)
    seq = pl.BlockSpec((nb, tc, 1024), lambda b, c: (b, c, 0))
    stt = pl.BlockSpec((nb, 8, 64, 128), lambda b, c: (b, 0, 0, 0))
    o, sT = pl.pallas_call(
        functools.partial(_wkv_kernel, tc=tc, nb=nb),
        grid=(B // nb, L // tc),
        in_specs=[seq] * 6 + [stt, pl.BlockSpec((64, 128), lambda b, c: (0, 0)),
                              pl.BlockSpec((128, 128), lambda b, c: (0, 0))],
        out_specs=[seq, stt],
        out_shape=[jax.ShapeDtypeStruct((B, L, 1024), F32), jax.ShapeDtypeStruct((B, 8, 64, 128), F32)],
        scratch_shapes=[pltpu.VMEM((nb, 8, 64, 128), F32)],
        compiler_params=_cparams("parallel", "arbitrary"),
    )(r, w, k, v, kn, ka, s0p, _pair_eye(), _block_ones())
    sT = sT.reshape(B, 8, 64, 2, 64).transpose(0, 1, 3, 2, 4).reshape(B, 16, 64, 64)
    return o, sT


def _rwkv_post_kernel(o_ref, r_ref, k_ref, v_ref, g_ref, lnw_ref, lnb_ref, rk_ref, bo_ref, y_ref):
    bo = bo_ref[...]
    inv = 1.0 / RWKV_HEAD_DIM
    for q in range(RWKV_HEADS // 2):
        sl = slice(q * LANES, (q + 1) * LANES)
        o = o_ref[:, sl]
        mean = _dot_exact(o, bo) * inv
        d = o - mean
        var = _dot_exact(d * d, bo) * inv
        on = d * lax.rsqrt(var + RWKV_LN_EPS) * lnw_ref[:, sl] + lnb_ref[:, sl]
        bonus = _dot_exact(r_ref[:, sl] * k_ref[:, sl] * rk_ref[:, sl], bo) * v_ref[:, sl]
        y_ref[:, sl] = ((on + bonus) * g_ref[:, sl]).astype(y_ref.dtype)


def _rwkv_post(o, r, k, v, g, lnw, lnb, rk, tm):
    M = o.shape[0]
    blk = pl.BlockSpec((tm, 1024), lambda i: (i, 0))
    vec = pl.BlockSpec((1, 1024), lambda i: (0, 0))
    return pl.pallas_call(
        _rwkv_post_kernel,
        grid=(M // tm,),
        in_specs=[blk] * 5 + [vec] * 3 + [pl.BlockSpec((128, 128), lambda i: (0, 0))],
        out_specs=blk,
        out_shape=jax.ShapeDtypeStruct((M, 1024), BF16),
        compiler_params=_cparams("parallel"),
    )(o, r, k, v, g, lnw, lnb, rk, _block_ones())


def _rope_tables(pos):
    half = ROPE_DIM // 2
    inv = ROPE_THETA ** (-jnp.arange(half, dtype=F32) * 2.0 / ROPE_DIM)
    ang = pos.astype(F32)[:, None] * inv
    cos, sin = jnp.cos(ang), jnp.sin(ang)
    n = pos.shape[0]
    rest = ATT_HEAD_DIM - ROPE_DIM
    c = jnp.concatenate([cos, cos, jnp.ones((n, rest), F32)], axis=1)
    s_next = jnp.concatenate([-sin, jnp.zeros((n, half + rest), F32)], axis=1)
    s_prev = jnp.concatenate([jnp.zeros((n, half), F32), sin, jnp.zeros((n, rest), F32)], axis=1)
    return tuple(jnp.concatenate([t, t], axis=1) for t in (c, s_next, s_prev))


def _rope_apply(x, c, s_next, s_prev):
    n = x.shape[1]
    reps = n // LANES
    tile = lambda t: jnp.concatenate([t] * reps, axis=1)
    half = ROPE_DIM // 2
    return x * tile(c) + pltpu.roll(x, n - half, 1) * tile(s_next) + pltpu.roll(x, half, 1) * tile(s_prev)


def _attn_prompt_kernel(q_ref, kc_ref, kp_ref, vc_ref, vp_ref, cc_ref, snc_ref, spc_ref, cp_ref, snp_ref,
                        spp_ref, o_ref, lse_ref, kr_ref):
    i = pl.program_id(2)
    QB = ATT_Q_BLOCK
    q = _rope_apply(q_ref[...], cc_ref[...], snc_ref[...], spc_ref[...])
    kc = _rope_apply(kc_ref[...], cc_ref[...], snc_ref[...], spc_ref[...])
    kp = _rope_apply(kp_ref[...], cp_ref[...], snp_ref[...], spp_ref[...])
    kr_ref[...] = kc
    k2 = jnp.concatenate([kp, kc], axis=0).astype(BF16)
    v2 = jnp.concatenate([vp_ref[...], vc_ref[...]], axis=0).astype(BF16)
    qb = q.astype(BF16)
    row = lax.broadcasted_iota(jnp.int32, (QB, 2 * QB), 0)
    col = lax.broadcasted_iota(jnp.int32, (QB, 2 * QB), 1)
    valid = (col >= row) & (col <= row + QB) & ((i > 0) | (col >= QB))
    scale = ATT_HEAD_DIM ** -0.5
    for h in range(ATT_HEADS):
        sl = slice(h * 64, (h + 1) * 64)
        s = _dot_nt(qb[:, sl], k2[:, sl]) * scale
        s = jnp.where(valid, s, -jnp.inf)
        m = jnp.max(s, axis=1, keepdims=True)
        p = jnp.exp(s - m)
        l = jnp.sum(p, axis=1, keepdims=True)
        o = _dot(p.astype(BF16), v2[:, sl]) / l
        o_ref[:, sl] = o
        lse_ref[:, sl] = jnp.broadcast_to(m + jnp.log(l), (QB, 64))


def _attn_prompt(qkv, tabs, B, L, gi):
    window, d = ATT_GROUPS[gi]
    assert window == ATT_Q_BLOCK * d
    QB = ATT_Q_BLOCK
    Ls = L // d
    nblk = Ls // QB
    x = qkv.reshape(B, Ls, d * 4608)
    tabs = [t.reshape(Ls, d * LANES) for t in tabs]

    def col(which):
        return lambda b, rho, i: (b, i, rho * 9 + gi * 3 + which)

    def colp(which):
        return lambda b, rho, i: (b, jnp.maximum(i - 1, 0), rho * 9 + gi * 3 + which)

    blk = lambda f: pl.BlockSpec((None, QB, 512), f)
    tcur = pl.BlockSpec((QB, LANES), lambda b, rho, i: (i, rho))
    tprev = pl.BlockSpec((QB, LANES), lambda b, rho, i: (jnp.maximum(i - 1, 0), rho))
    out = pl.BlockSpec((None, QB, 512), lambda b, rho, i: (b, i, rho))
    o, lse, kr = pl.pallas_call(
        _attn_prompt_kernel,
        grid=(B, d, nblk),
        in_specs=[blk(col(0)), blk(col(1)), blk(colp(1)), blk(col(2)), blk(colp(2)),
                  tcur, tcur, tcur, tprev, tprev, tprev],
        out_specs=[out, out, out],
        out_shape=[jax.ShapeDtypeStruct((B, Ls, d * 512), F32)] * 3,
        compiler_params=_cparams("parallel", "parallel", "arbitrary"),
    )(x, x, x, x, x, *tabs, *tabs)
    return tuple(t.reshape(B * L, 512) for t in (o, lse, kr))


def _attn_out_kernel(o0, l0, o1, l1, o2, l2, w_ref, res_ref, gate_ref, out_ref, h_ref):
    @pl.when(pl.program_id(1) == 0)
    def _():
        m = jnp.maximum(jnp.maximum(l0[...], l1[...]), l2[...])
        a0, a1, a2 = jnp.exp(l0[...] - m), jnp.exp(l1[...] - m), jnp.exp(l2[...] - m)
        o = (a0 * o0[...] + a1 * o1[...] + a2 * o2[...]) / (a0 + a1 + a2)
        h_ref[...] = o.astype(BF16)

    out_ref[...] = res_ref[...] + gate_ref[...] * _dot(h_ref[...], w_ref[...])


def _attn_out(ols, w, res, mod, kg, *, tm, tn, bpb):
    M = res.shape[0]
    N = w.shape[1]
    r = mod.shape[1]
    nj = N // tn
    part = pl.BlockSpec((tm, 512), lambda i, j: (i, 0))
    return pl.pallas_call(
        _attn_out_kernel,
        grid=(M // tm, nj),
        in_specs=[part] * 6 + [pl.BlockSpec((512, tn), lambda i, j: (0, j)),
                               pl.BlockSpec((tm, tn), lambda i, j: (i, j)),
                               pl.BlockSpec((None, r, tn), lambda i, j: (i // bpb, 0, kg * nj + j))],
        out_specs=pl.BlockSpec((tm, tn), lambda i, j: (i, j)),
        out_shape=jax.ShapeDtypeStruct((M, N), F32),
        scratch_shapes=[pltpu.VMEM((tm, 512), BF16)],
        compiler_params=_cparams("parallel", "arbitrary"),
    )(*ols, w, res, mod)


def _attn_step_kernel(qkv_ref, c0_ref, c1_ref, c2_ref, tc_ref, tn_ref, tp_ref, o_ref, kr_ref):
    caches = (c0_ref, c1_ref, c2_ref)
    hrow = lax.broadcasted_iota(jnp.int32, (ATT_HEADS, ATT_DIM), 0)
    hlane = lax.broadcasted_iota(jnp.int32, (ATT_HEADS, ATT_DIM), 1) // ATT_HEAD_DIM
    own = hrow == hlane
    scale = ATT_HEAD_DIM ** -0.5
    tabs = (tc_ref[...], tn_ref[...], tp_ref[...])
    parts = []
    for gi in range(len(ATT_GROUPS)):
        base = gi * 3 * ATT_DIM
        q = _rope_apply(qkv_ref[:, base:base + ATT_DIM], *tabs)
        kn = _rope_apply(qkv_ref[:, base + ATT_DIM:base + 2 * ATT_DIM], *tabs)
        vn = qkv_ref[:, base + 2 * ATT_DIM:base + 3 * ATT_DIM]
        kr_ref[:, gi * ATT_DIM:(gi + 1) * ATT_DIM] = kn
        kv = caches[gi][...]
        qm = jnp.where(own, q, 0.0)
        s = _dot_nt(qm.astype(BF16), kv[:, 0:ATT_DIM].astype(BF16)) * scale
        s_new = jnp.sum(qm * kn, axis=1, keepdims=True) * scale
        m = jnp.maximum(jnp.max(s, axis=1, keepdims=True), s_new)
        p = jnp.exp(s - m)
        p_new = jnp.exp(s_new - m)
        l = jnp.sum(p, axis=1, keepdims=True) + p_new
        acc = _dot(p.astype(BF16), kv[:, ATT_DIM:2 * ATT_DIM].astype(BF16)) + p_new * vn
        parts.append((acc / l, m + jnp.log(l)))
    mm = jnp.maximum(jnp.maximum(parts[0][1], parts[1][1]), parts[2][1])
    ws = [jnp.exp(lse - mm) for _, lse in parts]
    o = (ws[0] * parts[0][0] + ws[1] * parts[1][0] + ws[2] * parts[2][0]) / (ws[0] + ws[1] + ws[2])
    o_ref[...] = jnp.sum(jnp.where(own, o, 0.0), axis=0, keepdims=True).astype(o_ref.dtype)


def _attn_step(qkv, caches, layer, tabs):
    B = qkv.shape[0]
    specs = []
    views = []
    for gi, (window, d) in enumerate(ATT_GROUPS):
        assert caches[gi].shape[2] == window and window // d == 128
        views.append(caches[gi].reshape(caches[gi].shape[0], B, window // d, d * 2 * ATT_DIM))
        specs.append(pl.BlockSpec((None, None, window // d, 2 * ATT_DIM), lambda b: (layer, b, 0, 0)))
    tab = pl.BlockSpec((1, LANES), lambda b: (0, 0))
    return pl.pallas_call(
        _attn_step_kernel,
        grid=(B,),
        in_specs=[pl.BlockSpec((None, 1, 4608), lambda b: (b, 0, 0))] + specs + [tab] * 3,
        out_specs=[pl.BlockSpec((None, 1, ATT_DIM), lambda b: (b, 0, 0)),
                   pl.BlockSpec((None, 1, 3 * ATT_DIM), lambda b: (b, 0, 0))],
        out_shape=[jax.ShapeDtypeStruct((B, 1, ATT_DIM), BF16),
                   jax.ShapeDtypeStruct((B, 1, 3 * ATT_DIM), F32)],
        compiler_params=_cparams("parallel"),
    )(qkv, *views, *tabs)


def _hyb_params(i, hyb_w_in, hyb_w_out, ssd_conv_w, ssd_conv_b, ssd_dt_bias, ssd_a_log, ssd_d, ssd_norm_w,
                rwkv_mu, rwkv_w0, rwkv_w2, rwkv_a0, rwkv_a2, rwkv_g2, rwkv_k_k, rwkv_k_a, rwkv_r_k,
                rwkv_ln_w, rwkv_ln_b):
    w = hyb_w_in[i]
    rw0 = 2576
    w_perm = jnp.concatenate(
        [w[:, 0:1024], w[:, 1024:2048], w[:, rw0:rw0 + 3072], w[:, 2048:2560], w[:, rw0 + 3072:rw0 + 3328],
         w[:, 2560:2576], jnp.zeros((D_MODEL, U_COLS - U_DT - 16), F32)], axis=1).astype(BF16)
    pad128 = lambda v: jnp.concatenate([v, jnp.zeros((LANES - v.shape[0],), F32)])[None, :]
    z64 = jnp.zeros((64, 1024), F32)
    mu = rwkv_mu[i]
    p = dict(
        w_in=w_perm, w_out=hyb_w_out[i].astype(BF16),
        cw=ssd_conv_w[i], cb=ssd_conv_b[i][None, :], dtb=pad128(ssd_dt_bias[i]), alog=pad128(ssd_a_log[i]),
        dexp=jnp.repeat(ssd_d[i], SSD_HEAD_DIM)[None, :], nw=ssd_norm_w[i][None, :],
        mus=[mu[None, 0:1024], mu[None, 1024:2048], mu[None, 2048:3072], mu[None, 3072:3328]],
        prep_w=[rwkv_w0[i][None, :], jnp.concatenate([rwkv_w2[i], z64]).astype(BF16),
                rwkv_a0[i][None, :], jnp.concatenate([z64, rwkv_a2[i]]).astype(BF16),
                rwkv_g2[i].astype(BF16), rwkv_k_k[i][None, :], rwkv_k_a[i][None, :], _block_ones()],
        lnw=rwkv_ln_w[i][None, :], lnb=rwkv_ln_b[i][None, :], rk=rwkv_r_k[i].reshape(1, 1024),
    )
    return p


def _raw_conv_rows(u_rows):
    return jnp.concatenate([u_rows[..., U_XS:U_XS + 1024], u_rows[..., U_BC:U_BC + 512]], axis=-1)


def _raw_rw_rows(u_rows):
    return jnp.concatenate([u_rows[..., U_R:U_R + 3072], u_rows[..., U_LW:U_LW + 256]], axis=-1)


def _run_prompt(x, mods, P, hyb, att, norm_final, B, L):
    tm = 512
    bpb = L // tm
    T = B * L
    new = dict(ssd=[], conv=[], wkv=[], shift=[], win=[[], [], []])
    tabs = _rope_tables(jnp.arange(L))
    for l in range(DEPTH):
        mod = mods[l]
        i = l // 2
        gmix = P['norm_mix'][l][None, :]
        if l % 2 == 0:
            hp = hyb[i]
            u = _linear(x, hp['w_in'], tm=tm, tn=512, pro='normmod', norm=(gmix, mod, 1, 0), bpb=bpb)
            y_ssd, s_ssd = _ssd_prompt(u, B, L, hp['cw'], hp['cb'], hp['dtb'], hp['alog'], hp['dexp'], hp['nw'])
            zeros = [jnp.zeros((B, 1, c), F32) for c in (1024, 1024, 1024, 256)]
            r, w, k, v, kn, ka, g = _rwkv_prep(u, None, zeros, hp['mus'], hp['prep_w'], tm=256, bpb=L // 256,
                                               shifted=True)
            sh = lambda t: t.reshape(B, L, 1024)
            o, s_wkv = _wkv_scan(sh(r), sh(w), sh(k), sh(v), sh(kn), sh(ka),
                                 jnp.zeros((B, RWKV_HEADS, 64, 64), F32), tc=256, nb=B)
            y_rwkv = _rwkv_post(o.reshape(T, 1024), r, k, v, g, hp['lnw'], hp['lnb'], hp['rk'], 256)
            yy = jnp.concatenate([y_ssd, y_rwkv], axis=1)
            x = _linear(yy, hp['w_out'], tm=tm, tn=512, epi='resgate', res=x, gate=(mod, 2), bpb=bpb)
            u3 = u.reshape(B, L, U_COLS)
            new['ssd'].append(s_ssd)
            new['conv'].append(_raw_conv_rows(u3[:, L - (SSD_CONV - 1):]))
            new['wkv'].append(s_wkv)
            new['shift'].append(_raw_rw_rows(u3[:, L - 1]))
        else:
            ap = att[i]
            qkv = _linear(x, ap['w_qkv'], tm=tm, tn=512, pro='normmod', norm=(gmix, mod, 1, 0), bpb=bpb)
            ols = []
            q3 = qkv.reshape(B, L, 4608)
            for gi, (window, d) in enumerate(ATT_GROUPS):
                o, lse, kr = _attn_prompt(qkv, tabs, B, L, gi)
                ols += [o, lse]
                keep = min(window, L)
                kk = kr.reshape(B, L, 512)[:, L - keep:]
                vv = q3[:, L - keep:, gi * 1536 + 1024:gi * 1536 + 1536]
                new['win'][gi].append(jnp.stack([kk, vv], axis=2).reshape(B, keep, 2, ATT_HEADS, ATT_HEAD_DIM))
            x = _attn_out(ols, ap['w_out'], x, mod, 2, tm=tm, tn=512, bpb=bpb)
        gmlp = P['norm_mlp'][l][None, :]
        hid = _linear(x, P['w1'][l], tm=tm, tn=512, pro='normmod', epi='relu2', norm=(gmlp, mod, 4, 3),
                      bpb=bpb, out_dtype=BF16)
        x = _linear(hid, P['w2'][l], tm=tm, tn=512, epi='resgate', res=x, gate=(mod, 5), bpb=bpb)
    y = _rmsnorm(x, norm_final[None, :], tm).reshape(B, L, D_MODEL)
    return y, new


def _run_sample(x, mods, P, hyb, att, norm_final, states, B):
    state_ssd, state_conv, state_wkv, state_shift, caches = states
    tm = B
    new = dict(ssd=[], conv=[], wkv=[], shift=[], win=[[], [], []])
    tabs = _rope_tables(jnp.full((1,), PAST_LEN, jnp.int32))
    for l in range(DEPTH):
        mod = mods[l]
        i = l // 2
        gmix = P['norm_mix'][l][None, :]
        if l % 2 == 0:
            hp = hyb[i]
            u = _linear(x, hp['w_in'], tm=tm, tn=512, pro='normmod', norm=(gmix, mod, 1, 0))
            u3 = u.reshape(B, 1, U_COLS)
            cbuf = state_conv[i]
            y_ssd, s_ssd = _ssd_step(u3, cbuf[:, :, 0:1024], cbuf[:, :, 1024:1536], state_ssd[i],
                                     hp['cw'], hp['cb'], hp['dtb'], hp['alog'], hp['dexp'], hp['nw'])
            sb = state_shift[i]
            prev = [sb[:, 0:1024], sb[:, 1024:2048], sb[:, 2048:3072], sb[:, 3072:3328]]
            r, w, k, v, kn, ka, g = _rwkv_prep(u, prev, None, hp['mus'], hp['prep_w'], tm=tm, bpb=1,
                                               shifted=False)
            sh = lambda t: t.reshape(B, 1, 1024)
            o, s_wkv = _wkv_scan(sh(r), sh(w), sh(k), sh(v), sh(kn), sh(ka), state_wkv[i], tc=1, nb=2)
            y_rwkv = _rwkv_post(o.reshape(B, 1024), r, k, v, g, hp['lnw'], hp['lnb'], hp['rk'], tm)
            yy = jnp.concatenate([y_ssd.reshape(B, 1024), y_rwkv], axis=1)
            x = _linear(yy, hp['w_out'], tm=tm, tn=512, epi='resgate', res=x, gate=(mod, 2))
            new['ssd'].append(s_ssd)
            new['conv'].append(jnp.concatenate([cbuf[:, 1:], _raw_conv_rows(u3)], axis=1))
            new['wkv'].append(s_wkv)
            new['shift'].append(_raw_rw_rows(u))
        else:
            ap = att[i]
            qkv = _linear(x, ap['w_qkv'], tm=tm, tn=512, pro='normmod', norm=(gmix, mod, 1, 0))
            o, kr = _attn_step(qkv.reshape(B, 1, 4608), caches, i, tabs)
            x = _linear(o.reshape(B, ATT_DIM), ap['w_out'], tm=tm, tn=512, epi='resgate', res=x, gate=(mod, 2))
            for gi in range(len(ATT_GROUPS)):
                kn_ = kr[:, :, gi * ATT_DIM:(gi + 1) * ATT_DIM]
                vn_ = qkv.reshape(B, 1, 4608)[:, :, gi * 1536 + 1024:gi * 1536 + 1536]
                row = jnp.stack([kn_, vn_], axis=2).reshape(B, 1, 2, ATT_HEADS, ATT_HEAD_DIM)
                new['win'][gi].append(jnp.concatenate([caches[gi][i][:, 1:], row], axis=1))
        gmlp = P['norm_mlp'][l][None, :]
        hid = _linear(x, P['w1'][l], tm=tm, tn=512, pro='normmod', epi='relu2', norm=(gmlp, mod, 4, 3),
                      out_dtype=BF16)
        x = _linear(hid, P['w2'][l], tm=tm, tn=512, epi='resgate', res=x, gate=(mod, 5))
    y = _rmsnorm(x, norm_final[None, :], tm).reshape(B, 1, D_MODEL)
    return y, new


def kernel(x_prompt, x_sample, state_ssd, state_ssd_conv, state_wkv, state_wkv_shift, cache_win0, cache_win1, cache_win2, c_prompt, c_sample, norm_mix, norm_mlp, norm_final, ada_w, ada_b, mlp_w1, mlp_w2, hyb_w_in, hyb_w_out, ssd_conv_w, ssd_conv_b, ssd_dt_bias, ssd_a_log, ssd_d, ssd_norm_w, rwkv_mu, rwkv_w0, rwkv_w2, rwkv_a0, rwkv_a2, rwkv_g2, rwkv_k_k, rwkv_k_a, rwkv_r_k, rwkv_ln_w, rwkv_ln_b, att_w_qkv, att_w_out):
    Bp, L, _ = x_prompt.shape
    Bs = x_sample.shape[0]
    assert x_sample.shape[1] == 1

    nrow = Bp + Bs
    npad = -nrow % 16
    c_all = jnp.concatenate([c_prompt, c_sample, jnp.zeros((npad, D_MODEL), F32)], axis=0)
    mods_p, mods_s = [], []
    for l in range(DEPTH):
        mod = _linear(c_all, ada_w[l].astype(BF16), tm=nrow + npad, tn=512, pro='silu', epi='bias',
                      bias=ada_b[l][None, :])
        mods_p.append(mod[:Bp].reshape(Bp, 1, N_MOD * D_MODEL))
        mods_s.append(mod[Bp:nrow].reshape(1, Bs, N_MOD * D_MODEL))

    P = dict(norm_mix=norm_mix, norm_mlp=norm_mlp,
             w1=[mlp_w1[l].astype(BF16) for l in range(DEPTH)],
             w2=[mlp_w2[l].astype(BF16) for l in range(DEPTH)])
    hyb = [_hyb_params(i, hyb_w_in, hyb_w_out, ssd_conv_w, ssd_conv_b, ssd_dt_bias, ssd_a_log, ssd_d,
                       ssd_norm_w, rwkv_mu, rwkv_w0, rwkv_w2, rwkv_a0, rwkv_a2, rwkv_g2, rwkv_k_k,
                       rwkv_k_a, rwkv_r_k, rwkv_ln_w, rwkv_ln_b) for i in range(hyb_w_in.shape[0])]
    att = [dict(w_qkv=att_w_qkv[i].astype(BF16), w_out=att_w_out[i].astype(BF16))
           for i in range(att_w_qkv.shape[0])]

    y_p, new_p = _run_prompt(x_prompt.reshape(Bp * L, D_MODEL), mods_p, P, hyb, att, norm_final, Bp, L)
    y_s, new_s = _run_sample(x_sample.reshape(Bs, D_MODEL), mods_s, P, hyb, att, norm_final,
                             (state_ssd, state_ssd_conv, state_wkv, state_wkv_shift,
                              (cache_win0, cache_win1, cache_win2)), Bs)
    st = jnp.stack
    return (y_p, y_s, st(new_p['ssd']), st(new_s['ssd']), st(new_p['conv']), st(new_s['conv']),
            st(new_p['wkv']), st(new_s['wkv']), st(new_p['shift']), st(new_s['shift']),
            st(new_p['win'][0]), st(new_s['win'][0]), st(new_p['win'][1]), st(new_s['win'][1]),
            st(new_p['win'][2]), st(new_s['win'][2]))
```

```python
import functools
import math

import numpy as np
import jax
import jax.numpy as jnp
from jax import lax
from jax.experimental import pallas as pl
from jax.experimental.pallas import tpu as pltpu

F32 = jnp.float32
BF16 = jnp.bfloat16
HIGHEST = lax.Precision.HIGHEST

D_MODEL = 1024
DEPTH = 4
PAST_LEN = 8192
NORM_EPS = 1e-6
N_MOD = 6
SSD_HEADS = 16
SSD_HEAD_DIM = 64
SSD_GROUPS = 2
SSD_STATE = 128
SSD_CONV = 4
SSD_CHUNK = 128
RWKV_HEADS = 16
RWKV_HEAD_DIM = 64
RWKV_LN_EPS = 64e-5
ATT_GROUPS = ((128, 1), (512, 4), (2048, 16))
ATT_HEADS = 8
ATT_HEAD_DIM = 64
ATT_DIM = ATT_HEADS * ATT_HEAD_DIM
ATT_Q_BLOCK = 128
ROPE_THETA = 500000.0
ROPE_DIM = ATT_HEAD_DIM // 4
MLP_HIDDEN = 4 * D_MODEL

U_COLS = 6144
U_Z, U_XS, U_R, U_K, U_V, U_BC, U_LW, U_DT = 0, 1024, 2048, 3072, 4096, 5120, 5632, 5888

LANES = 128
VMEM_LIMIT = 48 * 1024 * 1024


def _cparams(*sem):
    return pltpu.CompilerParams(dimension_semantics=sem, vmem_limit_bytes=VMEM_LIMIT)


def _dot(a, b):
    return jnp.dot(a, b, preferred_element_type=F32)


def _dot_exact(a, b):
    return jnp.dot(a, b, preferred_element_type=F32, precision=HIGHEST)


def _dot_exact_nt(a, b):
    return lax.dot_general(a, b, (((1,), (1,)), ((), ())), preferred_element_type=F32, precision=HIGHEST)


def _dot_nt(a, b):
    return lax.dot_general(a, b, (((1,), (1,)), ((), ())), preferred_element_type=F32)


def _silu(x):
    return x * jax.nn.sigmoid(x)


def _softplus(x):
    return jnp.maximum(x, 0.0) + jnp.log1p(jnp.exp(-jnp.abs(x)))


def _block_ones():
    i = np.arange(LANES)
    return jnp.asarray((i[:, None] // 64 == i[None, :] // 64).astype(np.float32))


def _pair_eye():
    i = np.arange(64)
    j = np.arange(LANES)
    return jnp.asarray((i[:, None] == (j[None, :] % 64)).astype(np.float32))


def _head_expand(nheads, width):
    e = np.zeros((LANES, nheads * width), np.float32)
    for h in range(nheads):
        e[h, h * width:(h + 1) * width] = 1.0
    return jnp.asarray(e)


def _tril_ones(n):
    return jnp.asarray(np.tril(np.ones((n, n), np.float32)))


def _linear_kernel(*refs, pro, epi, two):
    refs = list(refs)
    x_ref = refs.pop(0)
    if pro == 'normmod':
        g_ref, sc_ref, sh_ref = refs.pop(0), refs.pop(0), refs.pop(0)
    w_ref = refs.pop(0)
    if two:
        x2_ref, w2_ref = refs.pop(0), refs.pop(0)
    if epi == 'bias':
        b_ref = refs.pop(0)
    if epi == 'resgate':
        res_ref, gate_ref = refs.pop(0), refs.pop(0)
    o_ref = refs.pop(0)

    if pro == 'cast':
        h = x_ref[...].astype(BF16)
    else:
        h_ref = refs.pop(0)

        @pl.when(pl.program_id(1) == 0)
        def _():
            x = x_ref[...].astype(F32)
            if pro == 'silu':
                hh = _silu(x)
            else:
                ms = jnp.mean(x * x, axis=-1, keepdims=True)
                y = (x * lax.rsqrt(ms + NORM_EPS)) * g_ref[...]
                hh = y * (1.0 + sc_ref[...]) + sh_ref[...]
            h_ref[...] = hh.astype(BF16)

        h = h_ref[...]
    acc = _dot(h, w_ref[...])
    if two:
        acc = acc + _dot(x2_ref[...].astype(BF16), w2_ref[...])
    if epi == 'bias':
        acc = acc + b_ref[...]
    elif epi == 'relu2':
        acc = jnp.square(jnp.maximum(acc, 0.0))
    elif epi == 'resgate':
        acc = res_ref[...] + gate_ref[...] * acc
    o_ref[...] = acc.astype(o_ref.dtype)


def _linear(x, w, *, tm, tn, pro='cast', epi='none', norm=None, bias=None, res=None, gate=None,
            bpb=1, out_dtype=F32, second=None):
    M, K = x.shape
    N = w.shape[1]
    assert M % tm == 0 and N % tn == 0
    in_specs = [pl.BlockSpec((tm, K), lambda i, j: (i, 0))]
    args = [x]
    scratch = []
    if pro == 'normmod':
        g, mod, ksc, ksh = norm
        r = mod.shape[1]
        in_specs += [pl.BlockSpec((1, K), lambda i, j: (0, 0)),
                     pl.BlockSpec((None, r, K), lambda i, j: (i // bpb, 0, ksc)),
                     pl.BlockSpec((None, r, K), lambda i, j: (i // bpb, 0, ksh))]
        args += [g, mod, mod]
    if pro != 'cast':
        scratch = [pltpu.VMEM((tm, K), BF16)]
    in_specs.append(pl.BlockSpec((K, tn), lambda i, j: (0, j)))
    args.append(w)
    if second is not None:
        x2, w2 = second
        K2 = x2.shape[1]
        in_specs += [pl.BlockSpec((tm, K2), lambda i, j: (i, 0)), pl.BlockSpec((K2, tn), lambda i, j: (0, j))]
        args += [x2, w2]
    if epi == 'bias':
        in_specs.append(pl.BlockSpec((1, tn), lambda i, j: (0, j)))
        args.append(bias)
    if epi == 'resgate':
        mod, kg = gate
        r = mod.shape[1]
        nj = N // tn
        in_specs += [pl.BlockSpec((tm, tn), lambda i, j: (i, j)),
                     pl.BlockSpec((None, r, tn), lambda i, j: (i // bpb, 0, kg * nj + j))]
        args += [res, mod]
    return pl.pallas_call(
        functools.partial(_linear_kernel, pro=pro, epi=epi, two=second is not None),
        name="linear_%s_%s" % (pro, epi),
        grid=(M // tm, N // tn),
        in_specs=in_specs,
        out_specs=pl.BlockSpec((tm, tn), lambda i, j: (i, j)),
        out_shape=jax.ShapeDtypeStruct((M, N), out_dtype),
        scratch_shapes=scratch,
        compiler_params=_cparams("parallel", "arbitrary"),
    )(*args)


def _rmsnorm_kernel(x_ref, g_ref, o_ref):
    x = x_ref[...]
    ms = jnp.mean(x * x, axis=-1, keepdims=True)
    o_ref[...] = (x * lax.rsqrt(ms + NORM_EPS)) * g_ref[...]


def _rmsnorm(x, g, tm):
    M, K = x.shape
    return pl.pallas_call(
        _rmsnorm_kernel,
        name="final_rmsnorm",
        grid=(M // tm,),
        in_specs=[pl.BlockSpec((tm, K), lambda i: (i, 0)), pl.BlockSpec((1, K), lambda i: (0, 0))],
        out_specs=pl.BlockSpec((tm, K), lambda i: (i, 0)),
        out_shape=jax.ShapeDtypeStruct((M, K), F32),
        compiler_params=_cparams("parallel"),
    )(x, g)


def _ssd_tail(y, xs, z, d_exp, norm_w):
    y = (y + d_exp * xs) * _silu(z)
    half = y.shape[1] // SSD_GROUPS
    outs = []
    for g in range(SSD_GROUPS):
        yg = y[:, g * half:(g + 1) * half]
        ms = jnp.mean(yg * yg, axis=-1, keepdims=True)
        outs.append(yg * lax.rsqrt(ms + NORM_EPS))
    return jnp.concatenate(outs, axis=1) * norm_w


def _ssd_prompt_kernel(z_ref, xs_ref, bc_ref, dt_ref, cw_ref, cb_ref, dtb_ref, alog_ref, dexp_ref,
                       nw_ref, tril_ref, e16_ref, y_ref, st_ref, extx, extbc, state, ybuf):
    c = pl.program_id(1)
    Q = SSD_CHUNK
    NX = SSD_HEADS * SSD_HEAD_DIM

    @pl.when(c == 0)
    def _():
        extx[0:8, :] = jnp.zeros((8, NX), F32)
        extbc[0:8, :] = jnp.zeros((8, 512), F32)
        state[...] = jnp.zeros_like(state)

    extx[8:8 + Q, :] = xs_ref[...]
    extbc[8:8 + Q, :] = bc_ref[...]
    cw = cw_ref[...]
    cb = cb_ref[...]
    xc = cb[:, 0:NX]
    bcc = cb[:, NX:NX + 512]
    for j in range(SSD_CONV):
        xc = xc + extx[pl.ds(5 + j, Q), :] * cw[j:j + 1, 0:NX]
        bcc = bcc + extbc[pl.ds(5 + j, Q), :] * cw[j:j + 1, NX:NX + 512]
    extx[0:8, :] = extx[Q:Q + 8, :]
    extbc[0:8, :] = extbc[Q:Q + 8, :]
    xs = _silu(xc)
    bcs = _silu(bcc)

    dt = _softplus(dt_ref[...] + dtb_ref[...])
    a_neg = -jnp.exp(alog_ref[...])
    acs = _dot_exact(tril_ref[...], dt * a_neg)
    acs_t = acs.T
    e16 = e16_ref[...]
    eacs = jnp.exp(acs)
    dt_exp = _dot_exact(dt, e16)
    eacs_exp = _dot_exact(eacs, e16)
    wend_exp = _dot_exact(jnp.exp(acs[Q - 1:Q, :] - acs) * dt, e16)
    xdt = (xs * dt_exp).astype(BF16)
    xw = (xs * wend_exp).astype(BF16)
    row = lax.broadcasted_iota(jnp.int32, (Q, Q), 0)
    col = lax.broadcasted_iota(jnp.int32, (Q, Q), 1)
    causal = row >= col
    HG = SSD_HEADS // SSD_GROUPS
    GW = HG * SSD_HEAD_DIM
    for g in range(SSD_GROUPS):
        b_g = bcs[:, g * SSD_STATE:(g + 1) * SSD_STATE]
        c_g = bcs[:, 256 + g * SSD_STATE:256 + (g + 1) * SSD_STATE].astype(BF16)
        cb_g = _dot_nt(c_g, b_g.astype(BF16))
        bt_g = b_g.T.astype(BF16)
        for hg in range(HG):
            h = g * HG + hg
            seg = acs[:, h:h + 1] - acs_t[h:h + 1, :]
            decay = jnp.where(causal, jnp.exp(seg), 0.0)
            scores = (cb_g * decay).astype(BF16)
            ybuf[:, h * 64:(h + 1) * 64] = _dot(scores, xdt[:, h * 64:(h + 1) * 64])
        st_g = state[g]
        y_off = _dot(c_g, st_g.astype(BF16)) * eacs_exp[:, g * GW:(g + 1) * GW]
        ybuf[:, g * GW:(g + 1) * GW] = ybuf[:, g * GW:(g + 1) * GW] + y_off
        state[g] = st_g * eacs_exp[Q - 1:Q, g * GW:(g + 1) * GW] + _dot(bt_g, xw[:, g * GW:(g + 1) * GW])

    y_ref[...] = _ssd_tail(ybuf[...], xs, z_ref[...], dexp_ref[...], nw_ref[...]).astype(y_ref.dtype)

    @pl.when(c == pl.num_programs(1) - 1)
    def _():
        st_ref[...] = state[...]


def _ssd_prompt(u, B, L, cw, cb, dtb, alog, dexp, nw):
    Q = SSD_CHUNK
    nc = L // Q
    row = lambda b, c: b * nc + c
    const = lambda shape: pl.BlockSpec(shape, lambda b, c: (0,) * len(shape))
    y, st = pl.pallas_call(
        _ssd_prompt_kernel,
        name="ssd_prompt",
        grid=(B, nc),
        in_specs=[pl.BlockSpec((Q, 1024), lambda b, c: (row(b, c), U_Z // 1024)),
                  pl.BlockSpec((Q, 1024), lambda b, c: (row(b, c), U_XS // 1024)),
                  pl.BlockSpec((Q, 512), lambda b, c: (row(b, c), U_BC // 512)),
                  pl.BlockSpec((Q, 128), lambda b, c: (row(b, c), U_DT // 128)),
                  const((SSD_CONV, 1536)), const((1, 1536)), const((1, 128)), const((1, 128)),
                  const((1, 1024)), const((1, 1024)), const((Q, Q)), const((128, 1024))],
        out_specs=[pl.BlockSpec((Q, 1024), lambda b, c: (row(b, c), 0)),
                   pl.BlockSpec((None, SSD_GROUPS, SSD_STATE, 512), lambda b, c: (b, 0, 0, 0))],
        out_shape=[jax.ShapeDtypeStruct((B * L, 1024), BF16),
                   jax.ShapeDtypeStruct((B, SSD_GROUPS, SSD_STATE, 512), F32)],
        scratch_shapes=[pltpu.VMEM((Q + 8, 1024), F32), pltpu.VMEM((Q + 8, 512), F32),
                        pltpu.VMEM((SSD_GROUPS, SSD_STATE, 512), F32), pltpu.VMEM((Q, 1024), F32)],
        compiler_params=_cparams("parallel", "arbitrary"),
    )(u, u, u, u, cw, cb, dtb, alog, dexp, nw, _tril_ones(Q), _head_expand(SSD_HEADS, 64))
    st = st.reshape(B, SSD_GROUPS, SSD_STATE, SSD_HEADS // SSD_GROUPS, SSD_HEAD_DIM)
    st = jnp.transpose(st, (0, 1, 3, 4, 2)).reshape(B, SSD_HEADS, SSD_HEAD_DIM, SSD_STATE)
    return y, st


def _ssd_step_kernel(z_ref, xs_ref, bc_ref, dt_ref, cx_ref, cbc_ref, s_ref, cw_ref, cb_ref, dtb_ref,
                     alog_ref, dexp_ref, nw_ref, e2_ref, y_ref, so_ref, ybuf):
    NX = SSD_HEADS * SSD_HEAD_DIM
    cw = cw_ref[...]
    cb = cb_ref[...]
    cx = cx_ref[...]
    cbc = cbc_ref[...]
    xc = cb[:, 0:NX] + xs_ref[...] * cw[3:4, 0:NX]
    bcc = cb[:, NX:NX + 512] + bc_ref[...] * cw[3:4, NX:NX + 512]
    for j in range(SSD_CONV - 1):
        xc = xc + cx[j:j + 1, :] * cw[j:j + 1, 0:NX]
        bcc = bcc + cbc[j:j + 1, :] * cw[j:j + 1, NX:NX + 512]
    xs = _silu(xc)
    bcs = _silu(bcc)
    dt = _softplus(dt_ref[...] + dtb_ref[...])
    da = jnp.exp(dt * (-jnp.exp(alog_ref[...])))
    e2 = e2_ref[...]
    lane = lax.broadcasted_iota(jnp.int32, (64, LANES), 1)
    first = lane < 64
    HG = SSD_HEADS // SSD_GROUPS
    for q in range(SSD_HEADS // 2):
        xrow = xs[:, q * LANES:(q + 1) * LANES]
        diag = e2 * xrow
        ycols = []
        for s in range(2):
            h = 2 * q + s
            g = h // HG
            xcol = jnp.sum(jnp.where(first == (s == 0), diag, 0.0), axis=1, keepdims=True)
            b_row = bcs[:, g * SSD_STATE:(g + 1) * SSD_STATE]
            c_row = bcs[:, 256 + g * SSD_STATE:256 + (g + 1) * SSD_STATE]
            s_new = s_ref[h] * da[:, h:h + 1] + (xcol * dt[:, h:h + 1]) * b_row
            so_ref[h] = s_new
            ycols.append(jnp.sum(s_new * c_row, axis=1, keepdims=True))
        ypair = jnp.where(first, ycols[0], ycols[1])
        ybuf[:, q * LANES:(q + 1) * LANES] = jnp.sum(e2 * ypair, axis=0, keepdims=True)
    y_ref[...] = _ssd_tail(ybuf[...], xs, z_ref[...], dexp_ref[...], nw_ref[...]).astype(y_ref.dtype)


def _ssd_step(u, conv_x, conv_bc, s0, cw, cb, dtb, alog, dexp, nw):
    B = u.shape[0]
    const = lambda shape: pl.BlockSpec(shape, lambda b: (0,) * len(shape))
    return pl.pallas_call(
        _ssd_step_kernel,
        name="ssd_step",
        grid=(B,),
        in_specs=[pl.BlockSpec((None, 1, 1024), lambda b: (b, 0, U_Z // 1024)),
                  pl.BlockSpec((None, 1, 1024), lambda b: (b, 0, U_XS // 1024)),
                  pl.BlockSpec((None, 1, 512), lambda b: (b, 0, U_BC // 512)),
                  pl.BlockSpec((None, 1, 128), lambda b: (b, 0, U_DT // 128)),
                  pl.BlockSpec((None, 3, 1024), lambda b: (b, 0, 0)),
                  pl.BlockSpec((None, 3, 512), lambda b: (b, 0, 0)),
                  pl.BlockSpec((None, SSD_HEADS, 64, 128), lambda b: (b, 0, 0, 0)),
                  const((SSD_CONV, 1536)), const((1, 1536)), const((1, 128)), const((1, 128)),
                  const((1, 1024)), const((1, 1024)), const((64, 128))],
        out_specs=[pl.BlockSpec((None, 1, 1024), lambda b: (b, 0, 0)),
                   pl.BlockSpec((None, SSD_HEADS, 64, 128), lambda b: (b, 0, 0, 0))],
        out_shape=[jax.ShapeDtypeStruct((B, 1, 1024), BF16),
                   jax.ShapeDtypeStruct(s0.shape, F32)],
        scratch_shapes=[pltpu.VMEM((1, 1024), F32)],
        compiler_params=_cparams("parallel"),
    )(u, u, u, u, conv_x, conv_bc, s0, cw, cb, dtb, alog, dexp, nw, _pair_eye())


def _rwkv_prep_kernel(*refs, shifted, bpb):
    refs = list(refs)
    cur = [refs.pop(0) for _ in range(4)]
    prev = [refs.pop(0) for _ in range(4)]
    if shifted:
        first = [refs.pop(0) for _ in range(4)]
    mu = [refs.pop(0) for _ in range(4)]
    (w0_ref, w2_ref, a0_ref, a2_ref, g2_ref, kk_ref, ka_ref, bo_ref) = [refs.pop(0) for _ in range(8)]
    (r_o, w_o, k_o, v_o, kn_o, kka_o, g_o) = refs
    i = pl.program_id(0)

    def mixed(n):
        x = cur[n][...]
        if shifted:
            rolled = pltpu.roll(x, 1, 0)
            before = jnp.where(i % bpb == 0, first[n][...], prev[n][7:8, :])
            rid = lax.broadcasted_iota(jnp.int32, x.shape, 0)
            p = jnp.where(rid == 0, before, rolled)
        else:
            p = prev[n][...]
        return x + (p - x) * mu[n][...]

    r, k, v, lw = mixed(0), mixed(1), mixed(2), mixed(3)
    blk = lw[:, 0:LANES]
    lane = lax.broadcasted_iota(jnp.int32, blk.shape, 1)
    tw = jnp.where(lane < 64, jnp.tanh(blk), blk).astype(BF16)
    wpre = w0_ref[...] + _dot(tw, w2_ref[...])
    apre = a0_ref[...] + _dot(tw, a2_ref[...])
    wlog = -_softplus(-wpre) - 0.5
    a = jax.nn.sigmoid(apre)
    g = _dot(jax.nn.sigmoid(lw[:, LANES:2 * LANES]).astype(BF16), g2_ref[...])
    kk = k * kk_ref[...]
    kk2 = kk * kk
    bo = bo_ref[...]
    for q in range(RWKV_HEADS // 2):
        sl = slice(q * LANES, (q + 1) * LANES)
        n2 = _dot_exact(kk2[:, sl], bo)
        kn = kk[:, sl] / jnp.maximum(jnp.sqrt(n2), 1e-12)
        kn_o[:, sl] = -kn
        kka_o[:, sl] = kn * a[:, sl]
    r_o[...] = r
    w_o[...] = -jnp.exp(wlog)
    k_o[...] = k * (1.0 + (a - 1.0) * ka_ref[...])
    v_o[...] = v
    g_o[...] = g


def _rwkv_prep(u, prev, first, mus, ws, *, tm, bpb, shifted):
    M = u.shape[0]
    cols = [(1024, U_R // 1024), (1024, U_K // 1024), (1024, U_V // 1024), (256, U_LW // 256)]
    in_specs = [pl.BlockSpec((tm, c), functools.partial(lambda i, kb: (i, kb), kb=kb)) for c, kb in cols]
    args = [u] * 4
    if shifted:
        in_specs += [pl.BlockSpec((8, c), functools.partial(
            lambda i, kb: (jnp.maximum(i * (tm // 8) - 1, 0), kb), kb=kb)) for c, kb in cols]
        args += [u] * 4
        in_specs += [pl.BlockSpec((None, 1, c), lambda i: (i // bpb, 0, 0)) for c, _ in cols]
        args += list(first)
    else:
        in_specs += [pl.BlockSpec((tm, c), lambda i: (i, 0)) for c, _ in cols]
        args += list(prev)
    in_specs += [pl.BlockSpec((1, c), lambda i: (0, 0)) for c, _ in cols]
    args += list(mus)
    wshapes = [(1, 1024), (128, 1024), (1, 1024), (128, 1024), (128, 1024), (1, 1024), (1, 1024), (128, 128)]
    in_specs += [pl.BlockSpec(s, lambda i: (0, 0)) for s in wshapes]
    args += list(ws)
    return pl.pallas_call(
        functools.partial(_rwkv_prep_kernel, shifted=shifted, bpb=bpb),
        name="rwkv_prep",
        grid=(M // tm,),
        in_specs=in_specs,
        out_specs=[pl.BlockSpec((tm, 1024), lambda i: (i, 0))] * 7,
        out_shape=[jax.ShapeDtypeStruct((M, 1024), F32)] * 7,
        compiler_params=_cparams("parallel"),
    )(*args)


def _wkv_step_kernel(r_ref, lw_ref, k_ref, v_ref, kn_ref, ka_ref, s0_ref, e2_ref, bo_ref, o_ref, sT_ref, *, nb):
    e2 = e2_ref[...]
    bo = bo_ref[...]
    first = lax.broadcasted_iota(jnp.int32, (64, LANES), 1) < 64

    def pair_sum(x):
        sa = jnp.sum(jnp.where(first, x, 0.0), axis=1, keepdims=True)
        sb = jnp.sum(jnp.where(first, 0.0, x), axis=1, keepdims=True)
        return jnp.where(first, sa, sb)

    for b in range(nb):
        for p in range(RWKV_HEADS // 2):
            sl = slice(p * LANES, (p + 1) * LANES)
            S = s0_ref[b, p]
            sa = pair_sum(S * kn_ref[b, :, sl])
            vcol = _dot_exact(e2 * v_ref[b, :, sl], bo)
            S = S * jnp.exp(lw_ref[b, :, sl]) + sa * ka_ref[b, :, sl] + vcol * k_ref[b, :, sl]
            sT_ref[b, p] = S
            o = pair_sum(S * r_ref[b, :, sl])
            o_ref[b, :, sl] = jnp.sum(e2 * o, axis=0, keepdims=True)


def _wkv_step(r, lw, k, v, kn, ka, s0, *, nb):
    B = r.shape[0]
    s0p = s0.reshape(B, 8, 2, 64, 64).transpose(0, 1, 3, 2, 4).reshape(B, 8, 64, 128)
    seq = pl.BlockSpec((nb, 1, 1024), lambda b: (b, 0, 0))
    stt = pl.BlockSpec((nb, 8, 64, 128), lambda b: (b, 0, 0, 0))
    o, sT = pl.pallas_call(
        functools.partial(_wkv_step_kernel, nb=nb),
        grid=(B // nb,),
        in_specs=[seq] * 6 + [stt, pl.BlockSpec((64, 128), lambda b: (0, 0)),
                              pl.BlockSpec((128, 128), lambda b: (0, 0))],
        out_specs=[seq, stt],
        out_shape=[jax.ShapeDtypeStruct((B, 1, 1024), F32), jax.ShapeDtypeStruct((B, 8, 64, 128), F32)],
        compiler_params=_cparams("parallel"),
        name="wkv_step",
    )(r, lw, k, v, kn, ka, s0p, _pair_eye(), _block_ones())
    sT = sT.reshape(B, 8, 64, 2, 64).transpose(0, 1, 3, 2, 4).reshape(B, 16, 64, 64)
    return o, sT


WKV_CHUNK = 64


def _wkv_chunk_kernel(r_ref, lw_ref, k_ref, v_ref, kn_ref, ka_ref, tril_ref, o_ref, sT_ref, S_ref, *, nb):
    C = WKV_CHUNK
    c = pl.program_id(1)

    @pl.when(c == 0)
    def _():
        S_ref[...] = jnp.zeros_like(S_ref)

    tril = tril_ref[...]
    lane = lax.broadcasted_iota(jnp.int32, (C, LANES), 1)
    rowi = lax.broadcasted_iota(jnp.int32, (C, LANES), 0)
    first = lane < 64
    strict = rowi > (lane % 64)
    incl = rowi >= (lane % 64)
    r128 = lax.broadcasted_iota(jnp.int32, (LANES, LANES), 0)
    c128 = lax.broadcasted_iota(jnp.int32, (LANES, LANES), 1)
    diag_blocks = (r128 < 64) == (c128 < 64)
    eye = r128 == c128

    def bd(x):
        return jnp.concatenate([jnp.where(first, x, 0.0), jnp.where(first, 0.0, x)], axis=0)

    bf = lambda x: x.astype(BF16)
    for b in range(nb):
        for p in range(RWKV_HEADS // 2):
            sl = slice(p * LANES, (p + 1) * LANES)
            r_, lw, kt, vv, al, be = (ref[b, :, sl] for ref in (r_ref, lw_ref, k_ref, v_ref, kn_ref, ka_ref))
            cs = _dot_exact(tril, lw)
            last = cs[C - 1:C, :]
            e_inv = jnp.exp(-cs)
            e_end = jnp.exp(last - cs)
            aq = al * jnp.exp(cs - lw)
            rq = r_ * jnp.exp(cs)
            bk = be * e_inv
            kk = kt * e_inv
            g = _dot_exact_nt(jnp.concatenate([aq, rq], axis=0),
                              jnp.concatenate([jnp.where(first, bk, 0.0), jnp.where(first, 0.0, bk),
                                               jnp.where(first, kk, 0.0), jnp.where(first, 0.0, kk)], axis=0))
            m1 = jnp.where(strict, g[0:C, 0:LANES], 0.0)
            m2 = jnp.where(strict, g[0:C, LANES:2 * LANES], 0.0)
            n1 = jnp.where(incl, g[C:2 * C, 0:LANES], 0.0)
            n2 = jnp.where(incl, g[C:2 * C, LANES:2 * LANES], 0.0)
            s0 = S_ref[b, p]
            s0b = bf(s0)
            vbd = bf(bd(vv))
            x = _dot(bf(aq), s0b) + _dot(bf(m2), vbd)
            mp = m1
            steps = int(math.log2(C))
            for i in range(steps):
                x = x + _dot_exact(mp, bd(x))
                if i + 1 < steps:
                    mp = _dot_exact(mp, bd(mp))
            o = _dot(bf(rq), s0b) + _dot(bf(n1), bf(bd(x))) + _dot(bf(n2), vbd)
            o_ref[b, :, sl] = o
            kv_t = jnp.concatenate([be * e_end, kt * e_end], axis=0).T
            upd = _dot(bf(kv_t), bf(jnp.concatenate([x, vv], axis=0)))
            gcol = jnp.sum(jnp.where(eye, jnp.exp(last), 0.0), axis=1, keepdims=True)
            S_ref[b, p] = jnp.where(diag_blocks, gcol * s0 + upd, 0.0)

    @pl.when(c == pl.num_programs(1) - 1)
    def _():
        sT_ref[...] = S_ref[...]


def _wkv_chunked(r, lw, k, v, kn, ka, *, nb):
    B, L, _ = r.shape
    C = WKV_CHUNK
    seq = pl.BlockSpec((nb, C, 1024), lambda b, c: (b, c, 0))
    stt = pl.BlockSpec((nb, 8, LANES, LANES), lambda b, c: (b, 0, 0, 0))
    o, sT = pl.pallas_call(
        functools.partial(_wkv_chunk_kernel, nb=nb),
        grid=(B // nb, L // C),
        in_specs=[seq] * 6 + [pl.BlockSpec((C, C), lambda b, c: (0, 0))],
        out_specs=[seq, stt],
        out_shape=[jax.ShapeDtypeStruct((B, L, 1024), F32), jax.ShapeDtypeStruct((B, 8, LANES, LANES), F32)],
        scratch_shapes=[pltpu.VMEM((nb, 8, LANES, LANES), F32)],
        compiler_params=_cparams("parallel", "arbitrary"),
        name="wkv_chunked",
    )(r, lw, k, v, kn, ka, _tril_ones(C))
    blocks = jnp.stack([sT[:, :, 0:64, 0:64], sT[:, :, 64:128, 64:128]], axis=2)
    return o, jnp.swapaxes(blocks, -1, -2).reshape(B, 16, 64, 64)


def _rwkv_post_kernel(o_ref, r_ref, k_ref, v_ref, g_ref, lnw_ref, lnb_ref, rk_ref, bo_ref, y_ref):
    bo = bo_ref[...]
    inv = 1.0 / RWKV_HEAD_DIM
    for q in range(RWKV_HEADS // 2):
        sl = slice(q * LANES, (q + 1) * LANES)
        o = o_ref[:, sl]
        mean = _dot_exact(o, bo) * inv
        d = o - mean
        var = _dot_exact(d * d, bo) * inv
        on = d * lax.rsqrt(var + RWKV_LN_EPS) * lnw_ref[:, sl] + lnb_ref[:, sl]
        bonus = _dot_exact(r_ref[:, sl] * k_ref[:, sl] * rk_ref[:, sl], bo) * v_ref[:, sl]
        y_ref[:, sl] = ((on + bonus) * g_ref[:, sl]).astype(y_ref.dtype)


def _rwkv_post(o, r, k, v, g, lnw, lnb, rk, tm):
    M = o.shape[0]
    blk = pl.BlockSpec((tm, 1024), lambda i: (i, 0))
    vec = pl.BlockSpec((1, 1024), lambda i: (0, 0))
    return pl.pallas_call(
        _rwkv_post_kernel,
        name="rwkv_post",
        grid=(M // tm,),
        in_specs=[blk] * 5 + [vec] * 3 + [pl.BlockSpec((128, 128), lambda i: (0, 0))],
        out_specs=blk,
        out_shape=jax.ShapeDtypeStruct((M, 1024), BF16),
        compiler_params=_cparams("parallel"),
    )(o, r, k, v, g, lnw, lnb, rk, _block_ones())


def _rope_tables(pos):
    half = ROPE_DIM // 2
    inv = ROPE_THETA ** (-jnp.arange(half, dtype=F32) * 2.0 / ROPE_DIM)
    ang = pos.astype(F32)[:, None] * inv
    cos, sin = jnp.cos(ang), jnp.sin(ang)
    n = pos.shape[0]
    rest = ATT_HEAD_DIM - ROPE_DIM
    c = jnp.concatenate([cos, cos, jnp.ones((n, rest), F32)], axis=1)
    s_next = jnp.concatenate([-sin, jnp.zeros((n, half + rest), F32)], axis=1)
    s_prev = jnp.concatenate([jnp.zeros((n, half), F32), sin, jnp.zeros((n, rest), F32)], axis=1)
    return tuple(jnp.concatenate([t, t], axis=1) for t in (c, s_next, s_prev))


def _rope_apply(x, c, s_next, s_prev):
    n = x.shape[1]
    reps = n // LANES
    tile = lambda t: jnp.concatenate([t] * reps, axis=1)
    half = ROPE_DIM // 2
    return x * tile(c) + pltpu.roll(x, n - half, 1) * tile(s_next) + pltpu.roll(x, half, 1) * tile(s_prev)


def _attn_prompt_kernel(q_ref, kc_ref, kp_ref, vc_ref, vp_ref, cc_ref, snc_ref, spc_ref, cp_ref, snp_ref,
                        spp_ref, o_ref, lse_ref, kr_ref, q_s, k_s, *, d):
    i = pl.program_id(0)
    QB = ATT_Q_BLOCK
    q_s[...] = _rope_apply(q_ref[...], cc_ref[...], snc_ref[...], spc_ref[...])
    kc = _rope_apply(kc_ref[...], cc_ref[...], snc_ref[...], spc_ref[...])
    kr_ref[...] = kc
    k_s[1] = kc
    k_s[0] = _rope_apply(kp_ref[...], cp_ref[...], snp_ref[...], spp_ref[...])
    row = lax.broadcasted_iota(jnp.int32, (QB, 2 * QB), 0)
    col = lax.broadcasted_iota(jnp.int32, (QB, 2 * QB), 1)
    valid = (col >= row) & (col <= row + QB) & ((i > 0) | (col >= QB))
    scale = ATT_HEAD_DIM ** -0.5
    for rho in range(d):
        rows = pl.ds(rho, QB, stride=d) if d > 1 else pl.ds(0, QB)
        qb = q_s[rows, :].astype(BF16)
        k2 = jnp.concatenate([k_s[0, rows, :], k_s[1, rows, :]], axis=0).astype(BF16)
        v2 = jnp.concatenate([vp_ref[rows, :], vc_ref[rows, :]], axis=0).astype(BF16)
        outs, lses = [], []
        for h in range(LANES // ATT_HEAD_DIM):
            sl = slice(h * ATT_HEAD_DIM, (h + 1) * ATT_HEAD_DIM)
            s = _dot_nt(qb[:, sl], k2[:, sl]) * scale
            s = jnp.where(valid, s, -jnp.inf)
            m = jnp.max(s, axis=1, keepdims=True)
            p = jnp.exp(s - m)
            l = jnp.sum(p, axis=1, keepdims=True)
            outs.append(_dot(p.astype(BF16), v2[:, sl]) / l)
            lses.append(jnp.broadcast_to(m + jnp.log(l), (QB, ATT_HEAD_DIM)))
        o_ref[rows, :] = jnp.concatenate(outs, axis=1)
        lse_ref[rows, :] = jnp.concatenate(lses, axis=1)


def _attn_prompt(qkv, tabs, B, L, gi):
    window, d = ATT_GROUPS[gi]
    assert window == ATT_Q_BLOCK * d
    R = ATT_Q_BLOCK * d
    nblk = L // R
    npair = ATT_DIM // LANES
    prev = lambda i: jnp.maximum(i - 1, 0)

    def col(which, f):
        return lambda i, b, hp: (b * nblk + f(i), gi * 3 * npair + which * npair + hp)

    same = lambda i: i
    blk = lambda f: pl.BlockSpec((R, LANES), f)
    tcur = pl.BlockSpec((R, LANES), lambda i, b, hp: (i, 0))
    tprev = pl.BlockSpec((R, LANES), lambda i, b, hp: (prev(i), 0))
    out = pl.BlockSpec((R, LANES), lambda i, b, hp: (b * nblk + i, hp))
    return pl.pallas_call(
        functools.partial(_attn_prompt_kernel, d=d),
        grid=(nblk, B, npair),
        in_specs=[blk(col(0, same)), blk(col(1, same)), blk(col(1, prev)), blk(col(2, same)), blk(col(2, prev)),
                  tcur, tcur, tcur, tprev, tprev, tprev],
        out_specs=[out, out, out],
        out_shape=[jax.ShapeDtypeStruct((B * L, ATT_DIM), F32)] * 3,
        scratch_shapes=[pltpu.VMEM((R, LANES), F32), pltpu.VMEM((2, R, LANES), F32)],
        compiler_params=_cparams("arbitrary", "arbitrary", "arbitrary"),
        name="attn_prompt_d%d" % d,
    )(qkv, qkv, qkv, qkv, qkv, *tabs, *tabs)


def _attn_out_kernel(o0, l0, o1, l1, o2, l2, w_ref, res_ref, gate_ref, out_ref, h_ref):
    @pl.when(pl.program_id(1) == 0)
    def _():
        m = jnp.maximum(jnp.maximum(l0[...], l1[...]), l2[...])
        a0, a1, a2 = jnp.exp(l0[...] - m), jnp.exp(l1[...] - m), jnp.exp(l2[...] - m)
        o = (a0 * o0[...] + a1 * o1[...] + a2 * o2[...]) / (a0 + a1 + a2)
        h_ref[...] = o.astype(BF16)

    out_ref[...] = res_ref[...] + gate_ref[...] * _dot(h_ref[...], w_ref[...])


def _attn_out(ols, w, res, mod, kg, *, tm, tn, bpb):
    M = res.shape[0]
    N = w.shape[1]
    r = mod.shape[1]
    nj = N // tn
    part = pl.BlockSpec((tm, 512), lambda i, j: (i, 0))
    return pl.pallas_call(
        _attn_out_kernel,
        name="attn_out",
        grid=(M // tm, nj),
        in_specs=[part] * 6 + [pl.BlockSpec((512, tn), lambda i, j: (0, j)),
                               pl.BlockSpec((tm, tn), lambda i, j: (i, j)),
                               pl.BlockSpec((None, r, tn), lambda i, j: (i // bpb, 0, kg * nj + j))],
        out_specs=pl.BlockSpec((tm, tn), lambda i, j: (i, j)),
        out_shape=jax.ShapeDtypeStruct((M, N), F32),
        scratch_shapes=[pltpu.VMEM((tm, 512), BF16)],
        compiler_params=_cparams("parallel", "arbitrary"),
    )(*ols, w, res, mod)


def _rope_step_kernel(qkv_ref, tc_ref, tn_ref, tp_ref, q_ref, k_ref):
    tabs = (tc_ref[...], tn_ref[...], tp_ref[...])
    for gi in range(len(ATT_GROUPS)):
        base = gi * 3 * ATT_DIM
        out = slice(gi * ATT_DIM, (gi + 1) * ATT_DIM)
        q_ref[:, out] = _rope_apply(qkv_ref[:, base:base + ATT_DIM], *tabs)
        k_ref[:, out] = _rope_apply(qkv_ref[:, base + ATT_DIM:base + 2 * ATT_DIM], *tabs)


def _rope_step(qkv, tabs):
    B = qkv.shape[0]
    n = len(ATT_GROUPS) * ATT_DIM
    full = lambda shape: pl.BlockSpec(shape, lambda: (0,) * len(shape))
    return pl.pallas_call(
        _rope_step_kernel,
        in_specs=[full(qkv.shape)] + [full((1, LANES))] * 3,
        out_specs=[full((B, n))] * 2,
        out_shape=[jax.ShapeDtypeStruct((B, n), F32)] * 2,
        name="rope_step",
    )(qkv, *tabs)


def _attn_step_kernel(q_ref, kn_ref, vn_ref, c0_ref, c1_ref, c2_ref, o_ref):
    caches = (c0_ref, c1_ref, c2_ref)
    scale = ATT_HEAD_DIM ** -0.5
    parts = []
    for gi in range(len(ATT_GROUPS)):
        q, kn, vn = q_ref[gi], kn_ref[gi], vn_ref[gi]
        kc = caches[gi][:, 0]
        vc = caches[gi][:, 1]
        s = jnp.sum(kc * q[None], axis=-1, keepdims=True) * scale
        s_new = jnp.sum(q * kn, axis=-1, keepdims=True) * scale
        m = jnp.maximum(jnp.max(s, axis=0), s_new)
        p = jnp.exp(s - m[None])
        p_new = jnp.exp(s_new - m)
        l = jnp.sum(p, axis=0) + p_new
        acc = jnp.sum(p * vc, axis=0) + p_new * vn
        parts.append((acc / l, m + jnp.log(l)))
    mm = jnp.maximum(jnp.maximum(parts[0][1], parts[1][1]), parts[2][1])
    ws = [jnp.exp(lse - mm) for _, lse in parts]
    o_ref[...] = (ws[0] * parts[0][0] + ws[1] * parts[1][0] + ws[2] * parts[2][0]) / (ws[0] + ws[1] + ws[2])


def _attn_step(q3, kn3, vn3, caches, layer):
    B = q3.shape[0]
    specs = []
    views = []
    for gi, (window, d) in enumerate(ATT_GROUPS):
        c = caches[gi]
        assert c.shape[2] == window and window // d == 128
        views.append(c.reshape(c.shape[0], B, window // d, d, 2, ATT_HEADS, ATT_HEAD_DIM))
        specs.append(pl.BlockSpec((None, None, window // d, None, 2, ATT_HEADS, ATT_HEAD_DIM),
                                  lambda b: (layer, b, 0, 0, 0, 0, 0)))
    new = pl.BlockSpec((None, len(ATT_GROUPS), ATT_HEADS, ATT_HEAD_DIM), lambda b: (b, 0, 0, 0))
    return pl.pallas_call(
        _attn_step_kernel,
        grid=(B,),
        in_specs=[new] * 3 + specs,
        out_specs=pl.BlockSpec((None, ATT_HEADS, ATT_HEAD_DIM), lambda b: (b, 0, 0)),
        out_shape=jax.ShapeDtypeStruct((B, ATT_HEADS, ATT_HEAD_DIM), F32),
        compiler_params=_cparams("parallel"),
        name="attn_step",
    )(q3, kn3, vn3, *views)


def _roll_kernel(*refs):
    n = len(ATT_GROUPS)
    caches, rows, outs, sem = refs[0:n], refs[n:2 * n], refs[2 * n:3 * n], refs[3 * n]
    copies = []
    for g in range(n):
        nc, w = caches[g].shape[0], caches[g].shape[2]
        for i in range(nc):
            copies.append(pltpu.make_async_copy(caches[g].at[i, :, pl.ds(1, w - 1)],
                                                outs[g].at[i, :, pl.ds(0, w - 1)], sem.at[len(copies)]))
        copies.append(pltpu.make_async_copy(rows[g], outs[g].at[:, :, pl.ds(w - 1, 1)], sem.at[len(copies)]))
    for cp in copies:
        cp.start()
    for cp in copies:
        cp.wait()


def _roll_windows(caches, rows):
    n = len(ATT_GROUPS)
    nsem = sum(c.shape[0] + 1 for c in caches)
    hbm = pl.BlockSpec(memory_space=pl.ANY)
    return pl.pallas_call(
        _roll_kernel,
        in_specs=[hbm] * (2 * n),
        out_specs=[hbm] * n,
        out_shape=[jax.ShapeDtypeStruct(c.shape, c.dtype) for c in caches],
        scratch_shapes=[pltpu.SemaphoreType.DMA((nsem,))],
        name="roll_windows",
    )(*caches, *rows)


def _hyb_params(i, hyb_w_in, hyb_w_out, ssd_conv_w, ssd_conv_b, ssd_dt_bias, ssd_a_log, ssd_d, ssd_norm_w,
                rwkv_mu, rwkv_w0, rwkv_w2, rwkv_a0, rwkv_a2, rwkv_g2, rwkv_k_k, rwkv_k_a, rwkv_r_k,
                rwkv_ln_w, rwkv_ln_b):
    w = hyb_w_in[i]
    rw0 = 2576
    w_perm = jnp.concatenate(
        [w[:, 0:1024], w[:, 1024:2048], w[:, rw0:rw0 + 3072], w[:, 2048:2560], w[:, rw0 + 3072:rw0 + 3328],
         w[:, 2560:2576], jnp.zeros((D_MODEL, U_COLS - U_DT - 16), F32)], axis=1).astype(BF16)
    pad128 = lambda v: jnp.concatenate([v, jnp.zeros((LANES - v.shape[0],), F32)])[None, :]
    z64 = jnp.zeros((64, 1024), F32)
    mu = rwkv_mu[i]
    p = dict(
        w_in=w_perm, w_out=hyb_w_out[i].astype(BF16),
        cw=ssd_conv_w[i], cb=ssd_conv_b[i][None, :], dtb=pad128(ssd_dt_bias[i]), alog=pad128(ssd_a_log[i]),
        dexp=jnp.repeat(ssd_d[i], SSD_HEAD_DIM)[None, :], nw=ssd_norm_w[i][None, :],
        mus=[mu[None, 0:1024], mu[None, 1024:2048], mu[None, 2048:3072], mu[None, 3072:3328]],
        prep_w=[rwkv_w0[i][None, :], jnp.concatenate([rwkv_w2[i], z64]).astype(BF16),
                rwkv_a0[i][None, :], jnp.concatenate([z64, rwkv_a2[i]]).astype(BF16),
                rwkv_g2[i].astype(BF16), rwkv_k_k[i][None, :], rwkv_k_a[i][None, :], _block_ones()],
        lnw=rwkv_ln_w[i][None, :], lnb=rwkv_ln_b[i][None, :], rk=rwkv_r_k[i].reshape(1, 1024),
    )
    return p


def _raw_conv_rows(u_rows):
    return jnp.concatenate([u_rows[..., U_XS:U_XS + 1024], u_rows[..., U_BC:U_BC + 512]], axis=-1)


def _raw_rw_rows(u_rows):
    return jnp.concatenate([u_rows[..., U_R:U_R + 3072], u_rows[..., U_LW:U_LW + 256]], axis=-1)


def _run_prompt(x, mods, P, hyb, att, norm_final, B, L):
    tm = 512
    bpb = L // tm
    T = B * L
    new = dict(ssd=[], conv=[], wkv=[], shift=[], win=[[], [], []])
    tabs = _rope_tables(jnp.arange(L))
    for l in range(DEPTH):
        mod = mods[l]
        i = l // 2
        gmix = P['norm_mix'][l][None, :]
        if l % 2 == 0:
            hp = hyb[i]
            u = _linear(x, hp['w_in'], tm=tm, tn=512, pro='normmod', norm=(gmix, mod, 1, 0), bpb=bpb)
            y_ssd, s_ssd = _ssd_prompt(u, B, L, hp['cw'], hp['cb'], hp['dtb'], hp['alog'], hp['dexp'], hp['nw'])
            zeros = [jnp.zeros((B, 1, c), F32) for c in (1024, 1024, 1024, 256)]
            r, w, k, v, kn, ka, g = _rwkv_prep(u, None, zeros, hp['mus'], hp['prep_w'], tm=256, bpb=L // 256,
                                               shifted=True)
            sh = lambda t: t.reshape(B, L, 1024)
            o, s_wkv = _wkv_chunked(sh(r), sh(w), sh(k), sh(v), sh(kn), sh(ka), nb=B)
            y_rwkv = _rwkv_post(o.reshape(T, 1024), r, k, v, g, hp['lnw'], hp['lnb'], hp['rk'], 256)
            x = _linear(y_ssd, hp['w_out'][:1024], tm=tm, tn=512, epi='resgate', res=x, gate=(mod, 2), bpb=bpb,
                        second=(y_rwkv, hp['w_out'][1024:]))
            u3 = u.reshape(B, L, U_COLS)
            new['ssd'].append(s_ssd)
            new['conv'].append(_raw_conv_rows(u3[:, L - (SSD_CONV - 1):]))
            new['wkv'].append(s_wkv)
            new['shift'].append(_raw_rw_rows(u3[:, L - 1]))
        else:
            ap = att[i]
            qkv = _linear(x, ap['w_qkv'], tm=tm, tn=512, pro='normmod', norm=(gmix, mod, 1, 0), bpb=bpb)
            ols = []
            q3 = qkv.reshape(B, L, 4608)
            for gi, (window, d) in enumerate(ATT_GROUPS):
                o, lse, kr = _attn_prompt(qkv, tabs, B, L, gi)
                ols += [o, lse]
                keep = min(window, L)
                kk = kr.reshape(B, L, 512)[:, L - keep:]
                vv = q3[:, L - keep:, gi * 1536 + 1024:gi * 1536 + 1536]
                new['win'][gi].append(jnp.stack([kk, vv], axis=2).reshape(B, keep, 2, ATT_HEADS, ATT_HEAD_DIM))
            x = _attn_out(ols, ap['w_out'], x, mod, 2, tm=tm, tn=512, bpb=bpb)
        gmlp = P['norm_mlp'][l][None, :]
        hid = _linear(x, P['w1'][l], tm=tm, tn=512, pro='normmod', epi='relu2', norm=(gmlp, mod, 4, 3),
                      bpb=bpb, out_dtype=BF16)
        x = _linear(hid, P['w2'][l], tm=tm, tn=512, epi='resgate', res=x, gate=(mod, 5), bpb=bpb)
    y = _rmsnorm(x, norm_final[None, :], tm).reshape(B, L, D_MODEL)
    return y, new


def _run_sample(x, mods, P, hyb, att, norm_final, states, B):
    state_ssd, state_conv, state_wkv, state_shift, caches = states
    tm = B
    new = dict(ssd=[], conv=[], wkv=[], shift=[], win=[[], [], []])
    tabs = _rope_tables(jnp.full((1,), PAST_LEN, jnp.int32))
    for l in range(DEPTH):
        mod = mods[l]
        i = l // 2
        gmix = P['norm_mix'][l][None, :]
        if l % 2 == 0:
            hp = hyb[i]
            u = _linear(x, hp['w_in'], tm=tm, tn=512, pro='normmod', norm=(gmix, mod, 1, 0))
            u3 = u.reshape(B, 1, U_COLS)
            cbuf = state_conv[i]
            y_ssd, s_ssd = _ssd_step(u3, cbuf[:, :, 0:1024], cbuf[:, :, 1024:1536], state_ssd[i],
                                     hp['cw'], hp['cb'], hp['dtb'], hp['alog'], hp['dexp'], hp['nw'])
            sb = state_shift[i]
            prev = [sb[:, 0:1024], sb[:, 1024:2048], sb[:, 2048:3072], sb[:, 3072:3328]]
            r, w, k, v, kn, ka, g = _rwkv_prep(u, prev, None, hp['mus'], hp['prep_w'], tm=tm, bpb=1,
                                               shifted=False)
            sh = lambda t: t.reshape(B, 1, 1024)
            o, s_wkv = _wkv_step(sh(r), sh(w), sh(k), sh(v), sh(kn), sh(ka), state_wkv[i], nb=2)
            y_rwkv = _rwkv_post(o.reshape(B, 1024), r, k, v, g, hp['lnw'], hp['lnb'], hp['rk'], tm)
            x = _linear(y_ssd.reshape(B, 1024), hp['w_out'][:1024], tm=tm, tn=512, epi='resgate', res=x,
                        gate=(mod, 2), second=(y_rwkv, hp['w_out'][1024:]))
            new['ssd'].append(s_ssd)
            new['conv'].append(jnp.concatenate([cbuf[:, 1:], _raw_conv_rows(u3)], axis=1))
            new['wkv'].append(s_wkv)
            new['shift'].append(_raw_rw_rows(u))
        else:
            ap = att[i]
            qkv = _linear(x, ap['w_qkv'], tm=tm, tn=512, pro='normmod', norm=(gmix, mod, 1, 0))
            qr, kr = _rope_step(qkv, tabs)
            heads = lambda t: t.reshape(B, len(ATT_GROUPS), ATT_HEADS, ATT_HEAD_DIM)
            vn = qkv.reshape(B, len(ATT_GROUPS), 3, ATT_DIM)[:, :, 2]
            o = _attn_step(heads(qr), heads(kr), heads(vn), caches, i)
            x = _linear(o.reshape(B, ATT_DIM), ap['w_out'], tm=tm, tn=512, epi='resgate', res=x, gate=(mod, 2))
            rows = jnp.stack([heads(kr), heads(vn)], axis=2)
            for gi in range(len(ATT_GROUPS)):
                new['win'][gi].append(rows[:, gi][:, None])
        gmlp = P['norm_mlp'][l][None, :]
        hid = _linear(x, P['w1'][l], tm=tm, tn=512, pro='normmod', epi='relu2', norm=(gmlp, mod, 4, 3),
                      out_dtype=BF16)
        x = _linear(hid, P['w2'][l], tm=tm, tn=512, epi='resgate', res=x, gate=(mod, 5))
    y = _rmsnorm(x, norm_final[None, :], tm).reshape(B, 1, D_MODEL)
    return y, new


def kernel(x_prompt, x_sample, state_ssd, state_ssd_conv, state_wkv, state_wkv_shift, cache_win0, cache_win1, cache_win2, c_prompt, c_sample, norm_mix, norm_mlp, norm_final, ada_w, ada_b, mlp_w1, mlp_w2, hyb_w_in, hyb_w_out, ssd_conv_w, ssd_conv_b, ssd_dt_bias, ssd_a_log, ssd_d, ssd_norm_w, rwkv_mu, rwkv_w0, rwkv_w2, rwkv_a0, rwkv_a2, rwkv_g2, rwkv_k_k, rwkv_k_a, rwkv_r_k, rwkv_ln_w, rwkv_ln_b, att_w_qkv, att_w_out):
    Bp, L, _ = x_prompt.shape
    Bs = x_sample.shape[0]
    assert x_sample.shape[1] == 1

    nrow = Bp + Bs
    npad = -nrow % 16
    c_all = jnp.concatenate([c_prompt, c_sample, jnp.zeros((npad, D_MODEL), F32)], axis=0)
    mods_p, mods_s = [], []
    for l in range(DEPTH):
        mod = _linear(c_all, ada_w[l].astype(BF16), tm=nrow + npad, tn=512, pro='silu', epi='bias',
                      bias=ada_b[l][None, :])
        mods_p.append(mod[:Bp].reshape(Bp, 1, N_MOD * D_MODEL))
        mods_s.append(mod[Bp:nrow].reshape(1, Bs, N_MOD * D_MODEL))

    P = dict(norm_mix=norm_mix, norm_mlp=norm_mlp,
             w1=[mlp_w1[l].astype(BF16) for l in range(DEPTH)],
             w2=[mlp_w2[l].astype(BF16) for l in range(DEPTH)])
    hyb = [_hyb_params(i, hyb_w_in, hyb_w_out, ssd_conv_w, ssd_conv_b, ssd_dt_bias, ssd_a_log, ssd_d,
                       ssd_norm_w, rwkv_mu, rwkv_w0, rwkv_w2, rwkv_a0, rwkv_a2, rwkv_g2, rwkv_k_k,
                       rwkv_k_a, rwkv_r_k, rwkv_ln_w, rwkv_ln_b) for i in range(hyb_w_in.shape[0])]
    att = [dict(w_qkv=att_w_qkv[i].astype(BF16), w_out=att_w_out[i].astype(BF16))
           for i in range(att_w_qkv.shape[0])]

    y_p, new_p = _run_prompt(x_prompt.reshape(Bp * L, D_MODEL), mods_p, P, hyb, att, norm_final, Bp, L)
    y_s, new_s = _run_sample(x_sample.reshape(Bs, D_MODEL), mods_s, P, hyb, att, norm_final,
                             (state_ssd, state_ssd_conv, state_wkv, state_wkv_shift,
                              (cache_win0, cache_win1, cache_win2)), Bs)
    st = jnp.stack
    win_s = _roll_windows((cache_win0, cache_win1, cache_win2), [st(new_s['win'][g]) for g in range(3)])
    return (y_p, y_s, st(new_p['ssd']), st(new_s['ssd']), st(new_p['conv']), st(new_s['conv']),
            st(new_p['wkv']), st(new_s['wkv']), st(new_p['shift']), st(new_s['shift']),
            st(new_p['win'][0]), win_s[0], st(new_p['win'][1]), win_s[1],
            st(new_p['win'][2]), win_s[2])
```

```python
import functools
import math

import numpy as np
import jax
import jax.numpy as jnp
from jax import lax
from jax.experimental import pallas as pl
from jax.experimental.pallas import tpu as pltpu

F32 = jnp.float32
BF16 = jnp.bfloat16
HIGHEST = lax.Precision.HIGHEST

D_MODEL = 1024
DEPTH = 4
PAST_LEN = 8192
NORM_EPS = 1e-6
N_MOD = 6
SSD_HEADS = 16
SSD_HEAD_DIM = 64
SSD_GROUPS = 2
SSD_STATE = 128
SSD_CONV = 4
SSD_CHUNK = 128
RWKV_HEADS = 16
RWKV_HEAD_DIM = 64
RWKV_LN_EPS = 64e-5
ATT_GROUPS = ((128, 1), (512, 4), (2048, 16))
ATT_HEADS = 8
ATT_HEAD_DIM = 64
ATT_DIM = ATT_HEADS * ATT_HEAD_DIM
ATT_Q_BLOCK = 128
ROPE_THETA = 500000.0
ROPE_DIM = ATT_HEAD_DIM // 4
MLP_HIDDEN = 4 * D_MODEL

U_COLS = 6144
U_Z, U_XS, U_R, U_K, U_V, U_BC, U_LW, U_DT = 0, 1024, 2048, 3072, 4096, 5120, 5632, 5888

LANES = 128
VMEM_LIMIT = 48 * 1024 * 1024
PROMPT_ROWS = 1024


def _cparams(*sem):
    return pltpu.CompilerParams(dimension_semantics=sem, vmem_limit_bytes=VMEM_LIMIT)


def _dot(a, b):
    return jnp.dot(a, b, preferred_element_type=F32)


def _dot_exact(a, b):
    return jnp.dot(a, b, preferred_element_type=F32, precision=HIGHEST)


def _dot_exact_nt(a, b):
    return lax.dot_general(a, b, (((1,), (1,)), ((), ())), preferred_element_type=F32, precision=HIGHEST)


def _dot_nt(a, b):
    return lax.dot_general(a, b, (((1,), (1,)), ((), ())), preferred_element_type=F32)


def _silu(x):
    return x * jax.nn.sigmoid(x)


def _softplus(x):
    return jnp.maximum(x, 0.0) + jnp.log1p(jnp.exp(-jnp.abs(x)))


def _block_ones():
    i = np.arange(LANES)
    return jnp.asarray((i[:, None] // 64 == i[None, :] // 64).astype(np.float32))


def _pair_eye():
    i = np.arange(64)
    j = np.arange(LANES)
    return jnp.asarray((i[:, None] == (j[None, :] % 64)).astype(np.float32))


def _head_expand(nheads, width):
    e = np.zeros((LANES, nheads * width), np.float32)
    for h in range(nheads):
        e[h, h * width:(h + 1) * width] = 1.0
    return jnp.asarray(e)


def _tril_ones(n):
    return jnp.asarray(np.tril(np.ones((n, n), np.float32)))


def _linear_kernel(*refs, pro, epi, two):
    refs = list(refs)
    x_ref = refs.pop(0)
    if pro == 'normmod':
        g_ref, sc_ref, sh_ref = refs.pop(0), refs.pop(0), refs.pop(0)
    w_ref = refs.pop(0)
    if two:
        x2_ref, w2_ref = refs.pop(0), refs.pop(0)
    if epi == 'bias':
        b_ref = refs.pop(0)
    if epi == 'resgate':
        res_ref, gate_ref = refs.pop(0), refs.pop(0)
    o_ref = refs.pop(0)

    if pro == 'cast':
        h = x_ref[...].astype(BF16)
    else:
        h_ref = refs.pop(0)

        @pl.when(pl.program_id(1) == 0)
        def _():
            x = x_ref[...].astype(F32)
            if pro == 'silu':
                hh = _silu(x)
            else:
                ms = jnp.mean(x * x, axis=-1, keepdims=True)
                y = (x * lax.rsqrt(ms + NORM_EPS)) * g_ref[...]
                hh = y * (1.0 + sc_ref[...]) + sh_ref[...]
            h_ref[...] = hh.astype(BF16)

        h = h_ref[...]
    acc = _dot(h, w_ref[...])
    if two:
        acc = acc + _dot(x2_ref[...].astype(BF16), w2_ref[...])
    if epi == 'bias':
        acc = acc + b_ref[...]
    elif epi == 'relu2':
        acc = jnp.square(jnp.maximum(acc, 0.0))
    elif epi == 'resgate':
        acc = res_ref[...] + gate_ref[...] * acc
    o_ref[...] = acc.astype(o_ref.dtype)


def _linear(x, w, *, tm, tn, pro='cast', epi='none', norm=None, bias=None, res=None, gate=None,
            bpb=1, out_dtype=F32, second=None):
    M, K = x.shape
    N = w.shape[1]
    assert M % tm == 0 and N % tn == 0
    in_specs = [pl.BlockSpec((tm, K), lambda i, j: (i, 0))]
    args = [x]
    scratch = []
    if pro == 'normmod':
        g, mod, ksc, ksh = norm
        r = mod.shape[1]
        in_specs += [pl.BlockSpec((1, K), lambda i, j: (0, 0)),
                     pl.BlockSpec((None, r, K), lambda i, j: (i // bpb, 0, ksc)),
                     pl.BlockSpec((None, r, K), lambda i, j: (i // bpb, 0, ksh))]
        args += [g, mod, mod]
    if pro != 'cast':
        scratch = [pltpu.VMEM((tm, K), BF16)]
    in_specs.append(pl.BlockSpec((K, tn), lambda i, j: (0, j)))
    args.append(w)
    if second is not None:
        x2, w2 = second
        K2 = x2.shape[1]
        in_specs += [pl.BlockSpec((tm, K2), lambda i, j: (i, 0)), pl.BlockSpec((K2, tn), lambda i, j: (0, j))]
        args += [x2, w2]
    if epi == 'bias':
        in_specs.append(pl.BlockSpec((1, tn), lambda i, j: (0, j)))
        args.append(bias)
    if epi == 'resgate':
        mod, kg = gate
        r = mod.shape[1]
        nj = N // tn
        in_specs += [pl.BlockSpec((tm, tn), lambda i, j: (i, j)),
                     pl.BlockSpec((None, r, tn), lambda i, j: (i // bpb, 0, kg * nj + j))]
        args += [res, mod]
    return pl.pallas_call(
        functools.partial(_linear_kernel, pro=pro, epi=epi, two=second is not None),
        name="linear_%s_%s" % (pro, epi),
        grid=(M // tm, N // tn),
        in_specs=in_specs,
        out_specs=pl.BlockSpec((tm, tn), lambda i, j: (i, j)),
        out_shape=jax.ShapeDtypeStruct((M, N), out_dtype),
        scratch_shapes=scratch,
        compiler_params=_cparams("parallel", "arbitrary"),
    )(*args)


def _rmsnorm_kernel(x_ref, g_ref, o_ref):
    x = x_ref[...]
    ms = jnp.mean(x * x, axis=-1, keepdims=True)
    o_ref[...] = (x * lax.rsqrt(ms + NORM_EPS)) * g_ref[...]


def _rmsnorm(x, g, tm):
    M, K = x.shape
    return pl.pallas_call(
        _rmsnorm_kernel,
        name="final_rmsnorm",
        grid=(M // tm,),
        in_specs=[pl.BlockSpec((tm, K), lambda i: (i, 0)), pl.BlockSpec((1, K), lambda i: (0, 0))],
        out_specs=pl.BlockSpec((tm, K), lambda i: (i, 0)),
        out_shape=jax.ShapeDtypeStruct((M, K), F32),
        compiler_params=_cparams("parallel"),
    )(x, g)


def _ssd_tail(y, xs, z, d_exp, norm_w):
    y = (y + d_exp * xs) * _silu(z)
    half = y.shape[1] // SSD_GROUPS
    outs = []
    for g in range(SSD_GROUPS):
        yg = y[:, g * half:(g + 1) * half]
        ms = jnp.mean(yg * yg, axis=-1, keepdims=True)
        outs.append(yg * lax.rsqrt(ms + NORM_EPS))
    return jnp.concatenate(outs, axis=1) * norm_w


def _ssd_prompt_kernel(z_ref, xs_ref, bc_ref, dt_ref, cw_ref, cb_ref, dtb_ref, alog_ref, dexp_ref,
                       nw_ref, tril_ref, e16_ref, y_ref, st_ref, extx, extbc, state, ybuf):
    c = pl.program_id(1)
    Q = SSD_CHUNK
    NX = SSD_HEADS * SSD_HEAD_DIM

    @pl.when(c == 0)
    def _():
        extx[0:8, :] = jnp.zeros((8, NX), F32)
        extbc[0:8, :] = jnp.zeros((8, 512), F32)
        state[...] = jnp.zeros_like(state)

    extx[8:8 + Q, :] = xs_ref[...]
    extbc[8:8 + Q, :] = bc_ref[...]
    cw = cw_ref[...]
    cb = cb_ref[...]
    xc = cb[:, 0:NX]
    bcc = cb[:, NX:NX + 512]
    for j in range(SSD_CONV):
        xc = xc + extx[pl.ds(5 + j, Q), :] * cw[j:j + 1, 0:NX]
        bcc = bcc + extbc[pl.ds(5 + j, Q), :] * cw[j:j + 1, NX:NX + 512]
    extx[0:8, :] = extx[Q:Q + 8, :]
    extbc[0:8, :] = extbc[Q:Q + 8, :]
    xs = _silu(xc)
    bcs = _silu(bcc)

    dt = _softplus(dt_ref[...] + dtb_ref[...])
    a_neg = -jnp.exp(alog_ref[...])
    acs = _dot_exact(tril_ref[...], dt * a_neg)
    acs_t = acs.T
    e16 = e16_ref[...]
    eacs = jnp.exp(acs)
    dt_exp = _dot_exact(dt, e16)
    eacs_exp = _dot_exact(eacs, e16)
    wend_exp = _dot_exact(jnp.exp(acs[Q - 1:Q, :] - acs) * dt, e16)
    xdt = (xs * dt_exp).astype(BF16)
    xw = (xs * wend_exp).astype(BF16)
    row = lax.broadcasted_iota(jnp.int32, (Q, Q), 0)
    col = lax.broadcasted_iota(jnp.int32, (Q, Q), 1)
    causal = row >= col
    HG = SSD_HEADS // SSD_GROUPS
    GW = HG * SSD_HEAD_DIM
    for g in range(SSD_GROUPS):
        b_g = bcs[:, g * SSD_STATE:(g + 1) * SSD_STATE]
        c_g = bcs[:, 256 + g * SSD_STATE:256 + (g + 1) * SSD_STATE].astype(BF16)
        cb_g = _dot_nt(c_g, b_g.astype(BF16))
        bt_g = b_g.T.astype(BF16)
        for hg in range(HG):
            h = g * HG + hg
            seg = acs[:, h:h + 1] - acs_t[h:h + 1, :]
            decay = jnp.where(causal, jnp.exp(seg), 0.0)
            scores = (cb_g * decay).astype(BF16)
            ybuf[:, h * 64:(h + 1) * 64] = _dot(scores, xdt[:, h * 64:(h + 1) * 64])
        st_g = state[g]
        y_off = _dot(c_g, st_g.astype(BF16)) * eacs_exp[:, g * GW:(g + 1) * GW]
        ybuf[:, g * GW:(g + 1) * GW] = ybuf[:, g * GW:(g + 1) * GW] + y_off
        state[g] = st_g * eacs_exp[Q - 1:Q, g * GW:(g + 1) * GW] + _dot(bt_g, xw[:, g * GW:(g + 1) * GW])

    y_ref[...] = _ssd_tail(ybuf[...], xs, z_ref[...], dexp_ref[...], nw_ref[...]).astype(y_ref.dtype)

    @pl.when(c == pl.num_programs(1) - 1)
    def _():
        st_ref[...] = state[...]


def _ssd_prompt(u, B, L, cw, cb, dtb, alog, dexp, nw):
    Q = SSD_CHUNK
    nc = L // Q
    row = lambda b, c: b * nc + c
    const = lambda shape: pl.BlockSpec(shape, lambda b, c: (0,) * len(shape))
    y, st = pl.pallas_call(
        _ssd_prompt_kernel,
        name="ssd_prompt",
        grid=(B, nc),
        in_specs=[pl.BlockSpec((Q, 1024), lambda b, c: (row(b, c), U_Z // 1024)),
                  pl.BlockSpec((Q, 1024), lambda b, c: (row(b, c), U_XS // 1024)),
                  pl.BlockSpec((Q, 512), lambda b, c: (row(b, c), U_BC // 512)),
                  pl.BlockSpec((Q, 128), lambda b, c: (row(b, c), U_DT // 128)),
                  const((SSD_CONV, 1536)), const((1, 1536)), const((1, 128)), const((1, 128)),
                  const((1, 1024)), const((1, 1024)), const((Q, Q)), const((128, 1024))],
        out_specs=[pl.BlockSpec((Q, 1024), lambda b, c: (row(b, c), 0)),
                   pl.BlockSpec((None, SSD_GROUPS, SSD_STATE, 512), lambda b, c: (b, 0, 0, 0))],
        out_shape=[jax.ShapeDtypeStruct((B * L, 1024), BF16),
                   jax.ShapeDtypeStruct((B, SSD_GROUPS, SSD_STATE, 512), F32)],
        scratch_shapes=[pltpu.VMEM((Q + 8, 1024), F32), pltpu.VMEM((Q + 8, 512), F32),
                        pltpu.VMEM((SSD_GROUPS, SSD_STATE, 512), F32), pltpu.VMEM((Q, 1024), F32)],
        compiler_params=_cparams("parallel", "arbitrary"),
    )(u, u, u, u, cw, cb, dtb, alog, dexp, nw, _tril_ones(Q), _head_expand(SSD_HEADS, 64))
    st = st.reshape(B, SSD_GROUPS, SSD_STATE, SSD_HEADS // SSD_GROUPS, SSD_HEAD_DIM)
    st = jnp.transpose(st, (0, 1, 3, 4, 2)).reshape(B, SSD_HEADS, SSD_HEAD_DIM, SSD_STATE)
    return y, st


def _ssd_step_kernel(z_ref, xs_ref, bc_ref, dt_ref, cx_ref, cbc_ref, s_ref, cw_ref, cb_ref, dtb_ref,
                     alog_ref, dexp_ref, nw_ref, e2_ref, y_ref, so_ref, ybuf):
    NX = SSD_HEADS * SSD_HEAD_DIM
    cw = cw_ref[...]
    cb = cb_ref[...]
    cx = cx_ref[...]
    cbc = cbc_ref[...]
    xc = cb[:, 0:NX] + xs_ref[...] * cw[3:4, 0:NX]
    bcc = cb[:, NX:NX + 512] + bc_ref[...] * cw[3:4, NX:NX + 512]
    for j in range(SSD_CONV - 1):
        xc = xc + cx[j:j + 1, :] * cw[j:j + 1, 0:NX]
        bcc = bcc + cbc[j:j + 1, :] * cw[j:j + 1, NX:NX + 512]
    xs = _silu(xc)
    bcs = _silu(bcc)
    dt = _softplus(dt_ref[...] + dtb_ref[...])
    da = jnp.exp(dt * (-jnp.exp(alog_ref[...])))
    e2 = e2_ref[...]
    lane = lax.broadcasted_iota(jnp.int32, (64, LANES), 1)
    first = lane < 64
    HG = SSD_HEADS // SSD_GROUPS
    for q in range(SSD_HEADS // 2):
        xrow = xs[:, q * LANES:(q + 1) * LANES]
        diag = e2 * xrow
        ycols = []
        for s in range(2):
            h = 2 * q + s
            g = h // HG
            xcol = jnp.sum(jnp.where(first == (s == 0), diag, 0.0), axis=1, keepdims=True)
            b_row = bcs[:, g * SSD_STATE:(g + 1) * SSD_STATE]
            c_row = bcs[:, 256 + g * SSD_STATE:256 + (g + 1) * SSD_STATE]
            s_new = s_ref[h] * da[:, h:h + 1] + (xcol * dt[:, h:h + 1]) * b_row
            so_ref[h] = s_new
            ycols.append(jnp.sum(s_new * c_row, axis=1, keepdims=True))
        ypair = jnp.where(first, ycols[0], ycols[1])
        ybuf[:, q * LANES:(q + 1) * LANES] = jnp.sum(e2 * ypair, axis=0, keepdims=True)
    y_ref[...] = _ssd_tail(ybuf[...], xs, z_ref[...], dexp_ref[...], nw_ref[...]).astype(y_ref.dtype)


def _ssd_step(u, conv_x, conv_bc, s0, cw, cb, dtb, alog, dexp, nw):
    B = u.shape[0]
    const = lambda shape: pl.BlockSpec(shape, lambda b: (0,) * len(shape))
    return pl.pallas_call(
        _ssd_step_kernel,
        name="ssd_step",
        grid=(B,),
        in_specs=[pl.BlockSpec((None, 1, 1024), lambda b: (b, 0, U_Z // 1024)),
                  pl.BlockSpec((None, 1, 1024), lambda b: (b, 0, U_XS // 1024)),
                  pl.BlockSpec((None, 1, 512), lambda b: (b, 0, U_BC // 512)),
                  pl.BlockSpec((None, 1, 128), lambda b: (b, 0, U_DT // 128)),
                  pl.BlockSpec((None, 3, 1024), lambda b: (b, 0, 0)),
                  pl.BlockSpec((None, 3, 512), lambda b: (b, 0, 0)),
                  pl.BlockSpec((None, SSD_HEADS, 64, 128), lambda b: (b, 0, 0, 0)),
                  const((SSD_CONV, 1536)), const((1, 1536)), const((1, 128)), const((1, 128)),
                  const((1, 1024)), const((1, 1024)), const((64, 128))],
        out_specs=[pl.BlockSpec((None, 1, 1024), lambda b: (b, 0, 0)),
                   pl.BlockSpec((None, SSD_HEADS, 64, 128), lambda b: (b, 0, 0, 0))],
        out_shape=[jax.ShapeDtypeStruct((B, 1, 1024), BF16),
                   jax.ShapeDtypeStruct(s0.shape, F32)],
        scratch_shapes=[pltpu.VMEM((1, 1024), F32)],
        compiler_params=_cparams("parallel"),
    )(u, u, u, u, conv_x, conv_bc, s0, cw, cb, dtb, alog, dexp, nw, _pair_eye())


def _rwkv_prep_kernel(*refs, shifted, bpb):
    refs = list(refs)
    cur = [refs.pop(0) for _ in range(4)]
    prev = [refs.pop(0) for _ in range(4)]
    if shifted:
        first = [refs.pop(0) for _ in range(4)]
    mu = [refs.pop(0) for _ in range(4)]
    (w0_ref, w2_ref, a0_ref, a2_ref, g2_ref, kk_ref, ka_ref, bo_ref) = [refs.pop(0) for _ in range(8)]
    (r_o, w_o, k_o, v_o, kn_o, kka_o, g_o) = refs
    i = pl.program_id(0)

    def mixed(n):
        x = cur[n][...]
        if shifted:
            rolled = pltpu.roll(x, 1, 0)
            before = jnp.where(i % bpb == 0, first[n][...], prev[n][7:8, :])
            rid = lax.broadcasted_iota(jnp.int32, x.shape, 0)
            p = jnp.where(rid == 0, before, rolled)
        else:
            p = prev[n][...]
        return x + (p - x) * mu[n][...]

    r, k, v, lw = mixed(0), mixed(1), mixed(2), mixed(3)
    blk = lw[:, 0:LANES]
    lane = lax.broadcasted_iota(jnp.int32, blk.shape, 1)
    tw = jnp.where(lane < 64, jnp.tanh(blk), blk).astype(BF16)
    wpre = w0_ref[...] + _dot(tw, w2_ref[...])
    apre = a0_ref[...] + _dot(tw, a2_ref[...])
    wlog = -_softplus(-wpre) - 0.5
    a = jax.nn.sigmoid(apre)
    g = _dot(jax.nn.sigmoid(lw[:, LANES:2 * LANES]).astype(BF16), g2_ref[...])
    kk = k * kk_ref[...]
    kk2 = kk * kk
    bo = bo_ref[...]
    for q in range(RWKV_HEADS // 2):
        sl = slice(q * LANES, (q + 1) * LANES)
        n2 = _dot_exact(kk2[:, sl], bo)
        kn = kk[:, sl] / jnp.maximum(jnp.sqrt(n2), 1e-12)
        kn_o[:, sl] = -kn
        kka_o[:, sl] = kn * a[:, sl]
    r_o[...] = r
    w_o[...] = -jnp.exp(wlog)
    k_o[...] = k * (1.0 + (a - 1.0) * ka_ref[...])
    v_o[...] = v
    g_o[...] = g


def _rwkv_prep(u, prev, first, mus, ws, *, tm, bpb, shifted):
    M = u.shape[0]
    cols = [(1024, U_R // 1024), (1024, U_K // 1024), (1024, U_V // 1024), (256, U_LW // 256)]
    in_specs = [pl.BlockSpec((tm, c), functools.partial(lambda i, kb: (i, kb), kb=kb)) for c, kb in cols]
    args = [u] * 4
    if shifted:
        in_specs += [pl.BlockSpec((8, c), functools.partial(
            lambda i, kb: (jnp.maximum(i * (tm // 8) - 1, 0), kb), kb=kb)) for c, kb in cols]
        args += [u] * 4
        in_specs += [pl.BlockSpec((None, 1, c), lambda i: (i // bpb, 0, 0)) for c, _ in cols]
        args += list(first)
    else:
        in_specs += [pl.BlockSpec((tm, c), lambda i: (i, 0)) for c, _ in cols]
        args += list(prev)
    in_specs += [pl.BlockSpec((1, c), lambda i: (0, 0)) for c, _ in cols]
    args += list(mus)
    wshapes = [(1, 1024), (128, 1024), (1, 1024), (128, 1024), (128, 1024), (1, 1024), (1, 1024), (128, 128)]
    in_specs += [pl.BlockSpec(s, lambda i: (0, 0)) for s in wshapes]
    args += list(ws)
    return pl.pallas_call(
        functools.partial(_rwkv_prep_kernel, shifted=shifted, bpb=bpb),
        name="rwkv_prep",
        grid=(M // tm,),
        in_specs=in_specs,
        out_specs=[pl.BlockSpec((tm, 1024), lambda i: (i, 0))] * 7,
        out_shape=[jax.ShapeDtypeStruct((M, 1024), F32)] * 7,
        compiler_params=_cparams("parallel"),
    )(*args)


def _wkv_step_kernel(r_ref, lw_ref, k_ref, v_ref, kn_ref, ka_ref, s0_ref, e2_ref, bo_ref, o_ref, sT_ref, *, nb):
    e2 = e2_ref[...]
    bo = bo_ref[...]
    first = lax.broadcasted_iota(jnp.int32, (64, LANES), 1) < 64

    def pair_sum(x):
        sa = jnp.sum(jnp.where(first, x, 0.0), axis=1, keepdims=True)
        sb = jnp.sum(jnp.where(first, 0.0, x), axis=1, keepdims=True)
        return jnp.where(first, sa, sb)

    for b in range(nb):
        for p in range(RWKV_HEADS // 2):
            sl = slice(p * LANES, (p + 1) * LANES)
            S = s0_ref[b, p]
            sa = pair_sum(S * kn_ref[b, :, sl])
            vcol = _dot_exact(e2 * v_ref[b, :, sl], bo)
            S = S * jnp.exp(lw_ref[b, :, sl]) + sa * ka_ref[b, :, sl] + vcol * k_ref[b, :, sl]
            sT_ref[b, p] = S
            o = pair_sum(S * r_ref[b, :, sl])
            o_ref[b, :, sl] = jnp.sum(e2 * o, axis=0, keepdims=True)


def _wkv_step(r, lw, k, v, kn, ka, s0, *, nb):
    B = r.shape[0]
    s0p = s0.reshape(B, 8, 2, 64, 64).transpose(0, 1, 3, 2, 4).reshape(B, 8, 64, 128)
    seq = pl.BlockSpec((nb, 1, 1024), lambda b: (b, 0, 0))
    stt = pl.BlockSpec((nb, 8, 64, 128), lambda b: (b, 0, 0, 0))
    o, sT = pl.pallas_call(
        functools.partial(_wkv_step_kernel, nb=nb),
        grid=(B // nb,),
        in_specs=[seq] * 6 + [stt, pl.BlockSpec((64, 128), lambda b: (0, 0)),
                              pl.BlockSpec((128, 128), lambda b: (0, 0))],
        out_specs=[seq, stt],
        out_shape=[jax.ShapeDtypeStruct((B, 1, 1024), F32), jax.ShapeDtypeStruct((B, 8, 64, 128), F32)],
        compiler_params=_cparams("parallel"),
        name="wkv_step",
    )(r, lw, k, v, kn, ka, s0p, _pair_eye(), _block_ones())
    sT = sT.reshape(B, 8, 64, 2, 64).transpose(0, 1, 3, 2, 4).reshape(B, 16, 64, 64)
    return o, sT


WKV_CHUNK = 64


def _wkv_chunk_kernel(r_ref, lw_ref, k_ref, v_ref, kn_ref, ka_ref, tril_ref, o_ref, sT_ref, S_ref, *, nb):
    C = WKV_CHUNK
    c = pl.program_id(1)

    @pl.when(c == 0)
    def _():
        S_ref[...] = jnp.zeros_like(S_ref)

    tril = tril_ref[...]
    lane = lax.broadcasted_iota(jnp.int32, (C, LANES), 1)
    rowi = lax.broadcasted_iota(jnp.int32, (C, LANES), 0)
    first = lane < 64
    strict = rowi > (lane % 64)
    incl = rowi >= (lane % 64)
    r128 = lax.broadcasted_iota(jnp.int32, (LANES, LANES), 0)
    c128 = lax.broadcasted_iota(jnp.int32, (LANES, LANES), 1)
    diag_blocks = (r128 < 64) == (c128 < 64)
    eye = r128 == c128

    def bd(x):
        return jnp.concatenate([jnp.where(first, x, 0.0), jnp.where(first, 0.0, x)], axis=0)

    bf = lambda x: x.astype(BF16)
    pairs = [(b, p) for b in range(nb) for p in range(RWKV_HEADS // 2)]
    sls = [slice(p * LANES, (p + 1) * LANES) for _, p in pairs]
    load = lambda ref: [ref[b, :, sl] for (b, _), sl in zip(pairs, sls)]
    each = lambda f, *ls: [f(*a) for a in zip(*ls)]
    r_, lw, kt, vv, al, be = (load(ref) for ref in (r_ref, lw_ref, k_ref, v_ref, kn_ref, ka_ref))
    cs = each(lambda x: _dot_exact(tril, x), lw)
    last = each(lambda x: x[C - 1:C, :], cs)
    e_inv = each(lambda x: jnp.exp(-x), cs)
    aq = each(lambda a, x, l: a * jnp.exp(x - l), al, cs, lw)
    rq = each(lambda a, x: a * jnp.exp(x), r_, cs)
    bk = each(jnp.multiply, be, e_inv)
    kk = each(jnp.multiply, kt, e_inv)
    g = each(lambda a, q, b_, k_: _dot_exact_nt(
        jnp.concatenate([a, q], axis=0),
        jnp.concatenate([jnp.where(first, b_, 0.0), jnp.where(first, 0.0, b_),
                         jnp.where(first, k_, 0.0), jnp.where(first, 0.0, k_)], axis=0)), aq, rq, bk, kk)
    m1 = each(lambda x: bf(jnp.where(strict, x[0:C, 0:LANES], 0.0)), g)
    m2 = each(lambda x: bf(jnp.where(strict, x[0:C, LANES:2 * LANES], 0.0)), g)
    n1 = each(lambda x: bf(jnp.where(incl, x[C:2 * C, 0:LANES], 0.0)), g)
    n2 = each(lambda x: bf(jnp.where(incl, x[C:2 * C, LANES:2 * LANES], 0.0)), g)
    s0 = [S_ref[b, p] for b, p in pairs]
    s0b = each(bf, s0)
    vbd = each(lambda x: bf(bd(x)), vv)
    x = each(lambda a, s, m, v_: _dot(bf(a), s) + _dot(m, v_), aq, s0b, m2, vbd)
    mp = m1
    steps = int(math.log2(C))
    for i in range(steps):
        x = each(lambda x_, m: x_ + _dot(m, bf(bd(x_))), x, mp)
        if i + 1 < steps:
            mp = each(lambda m: bf(_dot(m, bd(m))), mp)
    o = each(lambda q, s, a, x_, b_, v_: _dot(bf(q), s) + _dot(a, bf(bd(x_))) + _dot(b_, v_),
             rq, s0b, n1, x, n2, vbd)
    for (b, _), sl, o_ in zip(pairs, sls, o):
        o_ref[b, :, sl] = o_
    e_end = each(lambda l, x_: jnp.exp(l - x_), last, cs)
    kv_t = each(lambda b_, k_, e: jnp.concatenate([b_ * e, k_ * e], axis=0).T, be, kt, e_end)
    upd = each(lambda t, x_, v_: _dot(bf(t), bf(jnp.concatenate([x_, v_], axis=0))), kv_t, x, vv)
    gcol = each(lambda l: jnp.sum(jnp.where(eye, jnp.exp(l), 0.0), axis=1, keepdims=True), last)
    for (b, p), s, u, gc in zip(pairs, s0, upd, gcol):
        S_ref[b, p] = jnp.where(diag_blocks, gc * s + u, 0.0)

    @pl.when(c == pl.num_programs(1) - 1)
    def _():
        sT_ref[...] = S_ref[...]


def _wkv_chunked(r, lw, k, v, kn, ka, *, nb):
    B, L, _ = r.shape
    C = WKV_CHUNK
    seq = pl.BlockSpec((nb, C, 1024), lambda b, c: (b, c, 0))
    stt = pl.BlockSpec((nb, 8, LANES, LANES), lambda b, c: (b, 0, 0, 0))
    o, sT = pl.pallas_call(
        functools.partial(_wkv_chunk_kernel, nb=nb),
        grid=(B // nb, L // C),
        in_specs=[seq] * 6 + [pl.BlockSpec((C, C), lambda b, c: (0, 0))],
        out_specs=[seq, stt],
        out_shape=[jax.ShapeDtypeStruct((B, L, 1024), F32), jax.ShapeDtypeStruct((B, 8, LANES, LANES), F32)],
        scratch_shapes=[pltpu.VMEM((nb, 8, LANES, LANES), F32)],
        compiler_params=_cparams("parallel", "arbitrary"),
        name="wkv_chunked",
    )(r, lw, k, v, kn, ka, _tril_ones(C))
    blocks = jnp.stack([sT[:, :, 0:64, 0:64], sT[:, :, 64:128, 64:128]], axis=2)
    return o, jnp.swapaxes(blocks, -1, -2).reshape(B, 16, 64, 64)


def _rwkv_post_kernel(o_ref, r_ref, k_ref, v_ref, g_ref, lnw_ref, lnb_ref, rk_ref, bo_ref, y_ref):
    bo = bo_ref[...]
    inv = 1.0 / RWKV_HEAD_DIM
    for q in range(RWKV_HEADS // 2):
        sl = slice(q * LANES, (q + 1) * LANES)
        o = o_ref[:, sl]
        mean = _dot_exact(o, bo) * inv
        d = o - mean
        var = _dot_exact(d * d, bo) * inv
        on = d * lax.rsqrt(var + RWKV_LN_EPS) * lnw_ref[:, sl] + lnb_ref[:, sl]
        bonus = _dot_exact(r_ref[:, sl] * k_ref[:, sl] * rk_ref[:, sl], bo) * v_ref[:, sl]
        y_ref[:, sl] = ((on + bonus) * g_ref[:, sl]).astype(y_ref.dtype)


def _rwkv_post(o, r, k, v, g, lnw, lnb, rk, tm):
    M = o.shape[0]
    blk = pl.BlockSpec((tm, 1024), lambda i: (i, 0))
    vec = pl.BlockSpec((1, 1024), lambda i: (0, 0))
    return pl.pallas_call(
        _rwkv_post_kernel,
        name="rwkv_post",
        grid=(M // tm,),
        in_specs=[blk] * 5 + [vec] * 3 + [pl.BlockSpec((128, 128), lambda i: (0, 0))],
        out_specs=blk,
        out_shape=jax.ShapeDtypeStruct((M, 1024), BF16),
        compiler_params=_cparams("parallel"),
    )(o, r, k, v, g, lnw, lnb, rk, _block_ones())


def _rope_tables(pos):
    half = ROPE_DIM // 2
    inv = ROPE_THETA ** (-jnp.arange(half, dtype=F32) * 2.0 / ROPE_DIM)
    ang = pos.astype(F32)[:, None] * inv
    cos, sin = jnp.cos(ang), jnp.sin(ang)
    n = pos.shape[0]
    rest = ATT_HEAD_DIM - ROPE_DIM
    c = jnp.concatenate([cos, cos, jnp.ones((n, rest), F32)], axis=1)
    s_next = jnp.concatenate([-sin, jnp.zeros((n, half + rest), F32)], axis=1)
    s_prev = jnp.concatenate([jnp.zeros((n, half), F32), sin, jnp.zeros((n, rest), F32)], axis=1)
    return tuple(jnp.concatenate([t, t], axis=1) for t in (c, s_next, s_prev))


def _rope_apply(x, c, s_next, s_prev):
    n = x.shape[1]
    reps = n // LANES
    tile = lambda t: jnp.concatenate([t] * reps, axis=1)
    half = ROPE_DIM // 2
    return x * tile(c) + pltpu.roll(x, n - half, 1) * tile(s_next) + pltpu.roll(x, half, 1) * tile(s_prev)


def _attn_prompt_kernel(q_ref, kc_ref, kp_ref, vc_ref, vp_ref, cc_ref, snc_ref, spc_ref, cp_ref, snp_ref,
                        spp_ref, o_ref, lse_ref, kr_ref, q_s, k_s, *, d):
    i = pl.program_id(0)
    QB = ATT_Q_BLOCK
    q_s[...] = _rope_apply(q_ref[...], cc_ref[...], snc_ref[...], spc_ref[...])
    kc = _rope_apply(kc_ref[...], cc_ref[...], snc_ref[...], spc_ref[...])
    kr_ref[...] = kc
    k_s[1] = kc
    k_s[0] = _rope_apply(kp_ref[...], cp_ref[...], snp_ref[...], spp_ref[...])
    row = lax.broadcasted_iota(jnp.int32, (QB, 2 * QB), 0)
    col = lax.broadcasted_iota(jnp.int32, (QB, 2 * QB), 1)
    valid = (col >= row) & (col <= row + QB) & ((i > 0) | (col >= QB))
    scale = ATT_HEAD_DIM ** -0.5
    for rho in range(d):
        rows = pl.ds(rho, QB, stride=d) if d > 1 else pl.ds(0, QB)
        qb = q_s[rows, :].astype(BF16)
        k2 = jnp.concatenate([k_s[0, rows, :], k_s[1, rows, :]], axis=0).astype(BF16)
        v2 = jnp.concatenate([vp_ref[rows, :], vc_ref[rows, :]], axis=0).astype(BF16)
        outs, lses = [], []
        for h in range(LANES // ATT_HEAD_DIM):
            sl = slice(h * ATT_HEAD_DIM, (h + 1) * ATT_HEAD_DIM)
            s = _dot_nt(qb[:, sl], k2[:, sl]) * scale
            s = jnp.where(valid, s, -jnp.inf)
            m = jnp.max(s, axis=1, keepdims=True)
            p = jnp.exp(s - m)
            l = jnp.sum(p, axis=1, keepdims=True)
            outs.append(_dot(p.astype(BF16), v2[:, sl]) / l)
            lses.append(jnp.broadcast_to(m + jnp.log(l), (QB, ATT_HEAD_DIM)))
        o_ref[rows, :] = jnp.concatenate(outs, axis=1)
        lse_ref[rows, :] = jnp.concatenate(lses, axis=1)


def _attn_prompt(qkv, tabs, B, L, gi):
    window, d = ATT_GROUPS[gi]
    assert window == ATT_Q_BLOCK * d
    R = ATT_Q_BLOCK * d
    nblk = L // R
    npair = ATT_DIM // LANES
    prev = lambda i: jnp.maximum(i - 1, 0)

    def col(which, f):
        return lambda i, b, hp: (b * nblk + f(i), gi * 3 * npair + which * npair + hp)

    same = lambda i: i
    blk = lambda f: pl.BlockSpec((R, LANES), f)
    tcur = pl.BlockSpec((R, LANES), lambda i, b, hp: (i, 0))
    tprev = pl.BlockSpec((R, LANES), lambda i, b, hp: (prev(i), 0))
    out = pl.BlockSpec((R, LANES), lambda i, b, hp: (b * nblk + i, hp))
    return pl.pallas_call(
        functools.partial(_attn_prompt_kernel, d=d),
        grid=(nblk, B, npair),
        in_specs=[blk(col(0, same)), blk(col(1, same)), blk(col(1, prev)), blk(col(2, same)), blk(col(2, prev)),
                  tcur, tcur, tcur, tprev, tprev, tprev],
        out_specs=[out, out, out],
        out_shape=[jax.ShapeDtypeStruct((B * L, ATT_DIM), F32)] * 3,
        scratch_shapes=[pltpu.VMEM((R, LANES), F32), pltpu.VMEM((2, R, LANES), F32)],
        compiler_params=_cparams("arbitrary", "arbitrary", "arbitrary"),
        name="attn_prompt_d%d" % d,
    )(qkv, qkv, qkv, qkv, qkv, *tabs, *tabs)


def _attn_out_kernel(o0, l0, o1, l1, o2, l2, w_ref, res_ref, gate_ref, out_ref, h_ref):
    @pl.when(pl.program_id(1) == 0)
    def _():
        m = jnp.maximum(jnp.maximum(l0[...], l1[...]), l2[...])
        a0, a1, a2 = jnp.exp(l0[...] - m), jnp.exp(l1[...] - m), jnp.exp(l2[...] - m)
        o = (a0 * o0[...] + a1 * o1[...] + a2 * o2[...]) / (a0 + a1 + a2)
        h_ref[...] = o.astype(BF16)

    out_ref[...] = res_ref[...] + gate_ref[...] * _dot(h_ref[...], w_ref[...])


def _attn_out(ols, w, res, mod, kg, *, tm, tn, bpb):
    M = res.shape[0]
    N = w.shape[1]
    r = mod.shape[1]
    nj = N // tn
    part = pl.BlockSpec((tm, 512), lambda i, j: (i, 0))
    return pl.pallas_call(
        _attn_out_kernel,
        name="attn_out",
        grid=(M // tm, nj),
        in_specs=[part] * 6 + [pl.BlockSpec((512, tn), lambda i, j: (0, j)),
                               pl.BlockSpec((tm, tn), lambda i, j: (i, j)),
                               pl.BlockSpec((None, r, tn), lambda i, j: (i // bpb, 0, kg * nj + j))],
        out_specs=pl.BlockSpec((tm, tn), lambda i, j: (i, j)),
        out_shape=jax.ShapeDtypeStruct((M, N), F32),
        scratch_shapes=[pltpu.VMEM((tm, 512), BF16)],
        compiler_params=_cparams("parallel", "arbitrary"),
    )(*ols, w, res, mod)


def _rope_step_kernel(qkv_ref, tc_ref, tn_ref, tp_ref, q_ref, k_ref):
    tabs = (tc_ref[...], tn_ref[...], tp_ref[...])
    for gi in range(len(ATT_GROUPS)):
        base = gi * 3 * ATT_DIM
        out = slice(gi * ATT_DIM, (gi + 1) * ATT_DIM)
        q_ref[:, out] = _rope_apply(qkv_ref[:, base:base + ATT_DIM], *tabs)
        k_ref[:, out] = _rope_apply(qkv_ref[:, base + ATT_DIM:base + 2 * ATT_DIM], *tabs)


def _rope_step(qkv, tabs):
    B = qkv.shape[0]
    n = len(ATT_GROUPS) * ATT_DIM
    full = lambda shape: pl.BlockSpec(shape, lambda: (0,) * len(shape))
    return pl.pallas_call(
        _rope_step_kernel,
        in_specs=[full(qkv.shape)] + [full((1, LANES))] * 3,
        out_specs=[full((B, n))] * 2,
        out_shape=[jax.ShapeDtypeStruct((B, n), F32)] * 2,
        name="rope_step",
    )(qkv, *tabs)


def _attn_step_kernel(q_ref, kn_ref, vn_ref, c0_ref, c1_ref, c2_ref, o_ref):
    caches = (c0_ref, c1_ref, c2_ref)
    scale = ATT_HEAD_DIM ** -0.5
    parts = []
    for gi in range(len(ATT_GROUPS)):
        q, kn, vn = q_ref[gi], kn_ref[gi], vn_ref[gi]
        kc = caches[gi][:, 0]
        vc = caches[gi][:, 1]
        s = jnp.sum(kc * q[None], axis=-1, keepdims=True) * scale
        s_new = jnp.sum(q * kn, axis=-1, keepdims=True) * scale
        m = jnp.maximum(jnp.max(s, axis=0), s_new)
        p = jnp.exp(s - m[None])
        p_new = jnp.exp(s_new - m)
        l = jnp.sum(p, axis=0) + p_new
        acc = jnp.sum(p * vc, axis=0) + p_new * vn
        parts.append((acc / l, m + jnp.log(l)))
    mm = jnp.maximum(jnp.maximum(parts[0][1], parts[1][1]), parts[2][1])
    ws = [jnp.exp(lse - mm) for _, lse in parts]
    o_ref[...] = (ws[0] * parts[0][0] + ws[1] * parts[1][0] + ws[2] * parts[2][0]) / (ws[0] + ws[1] + ws[2])


def _visible_cache_rows(caches):
    out = []
    for (window, d), c in zip(ATT_GROUPS, caches):
        assert c.shape[2] == window and window // d == 128
        out.append(c[:, :, ::d])
    return out


def _attn_step(q3, kn3, vn3, seen, layer):
    B = q3.shape[0]
    specs = [pl.BlockSpec((None, None) + c.shape[2:], lambda b: (layer, b, 0, 0, 0, 0)) for c in seen]
    new = pl.BlockSpec((None, len(ATT_GROUPS), ATT_HEADS, ATT_HEAD_DIM), lambda b: (b, 0, 0, 0))
    return pl.pallas_call(
        _attn_step_kernel,
        grid=(B,),
        in_specs=[new] * 3 + specs,
        out_specs=pl.BlockSpec((None, ATT_HEADS, ATT_HEAD_DIM), lambda b: (b, 0, 0)),
        out_shape=jax.ShapeDtypeStruct((B, ATT_HEADS, ATT_HEAD_DIM), F32),
        compiler_params=_cparams("parallel"),
        name="attn_step",
    )(q3, kn3, vn3, *seen)


def _hyb_params(i, hyb_w_in, hyb_w_out, ssd_conv_w, ssd_conv_b, ssd_dt_bias, ssd_a_log, ssd_d, ssd_norm_w,
                rwkv_mu, rwkv_w0, rwkv_w2, rwkv_a0, rwkv_a2, rwkv_g2, rwkv_k_k, rwkv_k_a, rwkv_r_k,
                rwkv_ln_w, rwkv_ln_b):
    w = hyb_w_in[i]
    rw0 = 2576
    w_perm = jnp.concatenate(
        [w[:, 0:1024], w[:, 1024:2048], w[:, rw0:rw0 + 3072], w[:, 2048:2560], w[:, rw0 + 3072:rw0 + 3328],
         w[:, 2560:2576], jnp.zeros((D_MODEL, U_COLS - U_DT - 16), F32)], axis=1).astype(BF16)
    pad128 = lambda v: jnp.concatenate([v, jnp.zeros((LANES - v.shape[0],), F32)])[None, :]
    z64 = jnp.zeros((64, 1024), F32)
    mu = rwkv_mu[i]
    p = dict(
        w_in=w_perm, w_out=hyb_w_out[i].astype(BF16),
        cw=ssd_conv_w[i], cb=ssd_conv_b[i][None, :], dtb=pad128(ssd_dt_bias[i]), alog=pad128(ssd_a_log[i]),
        dexp=jnp.repeat(ssd_d[i], SSD_HEAD_DIM)[None, :], nw=ssd_norm_w[i][None, :],
        mus=[mu[None, 0:1024], mu[None, 1024:2048], mu[None, 2048:3072], mu[None, 3072:3328]],
        prep_w=[rwkv_w0[i][None, :], jnp.concatenate([rwkv_w2[i], z64]).astype(BF16),
                rwkv_a0[i][None, :], jnp.concatenate([z64, rwkv_a2[i]]).astype(BF16),
                rwkv_g2[i].astype(BF16), rwkv_k_k[i][None, :], rwkv_k_a[i][None, :], _block_ones()],
        lnw=rwkv_ln_w[i][None, :], lnb=rwkv_ln_b[i][None, :], rk=rwkv_r_k[i].reshape(1, 1024),
    )
    return p


def _raw_conv_rows(u_rows):
    return jnp.concatenate([u_rows[..., U_XS:U_XS + 1024], u_rows[..., U_BC:U_BC + 512]], axis=-1)


def _raw_rw_rows(u_rows):
    return jnp.concatenate([u_rows[..., U_R:U_R + 3072], u_rows[..., U_LW:U_LW + 256]], axis=-1)


def _run_prompt(x, mods, P, hyb, att, norm_final, B, L):
    T = B * L
    big, half = PROMPT_ROWS, PROMPT_ROWS // 2
    lin = lambda *a, tm, **kw: _linear(*a, tm=tm, bpb=L // tm, **kw)
    new = dict(ssd=[], conv=[], wkv=[], shift=[], win=[[], [], []])
    tabs = _rope_tables(jnp.arange(L))
    for l in range(DEPTH):
        mod = mods[l]
        i = l // 2
        gmix = P['norm_mix'][l][None, :]
        if l % 2 == 0:
            hp = hyb[i]
            u = lin(x, hp['w_in'], tm=big, tn=1024, pro='normmod', norm=(gmix, mod, 1, 0))
            y_ssd, s_ssd = _ssd_prompt(u, B, L, hp['cw'], hp['cb'], hp['dtb'], hp['alog'], hp['dexp'], hp['nw'])
            zeros = [jnp.zeros((B, 1, c), F32) for c in (1024, 1024, 1024, 256)]
            r, w, k, v, kn, ka, g = _rwkv_prep(u, None, zeros, hp['mus'], hp['prep_w'], tm=256, bpb=L // 256,
                                               shifted=True)
            sh = lambda t: t.reshape(B, L, 1024)
            o, s_wkv = _wkv_chunked(sh(r), sh(w), sh(k), sh(v), sh(kn), sh(ka), nb=B)
            y_rwkv = _rwkv_post(o.reshape(T, 1024), r, k, v, g, hp['lnw'], hp['lnb'], hp['rk'], 256)
            x = lin(y_ssd, hp['w_out'][:1024], tm=big, tn=1024, epi='resgate', res=x, gate=(mod, 2),
                    second=(y_rwkv, hp['w_out'][1024:]))
            u3 = u.reshape(B, L, U_COLS)
            new['ssd'].append(s_ssd)
            new['conv'].append(_raw_conv_rows(u3[:, L - (SSD_CONV - 1):]))
            new['wkv'].append(s_wkv)
            new['shift'].append(_raw_rw_rows(u3[:, L - 1]))
        else:
            ap = att[i]
            qkv = lin(x, ap['w_qkv'], tm=big, tn=1536, pro='normmod', norm=(gmix, mod, 1, 0))
            ols = []
            q3 = qkv.reshape(B, L, 4608)
            for gi, (window, d) in enumerate(ATT_GROUPS):
                o, lse, kr = _attn_prompt(qkv, tabs, B, L, gi)
                ols += [o, lse]
                keep = min(window, L)
                kk = kr.reshape(B, L, 512)[:, L - keep:]
                vv = q3[:, L - keep:, gi * 1536 + 1024:gi * 1536 + 1536]
                new['win'][gi].append(jnp.stack([kk, vv], axis=2).reshape(B, keep, 2, ATT_HEADS, ATT_HEAD_DIM))
            x = _attn_out(ols, ap['w_out'], x, mod, 2, tm=half, tn=1024, bpb=L // half)
        gmlp = P['norm_mlp'][l][None, :]
        hid = lin(x, P['w1'][l], tm=big, tn=1024, pro='normmod', epi='relu2', norm=(gmlp, mod, 4, 3),
                  out_dtype=BF16)
        x = lin(hid, P['w2'][l], tm=half, tn=1024, epi='resgate', res=x, gate=(mod, 5))
    y = _rmsnorm(x, norm_final[None, :], half).reshape(B, L, D_MODEL)
    return y, new


def _run_sample(x, mods, P, hyb, att, norm_final, states, B):
    state_ssd, state_conv, state_wkv, state_shift, caches = states
    tm = B
    new = dict(ssd=[], conv=[], wkv=[], shift=[], win=[[], [], []])
    tabs = _rope_tables(jnp.full((1,), PAST_LEN, jnp.int32))
    seen = _visible_cache_rows(caches)
    for l in range(DEPTH):
        mod = mods[l]
        i = l // 2
        gmix = P['norm_mix'][l][None, :]
        if l % 2 == 0:
            hp = hyb[i]
            u = _linear(x, hp['w_in'], tm=tm, tn=512, pro='normmod', norm=(gmix, mod, 1, 0))
            u3 = u.reshape(B, 1, U_COLS)
            cbuf = state_conv[i]
            y_ssd, s_ssd = _ssd_step(u3, cbuf[:, :, 0:1024], cbuf[:, :, 1024:1536], state_ssd[i],
                                     hp['cw'], hp['cb'], hp['dtb'], hp['alog'], hp['dexp'], hp['nw'])
            sb = state_shift[i]
            prev = [sb[:, 0:1024], sb[:, 1024:2048], sb[:, 2048:3072], sb[:, 3072:3328]]
            r, w, k, v, kn, ka, g = _rwkv_prep(u, prev, None, hp['mus'], hp['prep_w'], tm=tm, bpb=1,
                                               shifted=False)
            sh = lambda t: t.reshape(B, 1, 1024)
            o, s_wkv = _wkv_step(sh(r), sh(w), sh(k), sh(v), sh(kn), sh(ka), state_wkv[i], nb=2)
            y_rwkv = _rwkv_post(o.reshape(B, 1024), r, k, v, g, hp['lnw'], hp['lnb'], hp['rk'], tm)
            x = _linear(y_ssd.reshape(B, 1024), hp['w_out'][:1024], tm=tm, tn=512, epi='resgate', res=x,
                        gate=(mod, 2), second=(y_rwkv, hp['w_out'][1024:]))
            new['ssd'].append(s_ssd)
            new['conv'].append(jnp.concatenate([cbuf[:, 1:], _raw_conv_rows(u3)], axis=1))
            new['wkv'].append(s_wkv)
            new['shift'].append(_raw_rw_rows(u))
        else:
            ap = att[i]
            qkv = _linear(x, ap['w_qkv'], tm=tm, tn=512, pro='normmod', norm=(gmix, mod, 1, 0))
            qr, kr = _rope_step(qkv, tabs)
            heads = lambda t: t.reshape(B, len(ATT_GROUPS), ATT_HEADS, ATT_HEAD_DIM)
            vn = qkv.reshape(B, len(ATT_GROUPS), 3, ATT_DIM)[:, :, 2]
            o = _attn_step(heads(qr), heads(kr), heads(vn), seen, i)
            x = _linear(o.reshape(B, ATT_DIM), ap['w_out'], tm=tm, tn=512, epi='resgate', res=x, gate=(mod, 2))
            rows = jnp.stack([heads(kr), heads(vn)], axis=2)
            for gi in range(len(ATT_GROUPS)):
                new['win'][gi].append(rows[:, gi][:, None])
        gmlp = P['norm_mlp'][l][None, :]
        hid = _linear(x, P['w1'][l], tm=tm, tn=512, pro='normmod', epi='relu2', norm=(gmlp, mod, 4, 3),
                      out_dtype=BF16)
        x = _linear(hid, P['w2'][l], tm=tm, tn=512, epi='resgate', res=x, gate=(mod, 5))
    y = _rmsnorm(x, norm_final[None, :], tm).reshape(B, 1, D_MODEL)
    return y, new


def kernel(x_prompt, x_sample, state_ssd, state_ssd_conv, state_wkv, state_wkv_shift, cache_win0, cache_win1, cache_win2, c_prompt, c_sample, norm_mix, norm_mlp, norm_final, ada_w, ada_b, mlp_w1, mlp_w2, hyb_w_in, hyb_w_out, ssd_conv_w, ssd_conv_b, ssd_dt_bias, ssd_a_log, ssd_d, ssd_norm_w, rwkv_mu, rwkv_w0, rwkv_w2, rwkv_a0, rwkv_a2, rwkv_g2, rwkv_k_k, rwkv_k_a, rwkv_r_k, rwkv_ln_w, rwkv_ln_b, att_w_qkv, att_w_out):
    Bp, L, _ = x_prompt.shape
    Bs = x_sample.shape[0]
    assert x_sample.shape[1] == 1

    nrow = Bp + Bs
    npad = -nrow % 16
    c_all = jnp.concatenate([c_prompt, c_sample, jnp.zeros((npad, D_MODEL), F32)], axis=0)
    mods_p, mods_s = [], []
    for l in range(DEPTH):
        mod = _linear(c_all, ada_w[l].astype(BF16), tm=nrow + npad, tn=512, pro='silu', epi='bias',
                      bias=ada_b[l][None, :])
        mods_p.append(mod[:Bp].reshape(Bp, 1, N_MOD * D_MODEL))
        mods_s.append(mod[Bp:nrow].reshape(1, Bs, N_MOD * D_MODEL))

    P = dict(norm_mix=norm_mix, norm_mlp=norm_mlp,
             w1=[mlp_w1[l].astype(BF16) for l in range(DEPTH)],
             w2=[mlp_w2[l].astype(BF16) for l in range(DEPTH)])
    hyb = [_hyb_params(i, hyb_w_in, hyb_w_out, ssd_conv_w, ssd_conv_b, ssd_dt_bias, ssd_a_log, ssd_d,
                       ssd_norm_w, rwkv_mu, rwkv_w0, rwkv_w2, rwkv_a0, rwkv_a2, rwkv_g2, rwkv_k_k,
                       rwkv_k_a, rwkv_r_k, rwkv_ln_w, rwkv_ln_b) for i in range(hyb_w_in.shape[0])]
    att = [dict(w_qkv=att_w_qkv[i].astype(BF16), w_out=att_w_out[i].astype(BF16))
           for i in range(att_w_qkv.shape[0])]

    y_p, new_p = _run_prompt(x_prompt.reshape(Bp * L, D_MODEL), mods_p, P, hyb, att, norm_final, Bp, L)
    y_s, new_s = _run_sample(x_sample.reshape(Bs, D_MODEL), mods_s, P, hyb, att, norm_final,
                             (state_ssd, state_ssd_conv, state_wkv, state_wkv_shift,
                              (cache_win0, cache_win1, cache_win2)), Bs)
    st = jnp.stack
    win_s = [jnp.concatenate([c[:, :, 1:], st(new_s['win'][g])], axis=2)
             for g, c in enumerate((cache_win0, cache_win1, cache_win2))]
    return (y_p, y_s, st(new_p['ssd']), st(new_s['ssd']), st(new_p['conv']), st(new_s['conv']),
            st(new_p['wkv']), st(new_s['wkv']), st(new_p['shift']), st(new_s['shift']),
            st(new_p['win'][0]), win_s[0], st(new_p['win'][1]), win_s[1],
            st(new_p['win'][2]), win_s[2])
```

```python
import functools
import math

import numpy as np
import jax
import jax.numpy as jnp
from jax import lax
from jax.experimental import pallas as pl
from jax.experimental.pallas import tpu as pltpu

F32 = jnp.float32
BF16 = jnp.bfloat16
HIGHEST = lax.Precision.HIGHEST

D_MODEL = 1024
DEPTH = 4
PAST_LEN = 8192
NORM_EPS = 1e-6
N_MOD = 6
SSD_HEADS = 16
SSD_HEAD_DIM = 64
SSD_GROUPS = 2
SSD_STATE = 128
SSD_CONV = 4
SSD_CHUNK = 128
RWKV_HEADS = 16
RWKV_HEAD_DIM = 64
RWKV_LN_EPS = 64e-5
ATT_GROUPS = ((128, 1), (512, 4), (2048, 16))
ATT_HEADS = 8
ATT_HEAD_DIM = 64
ATT_DIM = ATT_HEADS * ATT_HEAD_DIM
ATT_Q_BLOCK = 128
ROPE_THETA = 500000.0
ROPE_DIM = ATT_HEAD_DIM // 4
MLP_HIDDEN = 4 * D_MODEL

U_COLS = 6144
U_Z, U_XS, U_R, U_K, U_V, U_BC, U_LW, U_DT = 0, 1024, 2048, 3072, 4096, 5120, 5632, 5888

LANES = 128
VMEM_LIMIT = 48 * 1024 * 1024
PROMPT_ROWS = 1024


def _cparams(*sem):
    return pltpu.CompilerParams(dimension_semantics=sem, vmem_limit_bytes=VMEM_LIMIT)


def _dot(a, b):
    return jnp.dot(a, b, preferred_element_type=F32)


def _dot_exact(a, b):
    return jnp.dot(a, b, preferred_element_type=F32, precision=HIGHEST)


def _dot_exact_nt(a, b):
    return lax.dot_general(a, b, (((1,), (1,)), ((), ())), preferred_element_type=F32, precision=HIGHEST)


def _dot_nt(a, b):
    return lax.dot_general(a, b, (((1,), (1,)), ((), ())), preferred_element_type=F32)


def _silu(x):
    return x * jax.nn.sigmoid(x)


def _softplus(x):
    return jnp.maximum(x, 0.0) + jnp.log1p(jnp.exp(-jnp.abs(x)))


def _block_ones():
    i = np.arange(LANES)
    return jnp.asarray((i[:, None] // 64 == i[None, :] // 64).astype(np.float32))


def _pair_eye():
    i = np.arange(64)
    j = np.arange(LANES)
    return jnp.asarray((i[:, None] == (j[None, :] % 64)).astype(np.float32))


def _head_expand(nheads, width):
    e = np.zeros((LANES, nheads * width), np.float32)
    for h in range(nheads):
        e[h, h * width:(h + 1) * width] = 1.0
    return jnp.asarray(e)


def _tril_ones(n):
    return jnp.asarray(np.tril(np.ones((n, n), np.float32)))


def _linear_kernel(*refs, pro, epi, two):
    refs = list(refs)
    x_ref = refs.pop(0)
    if pro == 'normmod':
        g_ref, sc_ref, sh_ref = refs.pop(0), refs.pop(0), refs.pop(0)
    w_ref = refs.pop(0)
    if two:
        x2_ref, w2_ref = refs.pop(0), refs.pop(0)
    if epi == 'bias':
        b_ref = refs.pop(0)
    if epi == 'resgate':
        res_ref, gate_ref = refs.pop(0), refs.pop(0)
    o_ref = refs.pop(0)

    if pro == 'cast':
        h = x_ref[...].astype(BF16)
    else:
        h_ref = refs.pop(0)

        @pl.when(pl.program_id(1) == 0)
        def _():
            x = x_ref[...].astype(F32)
            if pro == 'silu':
                hh = _silu(x)
            else:
                ms = jnp.mean(x * x, axis=-1, keepdims=True)
                y = (x * lax.rsqrt(ms + NORM_EPS)) * g_ref[...]
                hh = y * (1.0 + sc_ref[...]) + sh_ref[...]
            h_ref[...] = hh.astype(BF16)

        h = h_ref[...]
    acc = _dot(h, w_ref[...])
    if two:
        acc = acc + _dot(x2_ref[...].astype(BF16), w2_ref[...])
    if epi == 'bias':
        acc = acc + b_ref[...]
    elif epi == 'relu2':
        acc = jnp.square(jnp.maximum(acc, 0.0))
    elif epi == 'resgate':
        acc = res_ref[...] + gate_ref[...] * acc
    o_ref[...] = acc.astype(o_ref.dtype)


def _linear(x, w, *, tm, tn, pro='cast', epi='none', norm=None, bias=None, res=None, gate=None,
            bpb=1, out_dtype=F32, second=None):
    M, K = x.shape
    N = w.shape[1]
    assert M % tm == 0 and N % tn == 0
    in_specs = [pl.BlockSpec((tm, K), lambda i, j: (i, 0))]
    args = [x]
    scratch = []
    if pro == 'normmod':
        g, mod, ksc, ksh = norm
        r = mod.shape[1]
        in_specs += [pl.BlockSpec((1, K), lambda i, j: (0, 0)),
                     pl.BlockSpec((None, r, K), lambda i, j: (i // bpb, 0, ksc)),
                     pl.BlockSpec((None, r, K), lambda i, j: (i // bpb, 0, ksh))]
        args += [g, mod, mod]
    if pro != 'cast':
        scratch = [pltpu.VMEM((tm, K), BF16)]
    in_specs.append(pl.BlockSpec((K, tn), lambda i, j: (0, j)))
    args.append(w)
    if second is not None:
        x2, w2 = second
        K2 = x2.shape[1]
        in_specs += [pl.BlockSpec((tm, K2), lambda i, j: (i, 0)), pl.BlockSpec((K2, tn), lambda i, j: (0, j))]
        args += [x2, w2]
    if epi == 'bias':
        in_specs.append(pl.BlockSpec((1, tn), lambda i, j: (0, j)))
        args.append(bias)
    if epi == 'resgate':
        mod, kg = gate
        r = mod.shape[1]
        nj = N // tn
        in_specs += [pl.BlockSpec((tm, tn), lambda i, j: (i, j)),
                     pl.BlockSpec((None, r, tn), lambda i, j: (i // bpb, 0, kg * nj + j))]
        args += [res, mod]
    return pl.pallas_call(
        functools.partial(_linear_kernel, pro=pro, epi=epi, two=second is not None),
        name="linear_%s_%s" % (pro, epi),
        grid=(M // tm, N // tn),
        in_specs=in_specs,
        out_specs=pl.BlockSpec((tm, tn), lambda i, j: (i, j)),
        out_shape=jax.ShapeDtypeStruct((M, N), out_dtype),
        scratch_shapes=scratch,
        compiler_params=_cparams("parallel", "arbitrary"),
    )(*args)


def _rmsnorm_kernel(x_ref, g_ref, o_ref):
    x = x_ref[...]
    ms = jnp.mean(x * x, axis=-1, keepdims=True)
    o_ref[...] = (x * lax.rsqrt(ms + NORM_EPS)) * g_ref[...]


def _rmsnorm(x, g, tm):
    M, K = x.shape
    return pl.pallas_call(
        _rmsnorm_kernel,
        name="final_rmsnorm",
        grid=(M // tm,),
        in_specs=[pl.BlockSpec((tm, K), lambda i: (i, 0)), pl.BlockSpec((1, K), lambda i: (0, 0))],
        out_specs=pl.BlockSpec((tm, K), lambda i: (i, 0)),
        out_shape=jax.ShapeDtypeStruct((M, K), F32),
        compiler_params=_cparams("parallel"),
    )(x, g)


def _ssd_tail(y, xs, z, d_exp, norm_w):
    y = (y + d_exp * xs) * _silu(z)
    half = y.shape[1] // SSD_GROUPS
    outs = []
    for g in range(SSD_GROUPS):
        yg = y[:, g * half:(g + 1) * half]
        ms = jnp.mean(yg * yg, axis=-1, keepdims=True)
        outs.append(yg * lax.rsqrt(ms + NORM_EPS))
    return jnp.concatenate(outs, axis=1) * norm_w


def _ssd_prompt_kernel(z_ref, xs_ref, bc_ref, dt_ref, cw_ref, cb_ref, dtb_ref, alog_ref, dexp_ref,
                       nw_ref, tril_ref, e16_ref, y_ref, st_ref, extx, extbc, state, ybuf):
    c = pl.program_id(1)
    Q = SSD_CHUNK
    NX = SSD_HEADS * SSD_HEAD_DIM

    @pl.when(c == 0)
    def _():
        extx[0:8, :] = jnp.zeros((8, NX), F32)
        extbc[0:8, :] = jnp.zeros((8, 512), F32)
        state[...] = jnp.zeros_like(state)

    extx[8:8 + Q, :] = xs_ref[...]
    extbc[8:8 + Q, :] = bc_ref[...]
    cw = cw_ref[...]
    cb = cb_ref[...]
    xc = cb[:, 0:NX]
    bcc = cb[:, NX:NX + 512]
    for j in range(SSD_CONV):
        xc = xc + extx[pl.ds(5 + j, Q), :] * cw[j:j + 1, 0:NX]
        bcc = bcc + extbc[pl.ds(5 + j, Q), :] * cw[j:j + 1, NX:NX + 512]
    extx[0:8, :] = extx[Q:Q + 8, :]
    extbc[0:8, :] = extbc[Q:Q + 8, :]
    xs = _silu(xc)
    bcs = _silu(bcc)

    dt = _softplus(dt_ref[...] + dtb_ref[...])
    a_neg = -jnp.exp(alog_ref[...])
    acs = _dot_exact(tril_ref[...], dt * a_neg)
    acs_t = acs.T
    e16 = e16_ref[...]
    eacs = jnp.exp(acs)
    dt_exp = _dot_exact(dt, e16)
    eacs_exp = _dot_exact(eacs, e16)
    wend_exp = _dot_exact(jnp.exp(acs[Q - 1:Q, :] - acs) * dt, e16)
    xdt = (xs * dt_exp).astype(BF16)
    xw = (xs * wend_exp).astype(BF16)
    row = lax.broadcasted_iota(jnp.int32, (Q, Q), 0)
    col = lax.broadcasted_iota(jnp.int32, (Q, Q), 1)
    causal = row >= col
    HG = SSD_HEADS // SSD_GROUPS
    GW = HG * SSD_HEAD_DIM
    for g in range(SSD_GROUPS):
        b_g = bcs[:, g * SSD_STATE:(g + 1) * SSD_STATE]
        c_g = bcs[:, 256 + g * SSD_STATE:256 + (g + 1) * SSD_STATE].astype(BF16)
        cb_g = _dot_nt(c_g, b_g.astype(BF16))
        bt_g = b_g.T.astype(BF16)
        for hg in range(HG):
            h = g * HG + hg
            seg = acs[:, h:h + 1] - acs_t[h:h + 1, :]
            decay = jnp.where(causal, jnp.exp(seg), 0.0)
            scores = (cb_g * decay).astype(BF16)
            ybuf[:, h * 64:(h + 1) * 64] = _dot(scores, xdt[:, h * 64:(h + 1) * 64])
        st_g = state[g]
        y_off = _dot(c_g, st_g.astype(BF16)) * eacs_exp[:, g * GW:(g + 1) * GW]
        ybuf[:, g * GW:(g + 1) * GW] = ybuf[:, g * GW:(g + 1) * GW] + y_off
        state[g] = st_g * eacs_exp[Q - 1:Q, g * GW:(g + 1) * GW] + _dot(bt_g, xw[:, g * GW:(g + 1) * GW])

    y_ref[...] = _ssd_tail(ybuf[...], xs, z_ref[...], dexp_ref[...], nw_ref[...]).astype(y_ref.dtype)

    @pl.when(c == pl.num_programs(1) - 1)
    def _():
        st_ref[...] = state[...]


def _ssd_prompt(u, B, L, cw, cb, dtb, alog, dexp, nw):
    Q = SSD_CHUNK
    nc = L // Q
    row = lambda b, c: b * nc + c
    const = lambda shape: pl.BlockSpec(shape, lambda b, c: (0,) * len(shape))
    y, st = pl.pallas_call(
        _ssd_prompt_kernel,
        name="ssd_prompt",
        grid=(B, nc),
        in_specs=[pl.BlockSpec((Q, 1024), lambda b, c: (row(b, c), U_Z // 1024)),
                  pl.BlockSpec((Q, 1024), lambda b, c: (row(b, c), U_XS // 1024)),
                  pl.BlockSpec((Q, 512), lambda b, c: (row(b, c), U_BC // 512)),
                  pl.BlockSpec((Q, 128), lambda b, c: (row(b, c), U_DT // 128)),
                  const((SSD_CONV, 1536)), const((1, 1536)), const((1, 128)), const((1, 128)),
                  const((1, 1024)), const((1, 1024)), const((Q, Q)), const((128, 1024))],
        out_specs=[pl.BlockSpec((Q, 1024), lambda b, c: (row(b, c), 0)),
                   pl.BlockSpec((None, SSD_GROUPS, SSD_STATE, 512), lambda b, c: (b, 0, 0, 0))],
        out_shape=[jax.ShapeDtypeStruct((B * L, 1024), BF16),
                   jax.ShapeDtypeStruct((B, SSD_GROUPS, SSD_STATE, 512), F32)],
        scratch_shapes=[pltpu.VMEM((Q + 8, 1024), F32), pltpu.VMEM((Q + 8, 512), F32),
                        pltpu.VMEM((SSD_GROUPS, SSD_STATE, 512), F32), pltpu.VMEM((Q, 1024), F32)],
        compiler_params=_cparams("parallel", "arbitrary"),
    )(u, u, u, u, cw, cb, dtb, alog, dexp, nw, _tril_ones(Q), _head_expand(SSD_HEADS, 64))
    st = st.reshape(B, SSD_GROUPS, SSD_STATE, SSD_HEADS // SSD_GROUPS, SSD_HEAD_DIM)
    st = jnp.transpose(st, (0, 1, 3, 4, 2)).reshape(B, SSD_HEADS, SSD_HEAD_DIM, SSD_STATE)
    return y, st


def _ssd_step_kernel(z_ref, xs_ref, bc_ref, dt_ref, cx_ref, cbc_ref, s_ref, cw_ref, cb_ref, dtb_ref,
                     alog_ref, dexp_ref, nw_ref, e2_ref, y_ref, so_ref, ybuf):
    NX = SSD_HEADS * SSD_HEAD_DIM
    cw = cw_ref[...]
    cb = cb_ref[...]
    cx = cx_ref[...]
    cbc = cbc_ref[...]
    xc = cb[:, 0:NX] + xs_ref[...] * cw[3:4, 0:NX]
    bcc = cb[:, NX:NX + 512] + bc_ref[...] * cw[3:4, NX:NX + 512]
    for j in range(SSD_CONV - 1):
        xc = xc + cx[j:j + 1, :] * cw[j:j + 1, 0:NX]
        bcc = bcc + cbc[j:j + 1, :] * cw[j:j + 1, NX:NX + 512]
    xs = _silu(xc)
    bcs = _silu(bcc)
    dt = _softplus(dt_ref[...] + dtb_ref[...])
    da = jnp.exp(dt * (-jnp.exp(alog_ref[...])))
    e2 = e2_ref[...]
    lane = lax.broadcasted_iota(jnp.int32, (64, LANES), 1)
    first = lane < 64
    HG = SSD_HEADS // SSD_GROUPS
    for q in range(SSD_HEADS // 2):
        xrow = xs[:, q * LANES:(q + 1) * LANES]
        diag = e2 * xrow
        ycols = []
        for s in range(2):
            h = 2 * q + s
            g = h // HG
            xcol = jnp.sum(jnp.where(first == (s == 0), diag, 0.0), axis=1, keepdims=True)
            b_row = bcs[:, g * SSD_STATE:(g + 1) * SSD_STATE]
            c_row = bcs[:, 256 + g * SSD_STATE:256 + (g + 1) * SSD_STATE]
            s_new = s_ref[h] * da[:, h:h + 1] + (xcol * dt[:, h:h + 1]) * b_row
            so_ref[h] = s_new
            ycols.append(jnp.sum(s_new * c_row, axis=1, keepdims=True))
        ypair = jnp.where(first, ycols[0], ycols[1])
        ybuf[:, q * LANES:(q + 1) * LANES] = jnp.sum(e2 * ypair, axis=0, keepdims=True)
    y_ref[...] = _ssd_tail(ybuf[...], xs, z_ref[...], dexp_ref[...], nw_ref[...]).astype(y_ref.dtype)


def _ssd_step(u, conv_x, conv_bc, s0, cw, cb, dtb, alog, dexp, nw):
    B = u.shape[0]
    const = lambda shape: pl.BlockSpec(shape, lambda b: (0,) * len(shape))
    return pl.pallas_call(
        _ssd_step_kernel,
        name="ssd_step",
        grid=(B,),
        in_specs=[pl.BlockSpec((None, 1, 1024), lambda b: (b, 0, U_Z // 1024)),
                  pl.BlockSpec((None, 1, 1024), lambda b: (b, 0, U_XS // 1024)),
                  pl.BlockSpec((None, 1, 512), lambda b: (b, 0, U_BC // 512)),
                  pl.BlockSpec((None, 1, 128), lambda b: (b, 0, U_DT // 128)),
                  pl.BlockSpec((None, 3, 1024), lambda b: (b, 0, 0)),
                  pl.BlockSpec((None, 3, 512), lambda b: (b, 0, 0)),
                  pl.BlockSpec((None, SSD_HEADS, 64, 128), lambda b: (b, 0, 0, 0)),
                  const((SSD_CONV, 1536)), const((1, 1536)), const((1, 128)), const((1, 128)),
                  const((1, 1024)), const((1, 1024)), const((64, 128))],
        out_specs=[pl.BlockSpec((None, 1, 1024), lambda b: (b, 0, 0)),
                   pl.BlockSpec((None, SSD_HEADS, 64, 128), lambda b: (b, 0, 0, 0))],
        out_shape=[jax.ShapeDtypeStruct((B, 1, 1024), BF16),
                   jax.ShapeDtypeStruct(s0.shape, F32)],
        scratch_shapes=[pltpu.VMEM((1, 1024), F32)],
        compiler_params=_cparams("parallel"),
    )(u, u, u, u, conv_x, conv_bc, s0, cw, cb, dtb, alog, dexp, nw, _pair_eye())


def _rwkv_prep_kernel(*refs, shifted, bpb):
    refs = list(refs)
    cur = [refs.pop(0) for _ in range(4)]
    prev = [refs.pop(0) for _ in range(4)]
    if shifted:
        first = [refs.pop(0) for _ in range(4)]
    mu = [refs.pop(0) for _ in range(4)]
    (w0_ref, w2_ref, a0_ref, a2_ref, g2_ref, kk_ref, ka_ref, bo_ref) = [refs.pop(0) for _ in range(8)]
    (r_o, w_o, k_o, v_o, kn_o, kka_o, g_o) = refs
    i = pl.program_id(0)

    def mixed(n):
        x = cur[n][...]
        if shifted:
            rolled = pltpu.roll(x, 1, 0)
            before = jnp.where(i % bpb == 0, first[n][...], prev[n][7:8, :])
            rid = lax.broadcasted_iota(jnp.int32, x.shape, 0)
            p = jnp.where(rid == 0, before, rolled)
        else:
            p = prev[n][...]
        return x + (p - x) * mu[n][...]

    r, k, v, lw = mixed(0), mixed(1), mixed(2), mixed(3)
    blk = lw[:, 0:LANES]
    lane = lax.broadcasted_iota(jnp.int32, blk.shape, 1)
    tw = jnp.where(lane < 64, jnp.tanh(blk), blk).astype(BF16)
    wpre = w0_ref[...] + _dot(tw, w2_ref[...])
    apre = a0_ref[...] + _dot(tw, a2_ref[...])
    wlog = -_softplus(-wpre) - 0.5
    a = jax.nn.sigmoid(apre)
    g = _dot(jax.nn.sigmoid(lw[:, LANES:2 * LANES]).astype(BF16), g2_ref[...])
    kk = k * kk_ref[...]
    kk2 = kk * kk
    bo = bo_ref[...]
    for q in range(RWKV_HEADS // 2):
        sl = slice(q * LANES, (q + 1) * LANES)
        n2 = _dot_exact(kk2[:, sl], bo)
        kn = kk[:, sl] / jnp.maximum(jnp.sqrt(n2), 1e-12)
        kn_o[:, sl] = -kn
        kka_o[:, sl] = kn * a[:, sl]
    r_o[...] = r
    w_o[...] = -jnp.exp(wlog)
    k_o[...] = k * (1.0 + (a - 1.0) * ka_ref[...])
    v_o[...] = v
    g_o[...] = g


def _rwkv_prep(u, prev, first, mus, ws, *, tm, bpb, shifted):
    M = u.shape[0]
    cols = [(1024, U_R // 1024), (1024, U_K // 1024), (1024, U_V // 1024), (256, U_LW // 256)]
    in_specs = [pl.BlockSpec((tm, c), functools.partial(lambda i, kb: (i, kb), kb=kb)) for c, kb in cols]
    args = [u] * 4
    if shifted:
        in_specs += [pl.BlockSpec((8, c), functools.partial(
            lambda i, kb: (jnp.maximum(i * (tm // 8) - 1, 0), kb), kb=kb)) for c, kb in cols]
        args += [u] * 4
        in_specs += [pl.BlockSpec((None, 1, c), lambda i: (i // bpb, 0, 0)) for c, _ in cols]
        args += list(first)
    else:
        in_specs += [pl.BlockSpec((tm, c), lambda i: (i, 0)) for c, _ in cols]
        args += list(prev)
    in_specs += [pl.BlockSpec((1, c), lambda i: (0, 0)) for c, _ in cols]
    args += list(mus)
    wshapes = [(1, 1024), (128, 1024), (1, 1024), (128, 1024), (128, 1024), (1, 1024), (1, 1024), (128, 128)]
    in_specs += [pl.BlockSpec(s, lambda i: (0, 0)) for s in wshapes]
    args += list(ws)
    return pl.pallas_call(
        functools.partial(_rwkv_prep_kernel, shifted=shifted, bpb=bpb),
        name="rwkv_prep",
        grid=(M // tm,),
        in_specs=in_specs,
        out_specs=[pl.BlockSpec((tm, 1024), lambda i: (i, 0))] * 7,
        out_shape=[jax.ShapeDtypeStruct((M, 1024), F32)] * 7,
        compiler_params=_cparams("parallel"),
    )(*args)


def _wkv_step_kernel(r_ref, lw_ref, k_ref, v_ref, kn_ref, ka_ref, s0_ref, e2_ref, bo_ref, o_ref, sT_ref, *, nb):
    e2 = e2_ref[...]
    bo = bo_ref[...]
    first = lax.broadcasted_iota(jnp.int32, (64, LANES), 1) < 64

    def pair_sum(x):
        sa = jnp.sum(jnp.where(first, x, 0.0), axis=1, keepdims=True)
        sb = jnp.sum(jnp.where(first, 0.0, x), axis=1, keepdims=True)
        return jnp.where(first, sa, sb)

    for b in range(nb):
        for p in range(RWKV_HEADS // 2):
            sl = slice(p * LANES, (p + 1) * LANES)
            S = s0_ref[b, p]
            sa = pair_sum(S * kn_ref[b, :, sl])
            vcol = _dot_exact(e2 * v_ref[b, :, sl], bo)
            S = S * jnp.exp(lw_ref[b, :, sl]) + sa * ka_ref[b, :, sl] + vcol * k_ref[b, :, sl]
            sT_ref[b, p] = S
            o = pair_sum(S * r_ref[b, :, sl])
            o_ref[b, :, sl] = jnp.sum(e2 * o, axis=0, keepdims=True)


def _wkv_step(r, lw, k, v, kn, ka, s0, *, nb):
    B = r.shape[0]
    s0p = s0.reshape(B, 8, 2, 64, 64).transpose(0, 1, 3, 2, 4).reshape(B, 8, 64, 128)
    seq = pl.BlockSpec((nb, 1, 1024), lambda b: (b, 0, 0))
    stt = pl.BlockSpec((nb, 8, 64, 128), lambda b: (b, 0, 0, 0))
    o, sT = pl.pallas_call(
        functools.partial(_wkv_step_kernel, nb=nb),
        grid=(B // nb,),
        in_specs=[seq] * 6 + [stt, pl.BlockSpec((64, 128), lambda b: (0, 0)),
                              pl.BlockSpec((128, 128), lambda b: (0, 0))],
        out_specs=[seq, stt],
        out_shape=[jax.ShapeDtypeStruct((B, 1, 1024), F32), jax.ShapeDtypeStruct((B, 8, 64, 128), F32)],
        compiler_params=_cparams("parallel"),
        name="wkv_step",
    )(r, lw, k, v, kn, ka, s0p, _pair_eye(), _block_ones())
    sT = sT.reshape(B, 8, 64, 2, 64).transpose(0, 1, 3, 2, 4).reshape(B, 16, 64, 64)
    return o, sT


WKV_CHUNK = 64


def _wkv_chunk_kernel(r_ref, lw_ref, k_ref, v_ref, kn_ref, ka_ref, tril_ref, o_ref, sT_ref, S_ref, *, nb):
    C = WKV_CHUNK
    c = pl.program_id(1)

    @pl.when(c == 0)
    def _():
        S_ref[...] = jnp.zeros_like(S_ref)

    tril = tril_ref[...]
    lane = lax.broadcasted_iota(jnp.int32, (C, LANES), 1)
    rowi = lax.broadcasted_iota(jnp.int32, (C, LANES), 0)
    first = lane < 64
    strict = rowi > (lane % 64)
    incl = rowi >= (lane % 64)
    r128 = lax.broadcasted_iota(jnp.int32, (LANES, LANES), 0)
    c128 = lax.broadcasted_iota(jnp.int32, (LANES, LANES), 1)
    diag_blocks = (r128 < 64) == (c128 < 64)
    eye = r128 == c128

    def bd(x):
        return jnp.concatenate([jnp.where(first, x, 0.0), jnp.where(first, 0.0, x)], axis=0)

    bf = lambda x: x.astype(BF16)
    pairs = [(b, p) for b in range(nb) for p in range(RWKV_HEADS // 2)]
    sls = [slice(p * LANES, (p + 1) * LANES) for _, p in pairs]
    load = lambda ref: [ref[b, :, sl] for (b, _), sl in zip(pairs, sls)]
    each = lambda f, *ls: [f(*a) for a in zip(*ls)]
    r_, lw, kt, vv, al, be = (load(ref) for ref in (r_ref, lw_ref, k_ref, v_ref, kn_ref, ka_ref))
    cs = each(lambda x: _dot_exact(tril, x), lw)
    last = each(lambda x: x[C - 1:C, :], cs)
    e_inv = each(lambda x: jnp.exp(-x), cs)
    aq = each(lambda a, x, l: a * jnp.exp(x - l), al, cs, lw)
    rq = each(lambda a, x: a * jnp.exp(x), r_, cs)
    bk = each(jnp.multiply, be, e_inv)
    kk = each(jnp.multiply, kt, e_inv)
    g = each(lambda a, q, b_, k_: _dot_exact_nt(
        jnp.concatenate([a, q], axis=0),
        jnp.concatenate([jnp.where(first, b_, 0.0), jnp.where(first, 0.0, b_),
                         jnp.where(first, k_, 0.0), jnp.where(first, 0.0, k_)], axis=0)), aq, rq, bk, kk)
    m1 = each(lambda x: bf(jnp.where(strict, x[0:C, 0:LANES], 0.0)), g)
    m2 = each(lambda x: bf(jnp.where(strict, x[0:C, LANES:2 * LANES], 0.0)), g)
    n1 = each(lambda x: bf(jnp.where(incl, x[C:2 * C, 0:LANES], 0.0)), g)
    n2 = each(lambda x: bf(jnp.where(incl, x[C:2 * C, LANES:2 * LANES], 0.0)), g)
    s0 = [S_ref[b, p] for b, p in pairs]
    s0b = each(bf, s0)
    vbd = each(lambda x: bf(bd(x)), vv)
    x = each(lambda a, s, m, v_: _dot(bf(a), s) + _dot(m, v_), aq, s0b, m2, vbd)
    mp = m1
    steps = int(math.log2(C))
    for i in range(steps):
        x = each(lambda x_, m: x_ + _dot(m, bf(bd(x_))), x, mp)
        if i + 1 < steps:
            mp = each(lambda m: bf(_dot(m, bd(m))), mp)
    o = each(lambda q, s, a, x_, b_, v_: _dot(bf(q), s) + _dot(a, bf(bd(x_))) + _dot(b_, v_),
             rq, s0b, n1, x, n2, vbd)
    for (b, _), sl, o_ in zip(pairs, sls, o):
        o_ref[b, :, sl] = o_
    e_end = each(lambda l, x_: jnp.exp(l - x_), last, cs)
    kv_t = each(lambda b_, k_, e: jnp.concatenate([b_ * e, k_ * e], axis=0).T, be, kt, e_end)
    upd = each(lambda t, x_, v_: _dot(bf(t), bf(jnp.concatenate([x_, v_], axis=0))), kv_t, x, vv)
    gcol = each(lambda l: jnp.sum(jnp.where(eye, jnp.exp(l), 0.0), axis=1, keepdims=True), last)
    for (b, p), s, u, gc in zip(pairs, s0, upd, gcol):
        S_ref[b, p] = jnp.where(diag_blocks, gc * s + u, 0.0)

    @pl.when(c == pl.num_programs(1) - 1)
    def _():
        sT_ref[...] = S_ref[...]


def _wkv_chunked(r, lw, k, v, kn, ka, *, nb):
    B, L, _ = r.shape
    C = WKV_CHUNK
    seq = pl.BlockSpec((nb, C, 1024), lambda b, c: (b, c, 0))
    stt = pl.BlockSpec((nb, 8, LANES, LANES), lambda b, c: (b, 0, 0, 0))
    o, sT = pl.pallas_call(
        functools.partial(_wkv_chunk_kernel, nb=nb),
        grid=(B // nb, L // C),
        in_specs=[seq] * 6 + [pl.BlockSpec((C, C), lambda b, c: (0, 0))],
        out_specs=[seq, stt],
        out_shape=[jax.ShapeDtypeStruct((B, L, 1024), F32), jax.ShapeDtypeStruct((B, 8, LANES, LANES), F32)],
        scratch_shapes=[pltpu.VMEM((nb, 8, LANES, LANES), F32)],
        compiler_params=_cparams("parallel", "arbitrary"),
        name="wkv_chunked",
    )(r, lw, k, v, kn, ka, _tril_ones(C))
    blocks = jnp.stack([sT[:, :, 0:64, 0:64], sT[:, :, 64:128, 64:128]], axis=2)
    return o, jnp.swapaxes(blocks, -1, -2).reshape(B, 16, 64, 64)


def _rwkv_post_kernel(o_ref, r_ref, k_ref, v_ref, g_ref, lnw_ref, lnb_ref, rk_ref, bo_ref, y_ref):
    bo = bo_ref[...]
    inv = 1.0 / RWKV_HEAD_DIM
    for q in range(RWKV_HEADS // 2):
        sl = slice(q * LANES, (q + 1) * LANES)
        o = o_ref[:, sl]
        mean = _dot_exact(o, bo) * inv
        d = o - mean
        var = _dot_exact(d * d, bo) * inv
        on = d * lax.rsqrt(var + RWKV_LN_EPS) * lnw_ref[:, sl] + lnb_ref[:, sl]
        bonus = _dot_exact(r_ref[:, sl] * k_ref[:, sl] * rk_ref[:, sl], bo) * v_ref[:, sl]
        y_ref[:, sl] = ((on + bonus) * g_ref[:, sl]).astype(y_ref.dtype)


def _rwkv_post(o, r, k, v, g, lnw, lnb, rk, tm):
    M = o.shape[0]
    blk = pl.BlockSpec((tm, 1024), lambda i: (i, 0))
    vec = pl.BlockSpec((1, 1024), lambda i: (0, 0))
    return pl.pallas_call(
        _rwkv_post_kernel,
        name="rwkv_post",
        grid=(M // tm,),
        in_specs=[blk] * 5 + [vec] * 3 + [pl.BlockSpec((128, 128), lambda i: (0, 0))],
        out_specs=blk,
        out_shape=jax.ShapeDtypeStruct((M, 1024), BF16),
        compiler_params=_cparams("parallel"),
    )(o, r, k, v, g, lnw, lnb, rk, _block_ones())


def _rope_tables(pos):
    half = ROPE_DIM // 2
    inv = ROPE_THETA ** (-jnp.arange(half, dtype=F32) * 2.0 / ROPE_DIM)
    ang = pos.astype(F32)[:, None] * inv
    cos, sin = jnp.cos(ang), jnp.sin(ang)
    n = pos.shape[0]
    rest = ATT_HEAD_DIM - ROPE_DIM
    c = jnp.concatenate([cos, cos, jnp.ones((n, rest), F32)], axis=1)
    s_next = jnp.concatenate([-sin, jnp.zeros((n, half + rest), F32)], axis=1)
    s_prev = jnp.concatenate([jnp.zeros((n, half), F32), sin, jnp.zeros((n, rest), F32)], axis=1)
    return tuple(jnp.concatenate([t, t], axis=1) for t in (c, s_next, s_prev))


def _rope_apply(x, c, s_next, s_prev):
    n = x.shape[1]
    reps = n // LANES
    tile = lambda t: jnp.concatenate([t] * reps, axis=1)
    half = ROPE_DIM // 2
    return x * tile(c) + pltpu.roll(x, n - half, 1) * tile(s_next) + pltpu.roll(x, half, 1) * tile(s_prev)


ATT_BLOCK_ROWS = ATT_Q_BLOCK * max(d for _, d in ATT_GROUPS)
ATT_BATCH = 4


def _attn_prompt_kernel(q_ref, kc_ref, kp_ref, vc_ref, vp_ref, cc_ref, snc_ref, spc_ref, cp_ref, snp_ref,
                        spp_ref, o_ref, lse_ref, kr_ref, q_s, k_s, v_s, *, d):
    i = pl.program_id(0)
    QB = ATT_Q_BLOCK
    R = q_ref.shape[0]
    q_s[...] = _rope_apply(q_ref[...], cc_ref[...], snc_ref[...], spc_ref[...])
    kc = _rope_apply(kc_ref[...], cc_ref[...], snc_ref[...], spc_ref[...])
    kr_ref[...] = kc
    k_s[R:2 * R, :] = kc
    k_s[0:R, :] = _rope_apply(kp_ref[...], cp_ref[...], snp_ref[...], spp_ref[...])
    v_s[R:2 * R, :] = vc_ref[...]
    v_s[0:R, :] = vp_ref[...]
    row = lax.broadcasted_iota(jnp.int32, (QB, 2 * QB), 0)
    col = lax.broadcasted_iota(jnp.int32, (QB, 2 * QB), 1)
    band = (col >= row) & (col <= row + QB)
    band_first = band & ((i > 0) | (col >= QB))
    first = lax.broadcasted_iota(jnp.int32, (QB, LANES), 1) < ATT_HEAD_DIM
    first_kv = lax.broadcasted_iota(jnp.int32, (2 * QB, LANES), 1) < ATT_HEAD_DIM
    scale = ATT_HEAD_DIM ** -0.5
    span = QB * d
    blocks = [(rho, j) for rho in range(d) for j in range(R // span)]
    each = lambda f, *ls: [f(*a) for a in zip(*ls)]
    for b0 in range(0, len(blocks), ATT_BATCH):
        batch = blocks[b0:b0 + ATT_BATCH]
        ds = lambda start, n: pl.ds(start, n, stride=d) if d > 1 else pl.ds(start, n)
        qrows = [ds(rho + span * j, QB) for rho, j in batch]
        krows = [ds(R + rho + span * (j - 1), 2 * QB) for rho, j in batch]
        valid = [band_first if j == 0 else band for _, j in batch]
        qb = [q_s[r, :].astype(BF16) for r in qrows]
        k2 = [k_s[r, :].astype(BF16) for r in krows]
        v2 = [v_s[r, :].astype(BF16) for r in krows]
        halves = [slice(0, ATT_HEAD_DIM), slice(ATT_HEAD_DIM, LANES)]
        s = [[jnp.where(vm, _dot_nt(q[:, sl], k[:, sl]) * scale, -jnp.inf) for sl in halves]
             for q, k, vm in zip(qb, k2, valid)]
        m = [[jnp.max(jnp.maximum(x[:, 0:QB], x[:, QB:2 * QB]), axis=1, keepdims=True) for x in pair] for pair in s]
        p = [[jnp.exp(x - mx).astype(BF16) for x, mx in zip(ps, ms)] for ps, ms in zip(s, m)]
        va = each(lambda v: jnp.where(first_kv, v, 1.0), v2)
        vb = each(lambda v: jnp.where(first_kv, 1.0, v), v2)
        ea = each(lambda pp, v: _dot(pp[0], v), p, va)
        eb = each(lambda pp, v: _dot(pp[1], v), p, vb)
        num = each(lambda a, b_: jnp.where(first, a, b_), ea, eb)
        den = each(lambda a, b_: pltpu.roll(jnp.where(first, b_, a), ATT_HEAD_DIM, 1), ea, eb)
        for r, n_, d_, mm in zip(qrows, num, den, m):
            o_ref[r, :] = n_ / d_
            lse_ref[r, :] = jnp.where(first, mm[0], mm[1]) + jnp.log(d_)


def _attn_prompt(qkv, tabs, B, L, gi):
    window, d = ATT_GROUPS[gi]
    R = min(ATT_BLOCK_ROWS, L)
    assert window == ATT_Q_BLOCK * d and R % (ATT_Q_BLOCK * d) == 0 and L % R == 0
    nblk = L // R
    npair = ATT_DIM // LANES
    prev = lambda i: jnp.maximum(i - 1, 0)

    def col(which, f):
        return lambda i, b, hp: (b * nblk + f(i), gi * 3 * npair + which * npair + hp)

    same = lambda i: i
    blk = lambda f: pl.BlockSpec((R, LANES), f)
    tcur = pl.BlockSpec((R, LANES), lambda i, b, hp: (i, 0))
    tprev = pl.BlockSpec((R, LANES), lambda i, b, hp: (prev(i), 0))
    out = pl.BlockSpec((R, LANES), lambda i, b, hp: (b * nblk + i, hp))
    return pl.pallas_call(
        functools.partial(_attn_prompt_kernel, d=d),
        grid=(nblk, B, npair),
        in_specs=[blk(col(0, same)), blk(col(1, same)), blk(col(1, prev)), blk(col(2, same)), blk(col(2, prev)),
                  tcur, tcur, tcur, tprev, tprev, tprev],
        out_specs=[out, out, out],
        out_shape=[jax.ShapeDtypeStruct((B * L, ATT_DIM), F32)] * 3,
        scratch_shapes=[pltpu.VMEM((R, LANES), F32), pltpu.VMEM((2 * R, LANES), F32),
                        pltpu.VMEM((2 * R, LANES), F32)],
        compiler_params=_cparams("arbitrary", "arbitrary", "arbitrary"),
        name="attn_prompt_d%d" % d,
    )(qkv, qkv, qkv, qkv, qkv, *tabs, *tabs)


def _attn_out_kernel(o0, l0, o1, l1, o2, l2, w_ref, res_ref, gate_ref, out_ref, h_ref):
    @pl.when(pl.program_id(1) == 0)
    def _():
        m = jnp.maximum(jnp.maximum(l0[...], l1[...]), l2[...])
        a0, a1, a2 = jnp.exp(l0[...] - m), jnp.exp(l1[...] - m), jnp.exp(l2[...] - m)
        o = (a0 * o0[...] + a1 * o1[...] + a2 * o2[...]) / (a0 + a1 + a2)
        h_ref[...] = o.astype(BF16)

    out_ref[...] = res_ref[...] + gate_ref[...] * _dot(h_ref[...], w_ref[...])


def _attn_out(ols, w, res, mod, kg, *, tm, tn, bpb):
    M = res.shape[0]
    N = w.shape[1]
    r = mod.shape[1]
    nj = N // tn
    part = pl.BlockSpec((tm, 512), lambda i, j: (i, 0))
    return pl.pallas_call(
        _attn_out_kernel,
        name="attn_out",
        grid=(M // tm, nj),
        in_specs=[part] * 6 + [pl.BlockSpec((512, tn), lambda i, j: (0, j)),
                               pl.BlockSpec((tm, tn), lambda i, j: (i, j)),
                               pl.BlockSpec((None, r, tn), lambda i, j: (i // bpb, 0, kg * nj + j))],
        out_specs=pl.BlockSpec((tm, tn), lambda i, j: (i, j)),
        out_shape=jax.ShapeDtypeStruct((M, N), F32),
        scratch_shapes=[pltpu.VMEM((tm, 512), BF16)],
        compiler_params=_cparams("parallel", "arbitrary"),
    )(*ols, w, res, mod)


def _rope_step_kernel(qkv_ref, tc_ref, tn_ref, tp_ref, q_ref, k_ref):
    tabs = (tc_ref[...], tn_ref[...], tp_ref[...])
    for gi in range(len(ATT_GROUPS)):
        base = gi * 3 * ATT_DIM
        out = slice(gi * ATT_DIM, (gi + 1) * ATT_DIM)
        q_ref[:, out] = _rope_apply(qkv_ref[:, base:base + ATT_DIM], *tabs)
        k_ref[:, out] = _rope_apply(qkv_ref[:, base + ATT_DIM:base + 2 * ATT_DIM], *tabs)


def _rope_step(qkv, tabs):
    B = qkv.shape[0]
    n = len(ATT_GROUPS) * ATT_DIM
    full = lambda shape: pl.BlockSpec(shape, lambda: (0,) * len(shape))
    return pl.pallas_call(
        _rope_step_kernel,
        in_specs=[full(qkv.shape)] + [full((1, LANES))] * 3,
        out_specs=[full((B, n))] * 2,
        out_shape=[jax.ShapeDtypeStruct((B, n), F32)] * 2,
        name="rope_step",
    )(qkv, *tabs)


def _attn_step_kernel(q_ref, kn_ref, vn_ref, c0_ref, c1_ref, c2_ref, o_ref):
    caches = (c0_ref, c1_ref, c2_ref)
    scale = ATT_HEAD_DIM ** -0.5
    hl = lax.broadcasted_iota(jnp.int32, (ATT_HEAD_DIM, ATT_HEADS), 1)
    out = jnp.zeros((ATT_HEAD_DIM, ATT_HEADS), F32)
    for h in range(ATT_HEADS):
        parts = []
        for gi, (window, d) in enumerate(ATT_GROUPS):
            q = q_ref[gi][:, h:h + 1]
            kn = kn_ref[gi][:, h:h + 1]
            vn = vn_ref[gi][:, h:h + 1]
            kc = caches[gi][0, h]
            vc = caches[gi][1, h]
            lane = lax.broadcasted_iota(jnp.int32, (1, window), 1)
            s = jnp.sum(kc * q, axis=0, keepdims=True) * scale
            s = jnp.where(lane % d == 0, s, -jnp.inf)
            s_new = jnp.sum(q * kn, axis=0, keepdims=True) * scale
            m = jnp.maximum(jnp.max(s, axis=1, keepdims=True), s_new)
            p = jnp.exp(s - m)
            p_new = jnp.exp(s_new - m)
            l = jnp.sum(p, axis=1, keepdims=True) + p_new
            acc = jnp.sum(vc * p, axis=1, keepdims=True) + p_new * vn
            parts.append((acc / l, m + jnp.log(l)))
        mm = jnp.maximum(jnp.maximum(parts[0][1], parts[1][1]), parts[2][1])
        ws = [jnp.exp(lse - mm) for _, lse in parts]
        o = (ws[0] * parts[0][0] + ws[1] * parts[1][0] + ws[2] * parts[2][0]) / (ws[0] + ws[1] + ws[2])
        out = jnp.where(hl == h, o, out)
    o_ref[...] = out


def _window_minor(c):
    return jnp.transpose(c, (0, 1, 3, 4, 5, 2))


def _attn_step(q3, kn3, vn3, caches_t, layer):
    B = q3.shape[0]
    for (window, d), c in zip(ATT_GROUPS, caches_t):
        assert c.shape[-1] == window and window // d == 128
    cols = lambda t: jnp.swapaxes(t, -1, -2)
    specs = [pl.BlockSpec((None, None) + c.shape[2:], lambda b: (layer, b, 0, 0, 0, 0)) for c in caches_t]
    new = pl.BlockSpec((None, len(ATT_GROUPS), ATT_HEAD_DIM, ATT_HEADS), lambda b: (b, 0, 0, 0))
    o = pl.pallas_call(
        _attn_step_kernel,
        grid=(B,),
        in_specs=[new] * 3 + specs,
        out_specs=pl.BlockSpec((None, ATT_HEAD_DIM, ATT_HEADS), lambda b: (b, 0, 0)),
        out_shape=jax.ShapeDtypeStruct((B, ATT_HEAD_DIM, ATT_HEADS), F32),
        compiler_params=_cparams("parallel"),
        name="attn_step",
    )(cols(q3), cols(kn3), cols(vn3), *caches_t)
    return jnp.swapaxes(o, -1, -2)


def _roll_kernel(c0, c1, c2, n0, n1, n2, o0, o1, o2):
    for c_ref, n_ref, o_ref in ((c0, n0, o0), (c1, n1, o1), (c2, n2, o2)):
        w = c_ref.shape[-1]
        rows = c_ref.shape[0] * c_ref.shape[1] * c_ref.shape[2]
        x = c_ref[...].reshape(rows, w)
        lane = lax.broadcasted_iota(jnp.int32, (rows, w), 1)
        y = jnp.where(lane == w - 1, n_ref[...].reshape(rows, 1), pltpu.roll(x, w - 1, 1))
        o_ref[...] = y.reshape(o_ref.shape)


def _roll_windows(caches_t, rows_t):
    NC, B = caches_t[0].shape[:2]
    spec = lambda a: pl.BlockSpec((None, None) + a.shape[2:], lambda i, b: (i, b, 0, 0, 0, 0))
    return pl.pallas_call(
        _roll_kernel,
        grid=(NC, B),
        in_specs=[spec(c) for c in caches_t] + [spec(r) for r in rows_t],
        out_specs=[spec(c) for c in caches_t],
        out_shape=[jax.ShapeDtypeStruct(c.shape, c.dtype) for c in caches_t],
        compiler_params=_cparams("parallel", "parallel"),
        name="roll_windows",
    )(*caches_t, *rows_t)


def _hyb_params(i, hyb_w_in, hyb_w_out, ssd_conv_w, ssd_conv_b, ssd_dt_bias, ssd_a_log, ssd_d, ssd_norm_w,
                rwkv_mu, rwkv_w0, rwkv_w2, rwkv_a0, rwkv_a2, rwkv_g2, rwkv_k_k, rwkv_k_a, rwkv_r_k,
                rwkv_ln_w, rwkv_ln_b):
    w = hyb_w_in[i]
    rw0 = 2576
    w_perm = jnp.concatenate(
        [w[:, 0:1024], w[:, 1024:2048], w[:, rw0:rw0 + 3072], w[:, 2048:2560], w[:, rw0 + 3072:rw0 + 3328],
         w[:, 2560:2576], jnp.zeros((D_MODEL, U_COLS - U_DT - 16), F32)], axis=1).astype(BF16)
    pad128 = lambda v: jnp.concatenate([v, jnp.zeros((LANES - v.shape[0],), F32)])[None, :]
    z64 = jnp.zeros((64, 1024), F32)
    mu = rwkv_mu[i]
    p = dict(
        w_in=w_perm, w_out=hyb_w_out[i].astype(BF16),
        cw=ssd_conv_w[i], cb=ssd_conv_b[i][None, :], dtb=pad128(ssd_dt_bias[i]), alog=pad128(ssd_a_log[i]),
        dexp=jnp.repeat(ssd_d[i], SSD_HEAD_DIM)[None, :], nw=ssd_norm_w[i][None, :],
        mus=[mu[None, 0:1024], mu[None, 1024:2048], mu[None, 2048:3072], mu[None, 3072:3328]],
        prep_w=[rwkv_w0[i][None, :], jnp.concatenate([rwkv_w2[i], z64]).astype(BF16),
                rwkv_a0[i][None, :], jnp.concatenate([z64, rwkv_a2[i]]).astype(BF16),
                rwkv_g2[i].astype(BF16), rwkv_k_k[i][None, :], rwkv_k_a[i][None, :], _block_ones()],
        lnw=rwkv_ln_w[i][None, :], lnb=rwkv_ln_b[i][None, :], rk=rwkv_r_k[i].reshape(1, 1024),
    )
    return p


def _raw_conv_rows(u_rows):
    return jnp.concatenate([u_rows[..., U_XS:U_XS + 1024], u_rows[..., U_BC:U_BC + 512]], axis=-1)


def _raw_rw_rows(u_rows):
    return jnp.concatenate([u_rows[..., U_R:U_R + 3072], u_rows[..., U_LW:U_LW + 256]], axis=-1)


def _run_prompt(x, mods, P, hyb, att, norm_final, B, L):
    T = B * L
    big, half = PROMPT_ROWS, PROMPT_ROWS // 2
    lin = lambda *a, tm, **kw: _linear(*a, tm=tm, bpb=L // tm, **kw)
    new = dict(ssd=[], conv=[], wkv=[], shift=[], win=[[], [], []])
    tabs = _rope_tables(jnp.arange(L))
    for l in range(DEPTH):
        mod = mods[l]
        i = l // 2
        gmix = P['norm_mix'][l][None, :]
        if l % 2 == 0:
            hp = hyb[i]
            u = lin(x, hp['w_in'], tm=big, tn=1024, pro='normmod', norm=(gmix, mod, 1, 0))
            y_ssd, s_ssd = _ssd_prompt(u, B, L, hp['cw'], hp['cb'], hp['dtb'], hp['alog'], hp['dexp'], hp['nw'])
            zeros = [jnp.zeros((B, 1, c), F32) for c in (1024, 1024, 1024, 256)]
            r, w, k, v, kn, ka, g = _rwkv_prep(u, None, zeros, hp['mus'], hp['prep_w'], tm=256, bpb=L // 256,
                                               shifted=True)
            sh = lambda t: t.reshape(B, L, 1024)
            o, s_wkv = _wkv_chunked(sh(r), sh(w), sh(k), sh(v), sh(kn), sh(ka), nb=B)
            y_rwkv = _rwkv_post(o.reshape(T, 1024), r, k, v, g, hp['lnw'], hp['lnb'], hp['rk'], 256)
            x = lin(y_ssd, hp['w_out'][:1024], tm=big, tn=1024, epi='resgate', res=x, gate=(mod, 2),
                    second=(y_rwkv, hp['w_out'][1024:]))
            u3 = u.reshape(B, L, U_COLS)
            new['ssd'].append(s_ssd)
            new['conv'].append(_raw_conv_rows(u3[:, L - (SSD_CONV - 1):]))
            new['wkv'].append(s_wkv)
            new['shift'].append(_raw_rw_rows(u3[:, L - 1]))
        else:
            ap = att[i]
            qkv = lin(x, ap['w_qkv'], tm=big, tn=1536, pro='normmod', norm=(gmix, mod, 1, 0))
            ols = []
            q3 = qkv.reshape(B, L, 4608)
            for gi, (window, d) in enumerate(ATT_GROUPS):
                o, lse, kr = _attn_prompt(qkv, tabs, B, L, gi)
                ols += [o, lse]
                keep = min(window, L)
                kk = kr.reshape(B, L, 512)[:, L - keep:]
                vv = q3[:, L - keep:, gi * 1536 + 1024:gi * 1536 + 1536]
                new['win'][gi].append(jnp.stack([kk, vv], axis=2).reshape(B, keep, 2, ATT_HEADS, ATT_HEAD_DIM))
            x = _attn_out(ols, ap['w_out'], x, mod, 2, tm=half, tn=1024, bpb=L // half)
        gmlp = P['norm_mlp'][l][None, :]
        hid = lin(x, P['w1'][l], tm=big, tn=1024, pro='normmod', epi='relu2', norm=(gmlp, mod, 4, 3),
                  out_dtype=BF16)
        x = lin(hid, P['w2'][l], tm=half, tn=1024, epi='resgate', res=x, gate=(mod, 5))
    y = _rmsnorm(x, norm_final[None, :], half).reshape(B, L, D_MODEL)
    return y, new


def _run_sample(x, mods, P, hyb, att, norm_final, states, B):
    state_ssd, state_conv, state_wkv, state_shift, caches = states
    tm = B
    new = dict(ssd=[], conv=[], wkv=[], shift=[], win=[[], [], []])
    tabs = _rope_tables(jnp.full((1,), PAST_LEN, jnp.int32))
    caches_t = [_window_minor(c) for c in caches]
    for l in range(DEPTH):
        mod = mods[l]
        i = l // 2
        gmix = P['norm_mix'][l][None, :]
        if l % 2 == 0:
            hp = hyb[i]
            u = _linear(x, hp['w_in'], tm=tm, tn=512, pro='normmod', norm=(gmix, mod, 1, 0))
            u3 = u.reshape(B, 1, U_COLS)
            cbuf = state_conv[i]
            y_ssd, s_ssd = _ssd_step(u3, cbuf[:, :, 0:1024], cbuf[:, :, 1024:1536], state_ssd[i],
                                     hp['cw'], hp['cb'], hp['dtb'], hp['alog'], hp['dexp'], hp['nw'])
            sb = state_shift[i]
            prev = [sb[:, 0:1024], sb[:, 1024:2048], sb[:, 2048:3072], sb[:, 3072:3328]]
            r, w, k, v, kn, ka, g = _rwkv_prep(u, prev, None, hp['mus'], hp['prep_w'], tm=tm, bpb=1,
                                               shifted=False)
            sh = lambda t: t.reshape(B, 1, 1024)
            o, s_wkv = _wkv_step(sh(r), sh(w), sh(k), sh(v), sh(kn), sh(ka), state_wkv[i], nb=2)
            y_rwkv = _rwkv_post(o.reshape(B, 1024), r, k, v, g, hp['lnw'], hp['lnb'], hp['rk'], tm)
            x = _linear(y_ssd.reshape(B, 1024), hp['w_out'][:1024], tm=tm, tn=512, epi='resgate', res=x,
                        gate=(mod, 2), second=(y_rwkv, hp['w_out'][1024:]))
            new['ssd'].append(s_ssd)
            new['conv'].append(jnp.concatenate([cbuf[:, 1:], _raw_conv_rows(u3)], axis=1))
            new['wkv'].append(s_wkv)
            new['shift'].append(_raw_rw_rows(u))
        else:
            ap = att[i]
            qkv = _linear(x, ap['w_qkv'], tm=tm, tn=512, pro='normmod', norm=(gmix, mod, 1, 0))
            qr, kr = _rope_step(qkv, tabs)
            heads = lambda t: t.reshape(B, len(ATT_GROUPS), ATT_HEADS, ATT_HEAD_DIM)
            vn = qkv.reshape(B, len(ATT_GROUPS), 3, ATT_DIM)[:, :, 2]
            o = _attn_step(heads(qr), heads(kr), heads(vn), caches_t, i)
            x = _linear(o.reshape(B, ATT_DIM), ap['w_out'], tm=tm, tn=512, epi='resgate', res=x, gate=(mod, 2))
            rows = jnp.stack([heads(kr), heads(vn)], axis=2)
            for gi in range(len(ATT_GROUPS)):
                new['win'][gi].append(rows[:, gi][:, None])
        gmlp = P['norm_mlp'][l][None, :]
        hid = _linear(x, P['w1'][l], tm=tm, tn=512, pro='normmod', epi='relu2', norm=(gmlp, mod, 4, 3),
                      out_dtype=BF16)
        x = _linear(hid, P['w2'][l], tm=tm, tn=512, epi='resgate', res=x, gate=(mod, 5))
    y = _rmsnorm(x, norm_final[None, :], tm).reshape(B, 1, D_MODEL)
    return y, new


def kernel(x_prompt, x_sample, state_ssd, state_ssd_conv, state_wkv, state_wkv_shift, cache_win0, cache_win1, cache_win2, c_prompt, c_sample, norm_mix, norm_mlp, norm_final, ada_w, ada_b, mlp_w1, mlp_w2, hyb_w_in, hyb_w_out, ssd_conv_w, ssd_conv_b, ssd_dt_bias, ssd_a_log, ssd_d, ssd_norm_w, rwkv_mu, rwkv_w0, rwkv_w2, rwkv_a0, rwkv_a2, rwkv_g2, rwkv_k_k, rwkv_k_a, rwkv_r_k, rwkv_ln_w, rwkv_ln_b, att_w_qkv, att_w_out):
    Bp, L, _ = x_prompt.shape
    Bs = x_sample.shape[0]
    assert x_sample.shape[1] == 1

    nrow = Bp + Bs
    npad = -nrow % 16
    c_all = jnp.concatenate([c_prompt, c_sample, jnp.zeros((npad, D_MODEL), F32)], axis=0)
    mods_p, mods_s = [], []
    for l in range(DEPTH):
        mod = _linear(c_all, ada_w[l].astype(BF16), tm=nrow + npad, tn=512, pro='silu', epi='bias',
                      bias=ada_b[l][None, :])
        mods_p.append(mod[:Bp].reshape(Bp, 1, N_MOD * D_MODEL))
        mods_s.append(mod[Bp:nrow].reshape(1, Bs, N_MOD * D_MODEL))

    P = dict(norm_mix=norm_mix, norm_mlp=norm_mlp,
             w1=[mlp_w1[l].astype(BF16) for l in range(DEPTH)],
             w2=[mlp_w2[l].astype(BF16) for l in range(DEPTH)])
    hyb = [_hyb_params(i, hyb_w_in, hyb_w_out, ssd_conv_w, ssd_conv_b, ssd_dt_bias, ssd_a_log, ssd_d,
                       ssd_norm_w, rwkv_mu, rwkv_w0, rwkv_w2, rwkv_a0, rwkv_a2, rwkv_g2, rwkv_k_k,
                       rwkv_k_a, rwkv_r_k, rwkv_ln_w, rwkv_ln_b) for i in range(hyb_w_in.shape[0])]
    att = [dict(w_qkv=att_w_qkv[i].astype(BF16), w_out=att_w_out[i].astype(BF16))
           for i in range(att_w_qkv.shape[0])]

    y_p, new_p = _run_prompt(x_prompt.reshape(Bp * L, D_MODEL), mods_p, P, hyb, att, norm_final, Bp, L)
    y_s, new_s = _run_sample(x_sample.reshape(Bs, D_MODEL), mods_s, P, hyb, att, norm_final,
                             (state_ssd, state_ssd_conv, state_wkv, state_wkv_shift,
                              (cache_win0, cache_win1, cache_win2)), Bs)
    st = jnp.stack
    caches = (cache_win0, cache_win1, cache_win2)
    rolled = _roll_windows([_window_minor(c) for c in caches],
                           [_window_minor(st(new_s['win'][g])) for g in range(len(caches))])
    win_s = [jnp.transpose(t, (0, 1, 5, 2, 3, 4)) for t in rolled]
    return (y_p, y_s, st(new_p['ssd']), st(new_s['ssd']), st(new_p['conv']), st(new_s['conv']),
            st(new_p['wkv']), st(new_s['wkv']), st(new_p['shift']), st(new_s['shift']),
            st(new_p['win'][0]), win_s[0], st(new_p['win'][1]), win_s[1],
            st(new_p['win'][2]), win_s[2])
```

```python
import functools
import math

import numpy as np
import jax
import jax.numpy as jnp
from jax import lax
from jax.experimental import pallas as pl
from jax.experimental.pallas import tpu as pltpu

F32 = jnp.float32
BF16 = jnp.bfloat16
HIGHEST = lax.Precision.HIGHEST

D_MODEL = 1024
DEPTH = 4
PAST_LEN = 8192
NORM_EPS = 1e-6
N_MOD = 6
SSD_HEADS = 16
SSD_HEAD_DIM = 64
SSD_GROUPS = 2
SSD_STATE = 128
SSD_CONV = 4
SSD_CHUNK = 128
RWKV_HEADS = 16
RWKV_HEAD_DIM = 64
RWKV_LN_EPS = 64e-5
ATT_GROUPS = ((128, 1), (512, 4), (2048, 16))
ATT_HEADS = 8
ATT_HEAD_DIM = 64
ATT_DIM = ATT_HEADS * ATT_HEAD_DIM
ATT_Q_BLOCK = 128
ROPE_THETA = 500000.0
ROPE_DIM = ATT_HEAD_DIM // 4
MLP_HIDDEN = 4 * D_MODEL

U_COLS = 6144
U_Z, U_XS, U_R, U_K, U_V, U_BC, U_LW, U_DT = 0, 1024, 2048, 3072, 4096, 5120, 5632, 5888

LANES = 128
VMEM_LIMIT = 48 * 1024 * 1024
PROMPT_ROWS = 1024


def _cparams(*sem):
    return pltpu.CompilerParams(dimension_semantics=sem, vmem_limit_bytes=VMEM_LIMIT)


def _dot(a, b):
    return jnp.dot(a, b, preferred_element_type=F32)


def _dot_exact(a, b):
    return jnp.dot(a, b, preferred_element_type=F32, precision=HIGHEST)


def _dot_split(x, w2):
    hi = x.astype(BF16)
    lo = (x - hi.astype(F32)).astype(BF16)
    return _dot(jnp.concatenate([hi, lo], axis=1), w2)


def _twice(w):
    return jnp.concatenate([w, w], axis=0).astype(BF16)


def _dot_exact_nt(a, b):
    return lax.dot_general(a, b, (((1,), (1,)), ((), ())), preferred_element_type=F32, precision=HIGHEST)


def _dot_nt(a, b):
    return lax.dot_general(a, b, (((1,), (1,)), ((), ())), preferred_element_type=F32)


def _silu(x):
    return x * jax.nn.sigmoid(x)


def _softplus(x):
    return jnp.maximum(x, 0.0) + jnp.log1p(jnp.exp(-jnp.abs(x)))


def _block_ones():
    i = np.arange(LANES)
    return jnp.asarray((i[:, None] // 64 == i[None, :] // 64).astype(np.float32))


def _pair_eye():
    i = np.arange(64)
    j = np.arange(LANES)
    return jnp.asarray((i[:, None] == (j[None, :] % 64)).astype(np.float32))


def _head_expand(nheads, width):
    e = np.zeros((LANES, nheads * width), np.float32)
    for h in range(nheads):
        e[h, h * width:(h + 1) * width] = 1.0
    return jnp.asarray(e)


def _tril_ones(n):
    return jnp.asarray(np.tril(np.ones((n, n), np.float32)))


def _linear_kernel(*refs, pro, epi, two):
    refs = list(refs)
    x_ref = refs.pop(0)
    if pro == 'normmod':
        g_ref, sc_ref, sh_ref = refs.pop(0), refs.pop(0), refs.pop(0)
    w_ref = refs.pop(0)
    if two:
        x2_ref, w2_ref = refs.pop(0), refs.pop(0)
    if epi == 'bias':
        b_ref = refs.pop(0)
    if epi == 'resgate':
        res_ref, gate_ref = refs.pop(0), refs.pop(0)
    if epi == 'rope':
        tab_refs = [refs.pop(0) for _ in range(3)]
    o_ref = refs.pop(0)

    if pro == 'cast':
        h = x_ref[...].astype(BF16)
    else:
        h_ref = refs.pop(0)

        @pl.when(pl.program_id(1) == 0)
        def _():
            x = x_ref[...].astype(F32)
            if pro == 'silu':
                hh = _silu(x)
            else:
                ms = jnp.mean(x * x, axis=-1, keepdims=True)
                y = (x * lax.rsqrt(ms + NORM_EPS)) * g_ref[...]
                hh = y * (1.0 + sc_ref[...]) + sh_ref[...]
            h_ref[...] = hh.astype(BF16)

        h = h_ref[...]
    if epi == 'rope':
        tabs = [t[...] for t in tab_refs]
        for c in range(3):
            cols = slice(c * ATT_DIM, (c + 1) * ATT_DIM)
            part = _dot(h, w_ref[:, cols])
            o_ref[:, cols] = _rope_apply(part, *tabs) if c < 2 else part
        return
    acc = _dot(h, w_ref[...])
    if two:
        acc = acc + _dot(x2_ref[...].astype(BF16), w2_ref[...])
    if epi == 'bias':
        acc = acc + b_ref[...]
    elif epi == 'relu2':
        acc = jnp.square(jnp.maximum(acc, 0.0))
    elif epi == 'resgate':
        acc = res_ref[...] + gate_ref[...] * acc
    o_ref[...] = acc.astype(o_ref.dtype)


def _linear(x, w, *, tm, tn, pro='cast', epi='none', norm=None, bias=None, res=None, gate=None,
            bpb=1, out_dtype=F32, second=None, rope=None):
    M, K = x.shape
    N = w.shape[1]
    assert M % tm == 0 and N % tn == 0
    in_specs = [pl.BlockSpec((tm, K), lambda i, j: (i, 0))]
    args = [x]
    scratch = []
    if pro == 'normmod':
        g, mod, ksc, ksh = norm
        r = mod.shape[1]
        in_specs += [pl.BlockSpec((1, K), lambda i, j: (0, 0)),
                     pl.BlockSpec((None, r, K), lambda i, j: (i // bpb, 0, ksc)),
                     pl.BlockSpec((None, r, K), lambda i, j: (i // bpb, 0, ksh))]
        args += [g, mod, mod]
    if pro != 'cast':
        scratch = [pltpu.VMEM((tm, K), BF16)]
    in_specs.append(pl.BlockSpec((K, tn), lambda i, j: (0, j)))
    args.append(w)
    if second is not None:
        x2, w2 = second
        K2 = x2.shape[1]
        in_specs += [pl.BlockSpec((tm, K2), lambda i, j: (i, 0)), pl.BlockSpec((K2, tn), lambda i, j: (0, j))]
        args += [x2, w2]
    if epi == 'bias':
        in_specs.append(pl.BlockSpec((1, tn), lambda i, j: (0, j)))
        args.append(bias)
    if epi == 'resgate':
        mod, kg = gate
        r = mod.shape[1]
        nj = N // tn
        in_specs += [pl.BlockSpec((tm, tn), lambda i, j: (i, j)),
                     pl.BlockSpec((None, r, tn), lambda i, j: (i // bpb, 0, kg * nj + j))]
        args += [res, mod]
    if epi == 'rope':
        assert tn == 3 * ATT_DIM
        if rope[0].shape[0] == 1:
            in_specs += [pl.BlockSpec((1, LANES), lambda i, j: (0, 0))] * 3
        else:
            in_specs += [pl.BlockSpec((tm, LANES), lambda i, j: (i % bpb, 0))] * 3
        args += list(rope)
    return pl.pallas_call(
        functools.partial(_linear_kernel, pro=pro, epi=epi, two=second is not None),
        name="linear_%s_%s" % (pro, epi),
        grid=(M // tm, N // tn),
        in_specs=in_specs,
        out_specs=pl.BlockSpec((tm, tn), lambda i, j: (i, j)),
        out_shape=jax.ShapeDtypeStruct((M, N), out_dtype),
        scratch_shapes=scratch,
        compiler_params=_cparams("parallel", "arbitrary"),
    )(*args)


def _rmsnorm_kernel(x_ref, g_ref, o_ref):
    x = x_ref[...]
    ms = jnp.mean(x * x, axis=-1, keepdims=True)
    o_ref[...] = (x * lax.rsqrt(ms + NORM_EPS)) * g_ref[...]


def _rmsnorm(x, g, tm):
    M, K = x.shape
    return pl.pallas_call(
        _rmsnorm_kernel,
        name="final_rmsnorm",
        grid=(M // tm,),
        in_specs=[pl.BlockSpec((tm, K), lambda i: (i, 0)), pl.BlockSpec((1, K), lambda i: (0, 0))],
        out_specs=pl.BlockSpec((tm, K), lambda i: (i, 0)),
        out_shape=jax.ShapeDtypeStruct((M, K), F32),
        compiler_params=_cparams("parallel"),
    )(x, g)


def _ssd_tail(y, xs, z, d_exp, norm_w):
    y = (y + d_exp * xs) * _silu(z)
    half = y.shape[1] // SSD_GROUPS
    outs = []
    for g in range(SSD_GROUPS):
        yg = y[:, g * half:(g + 1) * half]
        ms = jnp.mean(yg * yg, axis=-1, keepdims=True)
        outs.append(yg * lax.rsqrt(ms + NORM_EPS))
    return jnp.concatenate(outs, axis=1) * norm_w


def _ssd_prompt_kernel(z_ref, xs_ref, bc_ref, dt_ref, cw_ref, cb_ref, dtb_ref, alog_ref, dexp_ref,
                       nw_ref, tril_ref, e16_ref, y_ref, st_ref, extx, extbc, state, ybuf):
    c = pl.program_id(1)
    Q = SSD_CHUNK
    NX = SSD_HEADS * SSD_HEAD_DIM

    @pl.when(c == 0)
    def _():
        extx[0:8, :] = jnp.zeros((8, NX), F32)
        extbc[0:8, :] = jnp.zeros((8, 512), F32)
        state[...] = jnp.zeros_like(state)

    extx[8:8 + Q, :] = xs_ref[...]
    extbc[8:8 + Q, :] = bc_ref[...]
    cw = cw_ref[...]
    cb = cb_ref[...]
    xc = cb[:, 0:NX]
    bcc = cb[:, NX:NX + 512]
    for j in range(SSD_CONV):
        xc = xc + extx[pl.ds(5 + j, Q), :] * cw[j:j + 1, 0:NX]
        bcc = bcc + extbc[pl.ds(5 + j, Q), :] * cw[j:j + 1, NX:NX + 512]
    extx[0:8, :] = extx[Q:Q + 8, :]
    extbc[0:8, :] = extbc[Q:Q + 8, :]
    xs = _silu(xc)
    bcs = _silu(bcc)

    dt = _softplus(dt_ref[...] + dtb_ref[...])
    a_neg = -jnp.exp(alog_ref[...])
    acs = _dot_exact(tril_ref[...], dt * a_neg)
    acs_t = acs.T
    e16 = e16_ref[...]
    eacs = jnp.exp(acs)
    dt_exp = _dot_split(dt, e16)
    eacs_exp = _dot_split(eacs, e16)
    wend_exp = _dot_split(jnp.exp(acs[Q - 1:Q, :] - acs) * dt, e16)
    xdt = (xs * dt_exp).astype(BF16)
    xw = (xs * wend_exp).astype(BF16)
    row = lax.broadcasted_iota(jnp.int32, (Q, Q), 0)
    col = lax.broadcasted_iota(jnp.int32, (Q, Q), 1)
    causal = row >= col
    HG = SSD_HEADS // SSD_GROUPS
    GW = HG * SSD_HEAD_DIM
    for g in range(SSD_GROUPS):
        b_g = bcs[:, g * SSD_STATE:(g + 1) * SSD_STATE]
        c_g = bcs[:, 256 + g * SSD_STATE:256 + (g + 1) * SSD_STATE].astype(BF16)
        cb_g = _dot_nt(c_g, b_g.astype(BF16))
        bt_g = b_g.T.astype(BF16)
        for hg in range(HG):
            h = g * HG + hg
            seg = acs[:, h:h + 1] - acs_t[h:h + 1, :]
            decay = jnp.where(causal, jnp.exp(seg), 0.0)
            scores = (cb_g * decay).astype(BF16)
            ybuf[:, h * 64:(h + 1) * 64] = _dot(scores, xdt[:, h * 64:(h + 1) * 64])
        st_g = state[g]
        y_off = _dot(c_g, st_g.astype(BF16)) * eacs_exp[:, g * GW:(g + 1) * GW]
        ybuf[:, g * GW:(g + 1) * GW] = ybuf[:, g * GW:(g + 1) * GW] + y_off
        state[g] = st_g * eacs_exp[Q - 1:Q, g * GW:(g + 1) * GW] + _dot(bt_g, xw[:, g * GW:(g + 1) * GW])

    y_ref[...] = _ssd_tail(ybuf[...], xs, z_ref[...], dexp_ref[...], nw_ref[...]).astype(y_ref.dtype)

    @pl.when(c == pl.num_programs(1) - 1)
    def _():
        st_ref[...] = state[...]


def _ssd_prompt(u, B, L, cw, cb, dtb, alog, dexp, nw):
    Q = SSD_CHUNK
    nc = L // Q
    row = lambda b, c: b * nc + c
    const = lambda shape: pl.BlockSpec(shape, lambda b, c: (0,) * len(shape))
    y, st = pl.pallas_call(
        _ssd_prompt_kernel,
        name="ssd_prompt",
        grid=(B, nc),
        in_specs=[pl.BlockSpec((Q, 1024), lambda b, c: (row(b, c), U_Z // 1024)),
                  pl.BlockSpec((Q, 1024), lambda b, c: (row(b, c), U_XS // 1024)),
                  pl.BlockSpec((Q, 512), lambda b, c: (row(b, c), U_BC // 512)),
                  pl.BlockSpec((Q, 128), lambda b, c: (row(b, c), U_DT // 128)),
                  const((SSD_CONV, 1536)), const((1, 1536)), const((1, 128)), const((1, 128)),
                  const((1, 1024)), const((1, 1024)), const((Q, Q)), const((256, 1024))],
        out_specs=[pl.BlockSpec((Q, 1024), lambda b, c: (row(b, c), 0)),
                   pl.BlockSpec((None, SSD_GROUPS, SSD_STATE, 512), lambda b, c: (b, 0, 0, 0))],
        out_shape=[jax.ShapeDtypeStruct((B * L, 1024), BF16),
                   jax.ShapeDtypeStruct((B, SSD_GROUPS, SSD_STATE, 512), F32)],
        scratch_shapes=[pltpu.VMEM((Q + 8, 1024), F32), pltpu.VMEM((Q + 8, 512), F32),
                        pltpu.VMEM((SSD_GROUPS, SSD_STATE, 512), F32), pltpu.VMEM((Q, 1024), F32)],
        compiler_params=_cparams("parallel", "arbitrary"),
    )(u, u, u, u, cw, cb, dtb, alog, dexp, nw, _tril_ones(Q), _twice(_head_expand(SSD_HEADS, 64)))
    st = st.reshape(B, SSD_GROUPS, SSD_STATE, SSD_HEADS // SSD_GROUPS, SSD_HEAD_DIM)
    st = jnp.transpose(st, (0, 1, 3, 4, 2)).reshape(B, SSD_HEADS, SSD_HEAD_DIM, SSD_STATE)
    return y, st


def _ssd_step_kernel(z_ref, xs_ref, bc_ref, dt_ref, cx_ref, cbc_ref, s_ref, cw_ref, cb_ref, dtb_ref,
                     alog_ref, dexp_ref, nw_ref, e2_ref, y_ref, so_ref, ybuf):
    NX = SSD_HEADS * SSD_HEAD_DIM
    cw = cw_ref[...]
    cb = cb_ref[...]
    cx = cx_ref[...]
    cbc = cbc_ref[...]
    xc = cb[:, 0:NX] + xs_ref[...] * cw[3:4, 0:NX]
    bcc = cb[:, NX:NX + 512] + bc_ref[...] * cw[3:4, NX:NX + 512]
    for j in range(SSD_CONV - 1):
        xc = xc + cx[j:j + 1, :] * cw[j:j + 1, 0:NX]
        bcc = bcc + cbc[j:j + 1, :] * cw[j:j + 1, NX:NX + 512]
    xs = _silu(xc)
    bcs = _silu(bcc)
    dt = _softplus(dt_ref[...] + dtb_ref[...])
    da = jnp.exp(dt * (-jnp.exp(alog_ref[...])))
    e2 = e2_ref[...]
    lane = lax.broadcasted_iota(jnp.int32, (64, LANES), 1)
    first = lane < 64
    HG = SSD_HEADS // SSD_GROUPS
    for q in range(SSD_HEADS // 2):
        xrow = xs[:, q * LANES:(q + 1) * LANES]
        diag = e2 * xrow
        ycols = []
        for s in range(2):
            h = 2 * q + s
            g = h // HG
            xcol = jnp.sum(jnp.where(first == (s == 0), diag, 0.0), axis=1, keepdims=True)
            b_row = bcs[:, g * SSD_STATE:(g + 1) * SSD_STATE]
            c_row = bcs[:, 256 + g * SSD_STATE:256 + (g + 1) * SSD_STATE]
            s_new = s_ref[h] * da[:, h:h + 1] + (xcol * dt[:, h:h + 1]) * b_row
            so_ref[h] = s_new
            ycols.append(jnp.sum(s_new * c_row, axis=1, keepdims=True))
        ypair = jnp.where(first, ycols[0], ycols[1])
        ybuf[:, q * LANES:(q + 1) * LANES] = jnp.sum(e2 * ypair, axis=0, keepdims=True)
    y_ref[...] = _ssd_tail(ybuf[...], xs, z_ref[...], dexp_ref[...], nw_ref[...]).astype(y_ref.dtype)


def _ssd_step(u, conv_x, conv_bc, s0, cw, cb, dtb, alog, dexp, nw):
    B = u.shape[0]
    const = lambda shape: pl.BlockSpec(shape, lambda b: (0,) * len(shape))
    return pl.pallas_call(
        _ssd_step_kernel,
        name="ssd_step",
        grid=(B,),
        in_specs=[pl.BlockSpec((None, 1, 1024), lambda b: (b, 0, U_Z // 1024)),
                  pl.BlockSpec((None, 1, 1024), lambda b: (b, 0, U_XS // 1024)),
                  pl.BlockSpec((None, 1, 512), lambda b: (b, 0, U_BC // 512)),
                  pl.BlockSpec((None, 1, 128), lambda b: (b, 0, U_DT // 128)),
                  pl.BlockSpec((None, 3, 1024), lambda b: (b, 0, 0)),
                  pl.BlockSpec((None, 3, 512), lambda b: (b, 0, 0)),
                  pl.BlockSpec((None, SSD_HEADS, 64, 128), lambda b: (b, 0, 0, 0)),
                  const((SSD_CONV, 1536)), const((1, 1536)), const((1, 128)), const((1, 128)),
                  const((1, 1024)), const((1, 1024)), const((64, 128))],
        out_specs=[pl.BlockSpec((None, 1, 1024), lambda b: (b, 0, 0)),
                   pl.BlockSpec((None, SSD_HEADS, 64, 128), lambda b: (b, 0, 0, 0))],
        out_shape=[jax.ShapeDtypeStruct((B, 1, 1024), BF16),
                   jax.ShapeDtypeStruct(s0.shape, F32)],
        scratch_shapes=[pltpu.VMEM((1, 1024), F32)],
        compiler_params=_cparams("parallel"),
    )(u, u, u, u, conv_x, conv_bc, s0, cw, cb, dtb, alog, dexp, nw, _pair_eye())


def _rwkv_prep_kernel(*refs, shifted, bpb):
    refs = list(refs)
    cur = [refs.pop(0) for _ in range(4)]
    prev = [refs.pop(0) for _ in range(4)]
    if shifted:
        first = [refs.pop(0) for _ in range(4)]
    mu = [refs.pop(0) for _ in range(4)]
    (w0_ref, w2_ref, a0_ref, a2_ref, g2_ref, kk_ref, ka_ref, bo_ref) = [refs.pop(0) for _ in range(8)]
    (r_o, w_o, k_o, v_o, kn_o, kka_o, g_o) = refs
    i = pl.program_id(0)

    def mixed(n):
        x = cur[n][...]
        if shifted:
            rolled = pltpu.roll(x, 1, 0)
            before = jnp.where(i % bpb == 0, first[n][...], prev[n][7:8, :])
            rid = lax.broadcasted_iota(jnp.int32, x.shape, 0)
            p = jnp.where(rid == 0, before, rolled)
        else:
            p = prev[n][...]
        return x + (p - x) * mu[n][...]

    r, k, v, lw = mixed(0), mixed(1), mixed(2), mixed(3)
    blk = lw[:, 0:LANES]
    lane = lax.broadcasted_iota(jnp.int32, blk.shape, 1)
    tw = jnp.where(lane < 64, jnp.tanh(blk), blk).astype(BF16)
    wpre = w0_ref[...] + _dot(tw, w2_ref[...])
    apre = a0_ref[...] + _dot(tw, a2_ref[...])
    wlog = -_softplus(-wpre) - 0.5
    a = jax.nn.sigmoid(apre)
    g = _dot(jax.nn.sigmoid(lw[:, LANES:2 * LANES]).astype(BF16), g2_ref[...])
    kk = k * kk_ref[...]
    kk2 = kk * kk
    bo = bo_ref[...]
    for q in range(RWKV_HEADS // 2):
        sl = slice(q * LANES, (q + 1) * LANES)
        n2 = _dot_exact(kk2[:, sl], bo)
        kn = kk[:, sl] / jnp.maximum(jnp.sqrt(n2), 1e-12)
        kn_o[:, sl] = -kn
        kka_o[:, sl] = kn * a[:, sl]
    r_o[...] = r
    w_o[...] = -jnp.exp(wlog)
    k_o[...] = k * (1.0 + (a - 1.0) * ka_ref[...])
    v_o[...] = v
    g_o[...] = g


def _rwkv_prep(u, prev, first, mus, ws, *, tm, bpb, shifted):
    M = u.shape[0]
    cols = [(1024, U_R // 1024), (1024, U_K // 1024), (1024, U_V // 1024), (256, U_LW // 256)]
    in_specs = [pl.BlockSpec((tm, c), functools.partial(lambda i, kb: (i, kb), kb=kb)) for c, kb in cols]
    args = [u] * 4
    if shifted:
        in_specs += [pl.BlockSpec((8, c), functools.partial(
            lambda i, kb: (jnp.maximum(i * (tm // 8) - 1, 0), kb), kb=kb)) for c, kb in cols]
        args += [u] * 4
        in_specs += [pl.BlockSpec((None, 1, c), lambda i: (i // bpb, 0, 0)) for c, _ in cols]
        args += list(first)
    else:
        in_specs += [pl.BlockSpec((tm, c), lambda i: (i, 0)) for c, _ in cols]
        args += list(prev)
    in_specs += [pl.BlockSpec((1, c), lambda i: (0, 0)) for c, _ in cols]
    args += list(mus)
    wshapes = [(1, 1024), (128, 1024), (1, 1024), (128, 1024), (128, 1024), (1, 1024), (1, 1024), (128, 128)]
    in_specs += [pl.BlockSpec(s, lambda i: (0, 0)) for s in wshapes]
    args += list(ws)
    return pl.pallas_call(
        functools.partial(_rwkv_prep_kernel, shifted=shifted, bpb=bpb),
        name="rwkv_prep",
        grid=(M // tm,),
        in_specs=in_specs,
        out_specs=[pl.BlockSpec((tm, 1024), lambda i: (i, 0))] * 7,
        out_shape=[jax.ShapeDtypeStruct((M, 1024), F32)] * 7,
        compiler_params=_cparams("parallel"),
    )(*args)


def _wkv_step_kernel(r_ref, lw_ref, k_ref, v_ref, kn_ref, ka_ref, s0_ref, e2_ref, bo_ref, o_ref, sT_ref, *, nb):
    e2 = e2_ref[...]
    bo = bo_ref[...]
    first = lax.broadcasted_iota(jnp.int32, (64, LANES), 1) < 64

    def pair_sum(x):
        sa = jnp.sum(jnp.where(first, x, 0.0), axis=1, keepdims=True)
        sb = jnp.sum(jnp.where(first, 0.0, x), axis=1, keepdims=True)
        return jnp.where(first, sa, sb)

    for b in range(nb):
        for p in range(RWKV_HEADS // 2):
            sl = slice(p * LANES, (p + 1) * LANES)
            S = s0_ref[b, p]
            sa = pair_sum(S * kn_ref[b, :, sl])
            vcol = _dot_exact(e2 * v_ref[b, :, sl], bo)
            S = S * jnp.exp(lw_ref[b, :, sl]) + sa * ka_ref[b, :, sl] + vcol * k_ref[b, :, sl]
            sT_ref[b, p] = S
            o = pair_sum(S * r_ref[b, :, sl])
            o_ref[b, :, sl] = jnp.sum(e2 * o, axis=0, keepdims=True)


def _wkv_step(r, lw, k, v, kn, ka, s0, *, nb):
    B = r.shape[0]
    s0p = s0.reshape(B, 8, 2, 64, 64).transpose(0, 1, 3, 2, 4).reshape(B, 8, 64, 128)
    seq = pl.BlockSpec((nb, 1, 1024), lambda b: (b, 0, 0))
    stt = pl.BlockSpec((nb, 8, 64, 128), lambda b: (b, 0, 0, 0))
    o, sT = pl.pallas_call(
        functools.partial(_wkv_step_kernel, nb=nb),
        grid=(B // nb,),
        in_specs=[seq] * 6 + [stt, pl.BlockSpec((64, 128), lambda b: (0, 0)),
                              pl.BlockSpec((128, 128), lambda b: (0, 0))],
        out_specs=[seq, stt],
        out_shape=[jax.ShapeDtypeStruct((B, 1, 1024), F32), jax.ShapeDtypeStruct((B, 8, 64, 128), F32)],
        compiler_params=_cparams("parallel"),
        name="wkv_step",
    )(r, lw, k, v, kn, ka, s0p, _pair_eye(), _block_ones())
    sT = sT.reshape(B, 8, 64, 2, 64).transpose(0, 1, 3, 2, 4).reshape(B, 16, 64, 64)
    return o, sT


WKV_CHUNK = 64


def _wkv_chunk_kernel(r_ref, lw_ref, k_ref, v_ref, kn_ref, ka_ref, tril_ref, o_ref, sT_ref, S_ref, *, nb):
    C = WKV_CHUNK
    c = pl.program_id(1)

    @pl.when(c == 0)
    def _():
        S_ref[...] = jnp.zeros_like(S_ref)

    tril = tril_ref[...]
    lane = lax.broadcasted_iota(jnp.int32, (C, LANES), 1)
    rowi = lax.broadcasted_iota(jnp.int32, (C, LANES), 0)
    first = lane < 64
    strict = rowi > (lane % 64)
    incl = rowi >= (lane % 64)
    r128 = lax.broadcasted_iota(jnp.int32, (LANES, LANES), 0)
    c128 = lax.broadcasted_iota(jnp.int32, (LANES, LANES), 1)
    diag_blocks = (r128 < 64) == (c128 < 64)
    eye = r128 == c128

    def bd(x):
        return jnp.concatenate([jnp.where(first, x, 0.0), jnp.where(first, 0.0, x)], axis=0)

    bf = lambda x: x.astype(BF16)
    pairs = [(b, p) for b in range(nb) for p in range(RWKV_HEADS // 2)]
    sls = [slice(p * LANES, (p + 1) * LANES) for _, p in pairs]
    load = lambda ref: [ref[b, :, sl] for (b, _), sl in zip(pairs, sls)]
    each = lambda f, *ls: [f(*a) for a in zip(*ls)]
    r_, lw, kt, vv, al, be = (load(ref) for ref in (r_ref, lw_ref, k_ref, v_ref, kn_ref, ka_ref))
    cs = each(lambda x: _dot_exact(tril, x), lw)
    last = each(lambda x: x[C - 1:C, :], cs)
    e_inv = each(lambda x: jnp.exp(-x), cs)
    aq = each(lambda a, x, l: a * jnp.exp(x - l), al, cs, lw)
    rq = each(lambda a, x: a * jnp.exp(x), r_, cs)
    bk = each(jnp.multiply, be, e_inv)
    kk = each(jnp.multiply, kt, e_inv)
    g = each(lambda a, q, b_, k_: _dot_exact_nt(
        jnp.concatenate([a, q], axis=0),
        jnp.concatenate([jnp.where(first, b_, 0.0), jnp.where(first, 0.0, b_),
                         jnp.where(first, k_, 0.0), jnp.where(first, 0.0, k_)], axis=0)), aq, rq, bk, kk)
    m1 = each(lambda x: bf(jnp.where(strict, x[0:C, 0:LANES], 0.0)), g)
    m2 = each(lambda x: bf(jnp.where(strict, x[0:C, LANES:2 * LANES], 0.0)), g)
    n1 = each(lambda x: bf(jnp.where(incl, x[C:2 * C, 0:LANES], 0.0)), g)
    n2 = each(lambda x: bf(jnp.where(incl, x[C:2 * C, LANES:2 * LANES], 0.0)), g)
    s0 = [S_ref[b, p] for b, p in pairs]
    s0b = each(bf, s0)
    vbd = each(lambda x: bf(bd(x)), vv)
    x = each(lambda a, s, m, v_: _dot(bf(a), s) + _dot(m, v_), aq, s0b, m2, vbd)
    mp = m1
    steps = int(math.log2(C))
    for i in range(steps):
        x = each(lambda x_, m: x_ + _dot(m, bf(bd(x_))), x, mp)
        if i + 1 < steps:
            mp = each(lambda m: bf(_dot(m, bd(m))), mp)
    o = each(lambda q, s, a, x_, b_, v_: _dot(bf(q), s) + _dot(a, bf(bd(x_))) + _dot(b_, v_),
             rq, s0b, n1, x, n2, vbd)
    for (b, _), sl, o_ in zip(pairs, sls, o):
        o_ref[b, :, sl] = o_
    e_end = each(lambda l, x_: jnp.exp(l - x_), last, cs)
    kv_t = each(lambda b_, k_, e: jnp.concatenate([b_ * e, k_ * e], axis=0).T, be, kt, e_end)
    upd = each(lambda t, x_, v_: _dot(bf(t), bf(jnp.concatenate([x_, v_], axis=0))), kv_t, x, vv)
    gcol = each(lambda l: jnp.sum(jnp.where(eye, jnp.exp(l), 0.0), axis=1, keepdims=True), last)
    for (b, p), s, u, gc in zip(pairs, s0, upd, gcol):
        S_ref[b, p] = jnp.where(diag_blocks, gc * s + u, 0.0)

    @pl.when(c == pl.num_programs(1) - 1)
    def _():
        sT_ref[...] = S_ref[...]


def _wkv_chunked(r, lw, k, v, kn, ka, *, nb):
    B, L, _ = r.shape
    C = WKV_CHUNK
    seq = pl.BlockSpec((nb, C, 1024), lambda b, c: (b, c, 0))
    stt = pl.BlockSpec((nb, 8, LANES, LANES), lambda b, c: (b, 0, 0, 0))
    o, sT = pl.pallas_call(
        functools.partial(_wkv_chunk_kernel, nb=nb),
        grid=(B // nb, L // C),
        in_specs=[seq] * 6 + [pl.BlockSpec((C, C), lambda b, c: (0, 0))],
        out_specs=[seq, stt],
        out_shape=[jax.ShapeDtypeStruct((B, L, 1024), F32), jax.ShapeDtypeStruct((B, 8, LANES, LANES), F32)],
        scratch_shapes=[pltpu.VMEM((nb, 8, LANES, LANES), F32)],
        compiler_params=_cparams("parallel", "arbitrary"),
        name="wkv_chunked",
    )(r, lw, k, v, kn, ka, _tril_ones(C))
    blocks = jnp.stack([sT[:, :, 0:64, 0:64], sT[:, :, 64:128, 64:128]], axis=2)
    return o, jnp.swapaxes(blocks, -1, -2).reshape(B, 16, 64, 64)


def _rwkv_post_kernel(o_ref, r_ref, k_ref, v_ref, g_ref, lnw_ref, lnb_ref, rk_ref, bo_ref, y_ref):
    bo = bo_ref[...]
    inv = 1.0 / RWKV_HEAD_DIM
    for q in range(RWKV_HEADS // 2):
        sl = slice(q * LANES, (q + 1) * LANES)
        o = o_ref[:, sl]
        mean = _dot_split(o, bo) * inv
        d = o - mean
        var = _dot_split(d * d, bo) * inv
        on = d * lax.rsqrt(var + RWKV_LN_EPS) * lnw_ref[:, sl] + lnb_ref[:, sl]
        bonus = _dot_split(r_ref[:, sl] * k_ref[:, sl] * rk_ref[:, sl], bo) * v_ref[:, sl]
        y_ref[:, sl] = ((on + bonus) * g_ref[:, sl]).astype(y_ref.dtype)


def _rwkv_post(o, r, k, v, g, lnw, lnb, rk, tm):
    M = o.shape[0]
    blk = pl.BlockSpec((tm, 1024), lambda i: (i, 0))
    vec = pl.BlockSpec((1, 1024), lambda i: (0, 0))
    return pl.pallas_call(
        _rwkv_post_kernel,
        name="rwkv_post",
        grid=(M // tm,),
        in_specs=[blk] * 5 + [vec] * 3 + [pl.BlockSpec((256, 128), lambda i: (0, 0))],
        out_specs=blk,
        out_shape=jax.ShapeDtypeStruct((M, 1024), BF16),
        compiler_params=_cparams("parallel"),
    )(o, r, k, v, g, lnw, lnb, rk, _twice(_block_ones()))


def _rope_tables(pos):
    half = ROPE_DIM // 2
    inv = ROPE_THETA ** (-jnp.arange(half, dtype=F32) * 2.0 / ROPE_DIM)
    ang = pos.astype(F32)[:, None] * inv
    cos, sin = jnp.cos(ang), jnp.sin(ang)
    n = pos.shape[0]
    rest = ATT_HEAD_DIM - ROPE_DIM
    c = jnp.concatenate([cos, cos, jnp.ones((n, rest), F32)], axis=1)
    s_next = jnp.concatenate([-sin, jnp.zeros((n, half + rest), F32)], axis=1)
    s_prev = jnp.concatenate([jnp.zeros((n, half), F32), sin, jnp.zeros((n, rest), F32)], axis=1)
    return tuple(jnp.concatenate([t, t], axis=1) for t in (c, s_next, s_prev))


def _rope_apply(x, c, s_next, s_prev):
    n = x.shape[1]
    reps = n // LANES
    tile = lambda t: jnp.concatenate([t] * reps, axis=1)
    half = ROPE_DIM // 2
    return x * tile(c) + pltpu.roll(x, n - half, 1) * tile(s_next) + pltpu.roll(x, half, 1) * tile(s_prev)


ATT_BLOCK_ROWS = ATT_Q_BLOCK * max(d for _, d in ATT_GROUPS)
ATT_BATCH = 4


def _attn_prompt_kernel(q_ref, kc_ref, kp_ref, vc_ref, vp_ref, o_ref, lse_ref, *, d):
    i = pl.program_id(0)
    QB = ATT_Q_BLOCK
    R = q_ref.shape[0]
    row = lax.broadcasted_iota(jnp.int32, (QB, 2 * QB), 0)
    col = lax.broadcasted_iota(jnp.int32, (QB, 2 * QB), 1)
    band = (col >= row) & (col <= row + QB)
    band_first = band & ((i > 0) | (col >= QB))
    first = lax.broadcasted_iota(jnp.int32, (QB, LANES), 1) < ATT_HEAD_DIM
    first_kv = lax.broadcasted_iota(jnp.int32, (2 * QB, LANES), 1) < ATT_HEAD_DIM
    scale = ATT_HEAD_DIM ** -0.5
    span = QB * d
    blocks = [(rho, j) for rho in range(d) for j in range(R // span)]
    each = lambda f, *ls: [f(*a) for a in zip(*ls)]
    for b0 in range(0, len(blocks), ATT_BATCH):
        batch = blocks[b0:b0 + ATT_BATCH]
        ds = lambda start, n: pl.ds(start, n, stride=d) if d > 1 else pl.ds(start, n)
        qrows = [ds(rho + span * j, QB) for rho, j in batch]
        valid = [band_first if j == 0 else band for _, j in batch]

        def keys(cur_ref, prev_ref, rho, j):
            if j > 0:
                return cur_ref[ds(rho + span * (j - 1), 2 * QB), :]
            return jnp.concatenate([prev_ref[ds(R - span + rho, QB), :], cur_ref[ds(rho, QB), :]], axis=0)

        qb = [q_ref[r, :].astype(BF16) for r in qrows]
        k2 = [keys(kc_ref, kp_ref, rho, j).astype(BF16) for rho, j in batch]
        v2 = [keys(vc_ref, vp_ref, rho, j).astype(BF16) for rho, j in batch]
        halves = [slice(0, ATT_HEAD_DIM), slice(ATT_HEAD_DIM, LANES)]
        s = [[jnp.where(vm, _dot_nt(q[:, sl], k[:, sl]) * scale, -jnp.inf) for sl in halves]
             for q, k, vm in zip(qb, k2, valid)]
        m = [[jnp.max(jnp.maximum(x[:, 0:QB], x[:, QB:2 * QB]), axis=1, keepdims=True) for x in pair] for pair in s]
        p = [[jnp.exp(x - mx).astype(BF16) for x, mx in zip(ps, ms)] for ps, ms in zip(s, m)]
        va = each(lambda v: jnp.where(first_kv, v, 1.0), v2)
        vb = each(lambda v: jnp.where(first_kv, 1.0, v), v2)
        ea = each(lambda pp, v: _dot(pp[0], v), p, va)
        eb = each(lambda pp, v: _dot(pp[1], v), p, vb)
        num = each(lambda a, b_: jnp.where(first, a, b_), ea, eb)
        den = each(lambda a, b_: pltpu.roll(jnp.where(first, b_, a), ATT_HEAD_DIM, 1), ea, eb)
        for r, n_, d_, mm in zip(qrows, num, den, m):
            o_ref[r, :] = n_ / d_
            lse_ref[r, :] = jnp.where(first, mm[0], mm[1]) + jnp.log(d_)


def _attn_prompt(qkv, B, L, gi):
    window, d = ATT_GROUPS[gi]
    R = min(ATT_BLOCK_ROWS, L)
    assert window == ATT_Q_BLOCK * d and R % (ATT_Q_BLOCK * d) == 0 and L % R == 0
    nblk = L // R
    npair = ATT_DIM // LANES
    prev = lambda i: jnp.maximum(i - 1, 0)

    def col(which, f):
        return lambda i, b, hp: (b * nblk + f(i), gi * 3 * npair + which * npair + hp)

    same = lambda i: i
    blk = lambda f: pl.BlockSpec((R, LANES), f)
    out = pl.BlockSpec((R, LANES), lambda i, b, hp: (b * nblk + i, hp))
    return pl.pallas_call(
        functools.partial(_attn_prompt_kernel, d=d),
        grid=(nblk, B, npair),
        in_specs=[blk(col(0, same)), blk(col(1, same)), blk(col(1, prev)), blk(col(2, same)), blk(col(2, prev))],
        out_specs=[out, out],
        out_shape=[jax.ShapeDtypeStruct((B * L, ATT_DIM), F32)] * 2,
        compiler_params=_cparams("arbitrary", "arbitrary", "arbitrary"),
        name="attn_prompt_d%d" % d,
    )(qkv, qkv, qkv, qkv, qkv)


def _attn_out_kernel(o0, l0, o1, l1, o2, l2, w_ref, res_ref, gate_ref, out_ref, h_ref):
    @pl.when(pl.program_id(1) == 0)
    def _():
        m = jnp.maximum(jnp.maximum(l0[...], l1[...]), l2[...])
        a0, a1, a2 = jnp.exp(l0[...] - m), jnp.exp(l1[...] - m), jnp.exp(l2[...] - m)
        o = (a0 * o0[...] + a1 * o1[...] + a2 * o2[...]) / (a0 + a1 + a2)
        h_ref[...] = o.astype(BF16)

    out_ref[...] = res_ref[...] + gate_ref[...] * _dot(h_ref[...], w_ref[...])


def _attn_out(ols, w, res, mod, kg, *, tm, tn, bpb):
    M = res.shape[0]
    N = w.shape[1]
    r = mod.shape[1]
    nj = N // tn
    part = pl.BlockSpec((tm, 512), lambda i, j: (i, 0))
    return pl.pallas_call(
        _attn_out_kernel,
        name="attn_out",
        grid=(M // tm, nj),
        in_specs=[part] * 6 + [pl.BlockSpec((512, tn), lambda i, j: (0, j)),
                               pl.BlockSpec((tm, tn), lambda i, j: (i, j)),
                               pl.BlockSpec((None, r, tn), lambda i, j: (i // bpb, 0, kg * nj + j))],
        out_specs=pl.BlockSpec((tm, tn), lambda i, j: (i, j)),
        out_shape=jax.ShapeDtypeStruct((M, N), F32),
        scratch_shapes=[pltpu.VMEM((tm, 512), BF16)],
        compiler_params=_cparams("parallel", "arbitrary"),
    )(*ols, w, res, mod)


def _attn_step_kernel(q_ref, kn_ref, vn_ref, c0_ref, c1_ref, c2_ref, o_ref):
    caches = (c0_ref, c1_ref, c2_ref)
    scale = ATT_HEAD_DIM ** -0.5
    hl = lax.broadcasted_iota(jnp.int32, (ATT_HEAD_DIM, ATT_HEADS), 1)
    ng = len(ATT_GROUPS)
    combos = [(h, gi) for h in range(ATT_HEADS) for gi in range(ng)]
    each = lambda f, *ls: [f(*a) for a in zip(*ls)]
    col = lambda ref: [ref[gi][:, h:h + 1] for h, gi in combos]
    q, kn, vn = col(q_ref), col(kn_ref), col(vn_ref)
    vis = [lax.broadcasted_iota(jnp.int32, (1, w), 1) % d == 0 for w, d in ATT_GROUPS]
    s = [jnp.where(vis[gi], jnp.sum(caches[gi][0, h] * q_, axis=0, keepdims=True) * scale, -jnp.inf)
         for (h, gi), q_ in zip(combos, q)]
    s_new = each(lambda a, b: jnp.sum(a * b, axis=0, keepdims=True) * scale, q, kn)
    m = each(lambda a, b: jnp.maximum(jnp.max(a, axis=1, keepdims=True), b), s, s_new)
    p = each(lambda a, b: jnp.exp(a - b), s, m)
    p_new = each(lambda a, b: jnp.exp(a - b), s_new, m)
    l = each(lambda a, b: jnp.sum(a, axis=1, keepdims=True) + b, p, p_new)
    acc = [jnp.sum(caches[gi][1, h] * p_, axis=1, keepdims=True) + pn * v_
           for (h, gi), p_, pn, v_ in zip(combos, p, p_new, vn)]
    og = each(lambda a, b: a / b, acc, l)
    lse = each(lambda a, b: a + jnp.log(b), m, l)
    out = jnp.zeros((ATT_HEAD_DIM, ATT_HEADS), F32)
    for h in range(ATT_HEADS):
        os_, ls_ = og[h * ng:(h + 1) * ng], lse[h * ng:(h + 1) * ng]
        mm = jnp.maximum(jnp.maximum(ls_[0], ls_[1]), ls_[2])
        ws = [jnp.exp(x - mm) for x in ls_]
        o = (ws[0] * os_[0] + ws[1] * os_[1] + ws[2] * os_[2]) / (ws[0] + ws[1] + ws[2])
        out = jnp.where(hl == h, o, out)
    o_ref[...] = out


def _window_minor(c):
    return jnp.transpose(c, (0, 1, 3, 4, 5, 2))


def _attn_step(q3, kn3, vn3, caches_t, layer):
    B = q3.shape[0]
    for (window, d), c in zip(ATT_GROUPS, caches_t):
        assert c.shape[-1] == window and window // d == 128
    cols = lambda t: jnp.swapaxes(t, -1, -2)
    specs = [pl.BlockSpec((None, None) + c.shape[2:], lambda b: (layer, b, 0, 0, 0, 0)) for c in caches_t]
    new = pl.BlockSpec((None, len(ATT_GROUPS), ATT_HEAD_DIM, ATT_HEADS), lambda b: (b, 0, 0, 0))
    o = pl.pallas_call(
        _attn_step_kernel,
        grid=(B,),
        in_specs=[new] * 3 + specs,
        out_specs=pl.BlockSpec((None, ATT_HEAD_DIM, ATT_HEADS), lambda b: (b, 0, 0)),
        out_shape=jax.ShapeDtypeStruct((B, ATT_HEAD_DIM, ATT_HEADS), F32),
        compiler_params=_cparams("parallel"),
        name="attn_step",
    )(cols(q3), cols(kn3), cols(vn3), *caches_t)
    return jnp.swapaxes(o, -1, -2)


def _roll_kernel(c0, c1, c2, n0, n1, n2, o0, o1, o2):
    for c_ref, n_ref, o_ref in ((c0, n0, o0), (c1, n1, o1), (c2, n2, o2)):
        w = c_ref.shape[-1]
        rows = c_ref.shape[0] * c_ref.shape[1] * c_ref.shape[2]
        x = c_ref[...].reshape(rows, w)
        lane = lax.broadcasted_iota(jnp.int32, (rows, w), 1)
        y = jnp.where(lane == w - 1, n_ref[...].reshape(rows, 1), pltpu.roll(x, w - 1, 1))
        o_ref[...] = y.reshape(o_ref.shape)


def _roll_windows(caches_t, rows_t):
    NC, B = caches_t[0].shape[:2]
    spec = lambda a: pl.BlockSpec((None, None) + a.shape[2:], lambda i, b: (i, b, 0, 0, 0, 0))
    return pl.pallas_call(
        _roll_kernel,
        grid=(NC, B),
        in_specs=[spec(c) for c in caches_t] + [spec(r) for r in rows_t],
        out_specs=[spec(c) for c in caches_t],
        out_shape=[jax.ShapeDtypeStruct(c.shape, c.dtype) for c in caches_t],
        compiler_params=_cparams("parallel", "parallel"),
        name="roll_windows",
    )(*caches_t, *rows_t)


def _hyb_params(i, hyb_w_in, hyb_w_out, ssd_conv_w, ssd_conv_b, ssd_dt_bias, ssd_a_log, ssd_d, ssd_norm_w,
                rwkv_mu, rwkv_w0, rwkv_w2, rwkv_a0, rwkv_a2, rwkv_g2, rwkv_k_k, rwkv_k_a, rwkv_r_k,
                rwkv_ln_w, rwkv_ln_b):
    w = hyb_w_in[i]
    rw0 = 2576
    w_perm = jnp.concatenate(
        [w[:, 0:1024], w[:, 1024:2048], w[:, rw0:rw0 + 3072], w[:, 2048:2560], w[:, rw0 + 3072:rw0 + 3328],
         w[:, 2560:2576], jnp.zeros((D_MODEL, U_COLS - U_DT - 16), F32)], axis=1).astype(BF16)
    pad128 = lambda v: jnp.concatenate([v, jnp.zeros((LANES - v.shape[0],), F32)])[None, :]
    z64 = jnp.zeros((64, 1024), F32)
    mu = rwkv_mu[i]
    p = dict(
        w_in=w_perm, w_out=hyb_w_out[i].astype(BF16),
        cw=ssd_conv_w[i], cb=ssd_conv_b[i][None, :], dtb=pad128(ssd_dt_bias[i]), alog=pad128(ssd_a_log[i]),
        dexp=jnp.repeat(ssd_d[i], SSD_HEAD_DIM)[None, :], nw=ssd_norm_w[i][None, :],
        mus=[mu[None, 0:1024], mu[None, 1024:2048], mu[None, 2048:3072], mu[None, 3072:3328]],
        prep_w=[rwkv_w0[i][None, :], jnp.concatenate([rwkv_w2[i], z64]).astype(BF16),
                rwkv_a0[i][None, :], jnp.concatenate([z64, rwkv_a2[i]]).astype(BF16),
                rwkv_g2[i].astype(BF16), rwkv_k_k[i][None, :], rwkv_k_a[i][None, :], _block_ones()],
        lnw=rwkv_ln_w[i][None, :], lnb=rwkv_ln_b[i][None, :], rk=rwkv_r_k[i].reshape(1, 1024),
    )
    return p


def _raw_conv_rows(u_rows):
    return jnp.concatenate([u_rows[..., U_XS:U_XS + 1024], u_rows[..., U_BC:U_BC + 512]], axis=-1)


def _raw_rw_rows(u_rows):
    return jnp.concatenate([u_rows[..., U_R:U_R + 3072], u_rows[..., U_LW:U_LW + 256]], axis=-1)


def _run_prompt(x, mods, P, hyb, att, norm_final, B, L):
    T = B * L
    big, half = PROMPT_ROWS, PROMPT_ROWS // 2
    lin = lambda *a, tm, **kw: _linear(*a, tm=tm, bpb=L // tm, **kw)
    new = dict(ssd=[], conv=[], wkv=[], shift=[], win=[[], [], []])
    tabs = _rope_tables(jnp.arange(L))
    for l in range(DEPTH):
        mod = mods[l]
        i = l // 2
        gmix = P['norm_mix'][l][None, :]
        if l % 2 == 0:
            hp = hyb[i]
            u = lin(x, hp['w_in'], tm=big, tn=1024, pro='normmod', norm=(gmix, mod, 1, 0))
            y_ssd, s_ssd = _ssd_prompt(u, B, L, hp['cw'], hp['cb'], hp['dtb'], hp['alog'], hp['dexp'], hp['nw'])
            zeros = [jnp.zeros((B, 1, c), F32) for c in (1024, 1024, 1024, 256)]
            r, w, k, v, kn, ka, g = _rwkv_prep(u, None, zeros, hp['mus'], hp['prep_w'], tm=256, bpb=L // 256,
                                               shifted=True)
            sh = lambda t: t.reshape(B, L, 1024)
            o, s_wkv = _wkv_chunked(sh(r), sh(w), sh(k), sh(v), sh(kn), sh(ka), nb=B)
            y_rwkv = _rwkv_post(o.reshape(T, 1024), r, k, v, g, hp['lnw'], hp['lnb'], hp['rk'], 256)
            x = lin(y_ssd, hp['w_out'][:1024], tm=big, tn=1024, epi='resgate', res=x, gate=(mod, 2),
                    second=(y_rwkv, hp['w_out'][1024:]))
            u3 = u.reshape(B, L, U_COLS)
            new['ssd'].append(s_ssd)
            new['conv'].append(_raw_conv_rows(u3[:, L - (SSD_CONV - 1):]))
            new['wkv'].append(s_wkv)
            new['shift'].append(_raw_rw_rows(u3[:, L - 1]))
        else:
            ap = att[i]
            qkv = lin(x, ap['w_qkv'], tm=big, tn=3 * ATT_DIM, pro='normmod', epi='rope',
                      norm=(gmix, mod, 1, 0), rope=tabs)
            ols = []
            q3 = qkv.reshape(B, L, len(ATT_GROUPS), 3, ATT_DIM)
            for gi, (window, d) in enumerate(ATT_GROUPS):
                ols += _attn_prompt(qkv, B, L, gi)
                keep = min(window, L)
                kv = q3[:, L - keep:, gi, 1:3]
                new['win'][gi].append(kv.reshape(B, keep, 2, ATT_HEADS, ATT_HEAD_DIM))
            x = _attn_out(ols, ap['w_out'], x, mod, 2, tm=half, tn=1024, bpb=L // half)
        gmlp = P['norm_mlp'][l][None, :]
        hid = lin(x, P['w1'][l], tm=big, tn=1024, pro='normmod', epi='relu2', norm=(gmlp, mod, 4, 3),
                  out_dtype=BF16)
        x = lin(hid, P['w2'][l], tm=half, tn=1024, epi='resgate', res=x, gate=(mod, 5))
    y = _rmsnorm(x, norm_final[None, :], half).reshape(B, L, D_MODEL)
    return y, new


def _run_sample(x, mods, P, hyb, att, norm_final, states, B):
    state_ssd, state_conv, state_wkv, state_shift, caches = states
    tm = B
    new = dict(ssd=[], conv=[], wkv=[], shift=[], win=[[], [], []])
    tabs = _rope_tables(jnp.full((1,), PAST_LEN, jnp.int32))
    caches_t = [_window_minor(c) for c in caches]
    for l in range(DEPTH):
        mod = mods[l]
        i = l // 2
        gmix = P['norm_mix'][l][None, :]
        if l % 2 == 0:
            hp = hyb[i]
            u = _linear(x, hp['w_in'], tm=tm, tn=512, pro='normmod', norm=(gmix, mod, 1, 0))
            u3 = u.reshape(B, 1, U_COLS)
            cbuf = state_conv[i]
            y_ssd, s_ssd = _ssd_step(u3, cbuf[:, :, 0:1024], cbuf[:, :, 1024:1536], state_ssd[i],
                                     hp['cw'], hp['cb'], hp['dtb'], hp['alog'], hp['dexp'], hp['nw'])
            sb = state_shift[i]
            prev = [sb[:, 0:1024], sb[:, 1024:2048], sb[:, 2048:3072], sb[:, 3072:3328]]
            r, w, k, v, kn, ka, g = _rwkv_prep(u, prev, None, hp['mus'], hp['prep_w'], tm=tm, bpb=1,
                                               shifted=False)
            sh = lambda t: t.reshape(B, 1, 1024)
            o, s_wkv = _wkv_step(sh(r), sh(w), sh(k), sh(v), sh(kn), sh(ka), state_wkv[i], nb=2)
            y_rwkv = _rwkv_post(o.reshape(B, 1024), r, k, v, g, hp['lnw'], hp['lnb'], hp['rk'], tm)
            x = _linear(y_ssd.reshape(B, 1024), hp['w_out'][:1024], tm=tm, tn=512, epi='resgate', res=x,
                        gate=(mod, 2), second=(y_rwkv, hp['w_out'][1024:]))
            new['ssd'].append(s_ssd)
            new['conv'].append(jnp.concatenate([cbuf[:, 1:], _raw_conv_rows(u3)], axis=1))
            new['wkv'].append(s_wkv)
            new['shift'].append(_raw_rw_rows(u))
        else:
            ap = att[i]
            qkv = _linear(x, ap['w_qkv'], tm=tm, tn=3 * ATT_DIM, pro='normmod', epi='rope',
                          norm=(gmix, mod, 1, 0), rope=tabs)
            parts = qkv.reshape(B, len(ATT_GROUPS), 3, ATT_HEADS, ATT_HEAD_DIM)
            o = _attn_step(parts[:, :, 0], parts[:, :, 1], parts[:, :, 2], caches_t, i)
            x = _linear(o.reshape(B, ATT_DIM), ap['w_out'], tm=tm, tn=512, epi='resgate', res=x, gate=(mod, 2))
            for gi in range(len(ATT_GROUPS)):
                new['win'][gi].append(parts[:, gi, 1:3][:, None])
        gmlp = P['norm_mlp'][l][None, :]
        hid = _linear(x, P['w1'][l], tm=tm, tn=512, pro='normmod', epi='relu2', norm=(gmlp, mod, 4, 3),
                      out_dtype=BF16)
        x = _linear(hid, P['w2'][l], tm=tm, tn=512, epi='resgate', res=x, gate=(mod, 5))
    y = _rmsnorm(x, norm_final[None, :], tm).reshape(B, 1, D_MODEL)
    return y, new


def kernel(x_prompt, x_sample, state_ssd, state_ssd_conv, state_wkv, state_wkv_shift, cache_win0, cache_win1, cache_win2, c_prompt, c_sample, norm_mix, norm_mlp, norm_final, ada_w, ada_b, mlp_w1, mlp_w2, hyb_w_in, hyb_w_out, ssd_conv_w, ssd_conv_b, ssd_dt_bias, ssd_a_log, ssd_d, ssd_norm_w, rwkv_mu, rwkv_w0, rwkv_w2, rwkv_a0, rwkv_a2, rwkv_g2, rwkv_k_k, rwkv_k_a, rwkv_r_k, rwkv_ln_w, rwkv_ln_b, att_w_qkv, att_w_out):
    Bp, L, _ = x_prompt.shape
    Bs = x_sample.shape[0]
    assert x_sample.shape[1] == 1

    nrow = Bp + Bs
    npad = -nrow % 16
    c_all = jnp.concatenate([c_prompt, c_sample, jnp.zeros((npad, D_MODEL), F32)], axis=0)
    mods_p, mods_s = [], []
    for l in range(DEPTH):
        mod = _linear(c_all, ada_w[l].astype(BF16), tm=nrow + npad, tn=512, pro='silu', epi='bias',
                      bias=ada_b[l][None, :])
        mods_p.append(mod[:Bp].reshape(Bp, 1, N_MOD * D_MODEL))
        mods_s.append(mod[Bp:nrow].reshape(1, Bs, N_MOD * D_MODEL))

    P = dict(norm_mix=norm_mix, norm_mlp=norm_mlp,
             w1=[mlp_w1[l].astype(BF16) for l in range(DEPTH)],
             w2=[mlp_w2[l].astype(BF16) for l in range(DEPTH)])
    hyb = [_hyb_params(i, hyb_w_in, hyb_w_out, ssd_conv_w, ssd_conv_b, ssd_dt_bias, ssd_a_log, ssd_d,
                       ssd_norm_w, rwkv_mu, rwkv_w0, rwkv_w2, rwkv_a0, rwkv_a2, rwkv_g2, rwkv_k_k,
                       rwkv_k_a, rwkv_r_k, rwkv_ln_w, rwkv_ln_b) for i in range(hyb_w_in.shape[0])]
    att = [dict(w_qkv=att_w_qkv[i].astype(BF16), w_out=att_w_out[i].astype(BF16))
           for i in range(att_w_qkv.shape[0])]

    y_p, new_p = _run_prompt(x_prompt.reshape(Bp * L, D_MODEL), mods_p, P, hyb, att, norm_final, Bp, L)
    y_s, new_s = _run_sample(x_sample.reshape(Bs, D_MODEL), mods_s, P, hyb, att, norm_final,
                             (state_ssd, state_ssd_conv, state_wkv, state_wkv_shift,
                              (cache_win0, cache_win1, cache_win2)), Bs)
    st = jnp.stack
    caches = (cache_win0, cache_win1, cache_win2)
    rolled = _roll_windows([_window_minor(c) for c in caches],
                           [_window_minor(st(new_s['win'][g])) for g in range(len(caches))])
    win_s = [jnp.transpose(t, (0, 1, 5, 2, 3, 4)) for t in rolled]
    return (y_p, y_s, st(new_p['ssd']), st(new_s['ssd']), st(new_p['conv']), st(new_s['conv']),
            st(new_p['wkv']), st(new_s['wkv']), st(new_p['shift']), st(new_s['shift']),
            st(new_p['win'][0]), win_s[0], st(new_p['win'][1]), win_s[1],
            st(new_p['win'][2]), win_s[2])
```

```python
import functools
import math

import numpy as np
import jax
import jax.numpy as jnp
from jax import lax
from jax.experimental import pallas as pl
from jax.experimental.pallas import tpu as pltpu

F32 = jnp.float32
BF16 = jnp.bfloat16
HIGHEST = lax.Precision.HIGHEST

D_MODEL = 1024
DEPTH = 4
PAST_LEN = 8192
NORM_EPS = 1e-6
N_MOD = 6
SSD_HEADS = 16
SSD_HEAD_DIM = 64
SSD_GROUPS = 2
SSD_STATE = 128
SSD_CONV = 4
SSD_CHUNK = 128
RWKV_HEADS = 16
RWKV_HEAD_DIM = 64
RWKV_LN_EPS = 64e-5
ATT_GROUPS = ((128, 1), (512, 4), (2048, 16))
ATT_HEADS = 8
ATT_HEAD_DIM = 64
ATT_DIM = ATT_HEADS * ATT_HEAD_DIM
ATT_Q_BLOCK = 128
ROPE_THETA = 500000.0
ROPE_DIM = ATT_HEAD_DIM // 4
MLP_HIDDEN = 4 * D_MODEL

U_COLS = 6144
U_Z, U_XS, U_R, U_K, U_V, U_BC, U_LW, U_DT = 0, 1024, 2048, 3072, 4096, 5120, 5632, 5888

LANES = 128
VMEM_LIMIT = 48 * 1024 * 1024
PROMPT_ROWS = 1024


def _cparams(*sem):
    return pltpu.CompilerParams(dimension_semantics=sem, vmem_limit_bytes=VMEM_LIMIT)


def _dot(a, b):
    return jnp.dot(a, b, preferred_element_type=F32)


def _dot_exact(a, b):
    return jnp.dot(a, b, preferred_element_type=F32, precision=HIGHEST)


def _dot_split(x, w2):
    hi = x.astype(BF16)
    lo = (x - hi.astype(F32)).astype(BF16)
    return _dot(jnp.concatenate([hi, lo], axis=1), w2)


def _twice(w):
    return jnp.concatenate([w, w], axis=0).astype(BF16)


def _dot_exact_nt(a, b):
    return lax.dot_general(a, b, (((1,), (1,)), ((), ())), preferred_element_type=F32, precision=HIGHEST)


def _dot_nt(a, b):
    return lax.dot_general(a, b, (((1,), (1,)), ((), ())), preferred_element_type=F32)


def _silu(x):
    return x * jax.nn.sigmoid(x)


def _softplus(x):
    return jnp.maximum(x, 0.0) + jnp.log1p(jnp.exp(-jnp.abs(x)))


def _block_ones():
    i = np.arange(LANES)
    return jnp.asarray((i[:, None] // 64 == i[None, :] // 64).astype(np.float32))


def _pair_eye():
    i = np.arange(64)
    j = np.arange(LANES)
    return jnp.asarray((i[:, None] == (j[None, :] % 64)).astype(np.float32))


def _head_expand(nheads, width):
    e = np.zeros((LANES, nheads * width), np.float32)
    for h in range(nheads):
        e[h, h * width:(h + 1) * width] = 1.0
    return jnp.asarray(e)


def _tril_ones(n):
    return jnp.asarray(np.tril(np.ones((n, n), np.float32)))


def _linear_kernel(*refs, pro, epi, two):
    refs = list(refs)
    x_ref = refs.pop(0)
    if pro == 'normmod':
        g_ref, sc_ref, sh_ref = refs.pop(0), refs.pop(0), refs.pop(0)
    w_ref = refs.pop(0)
    if two:
        x2_ref, w2_ref = refs.pop(0), refs.pop(0)
    if epi == 'bias':
        b_ref = refs.pop(0)
    if epi == 'resgate':
        res_ref, gate_ref = refs.pop(0), refs.pop(0)
    if epi == 'rope':
        tab_refs = [refs.pop(0) for _ in range(3)]
    o_ref = refs.pop(0)

    if pro == 'cast':
        h = x_ref[...].astype(BF16)
    else:
        h_ref = refs.pop(0)

        @pl.when(pl.program_id(1) == 0)
        def _():
            x = x_ref[...].astype(F32)
            if pro == 'silu':
                hh = _silu(x)
            else:
                ms = jnp.mean(x * x, axis=-1, keepdims=True)
                y = (x * lax.rsqrt(ms + NORM_EPS)) * g_ref[...]
                hh = y * (1.0 + sc_ref[...]) + sh_ref[...]
            h_ref[...] = hh.astype(BF16)

        h = h_ref[...]
    if epi == 'rope':
        tabs = [t[...] for t in tab_refs]
        for c in range(3):
            cols = slice(c * ATT_DIM, (c + 1) * ATT_DIM)
            part = _dot(h, w_ref[:, cols])
            o_ref[:, cols] = _rope_apply(part, *tabs) if c < 2 else part
        return
    acc = _dot(h, w_ref[...].astype(BF16))
    if two:
        acc = acc + _dot(x2_ref[...].astype(BF16), w2_ref[...])
    if epi == 'bias':
        acc = acc + b_ref[...]
    elif epi == 'relu2':
        acc = jnp.square(jnp.maximum(acc, 0.0))
    elif epi == 'resgate':
        acc = res_ref[...] + gate_ref[...] * acc
    o_ref[...] = acc.astype(o_ref.dtype)


def _linear(x, w, *, tm, tn, pro='cast', epi='none', norm=None, bias=None, res=None, gate=None,
            bpb=1, out_dtype=F32, second=None, rope=None, w_layer=0):
    M, K = x.shape
    N = w.shape[-1]
    assert M % tm == 0 and N % tn == 0
    in_specs = [pl.BlockSpec((tm, K), lambda i, j: (i, 0))]
    args = [x]
    scratch = []
    if pro == 'normmod':
        g, mod, ksc, ksh = norm
        r = mod.shape[1]
        in_specs += [pl.BlockSpec((1, K), lambda i, j: (0, 0)),
                     pl.BlockSpec((None, r, K), lambda i, j: (i // bpb, 0, ksc)),
                     pl.BlockSpec((None, r, K), lambda i, j: (i // bpb, 0, ksh))]
        args += [g, mod, mod]
    if pro != 'cast':
        scratch = [pltpu.VMEM((tm, K), BF16)]
    if w.ndim == 3:
        in_specs.append(pl.BlockSpec((None, K, tn), lambda i, j: (w_layer, 0, j)))
    else:
        in_specs.append(pl.BlockSpec((K, tn), lambda i, j: (0, j)))
    args.append(w)
    if second is not None:
        x2, w2 = second
        K2 = x2.shape[1]
        in_specs += [pl.BlockSpec((tm, K2), lambda i, j: (i, 0)), pl.BlockSpec((K2, tn), lambda i, j: (0, j))]
        args += [x2, w2]
    if epi == 'bias':
        in_specs.append(pl.BlockSpec((1, tn), lambda i, j: (0, j)))
        args.append(bias)
    if epi == 'resgate':
        mod, kg = gate
        r = mod.shape[1]
        nj = N // tn
        in_specs += [pl.BlockSpec((tm, tn), lambda i, j: (i, j)),
                     pl.BlockSpec((None, r, tn), lambda i, j: (i // bpb, 0, kg * nj + j))]
        args += [res, mod]
    if epi == 'rope':
        assert tn == 3 * ATT_DIM
        if rope[0].shape[0] == 1:
            in_specs += [pl.BlockSpec((1, LANES), lambda i, j: (0, 0))] * 3
        else:
            in_specs += [pl.BlockSpec((tm, LANES), lambda i, j: (i % bpb, 0))] * 3
        args += list(rope)
    return pl.pallas_call(
        functools.partial(_linear_kernel, pro=pro, epi=epi, two=second is not None),
        name="linear_%s_%s" % (pro, epi),
        grid=(M // tm, N // tn),
        in_specs=in_specs,
        out_specs=pl.BlockSpec((tm, tn), lambda i, j: (i, j)),
        out_shape=jax.ShapeDtypeStruct((M, N), out_dtype),
        scratch_shapes=scratch,
        compiler_params=_cparams("parallel", "arbitrary"),
    )(*args)


def _rmsnorm_kernel(x_ref, g_ref, o_ref):
    x = x_ref[...]
    ms = jnp.mean(x * x, axis=-1, keepdims=True)
    o_ref[...] = (x * lax.rsqrt(ms + NORM_EPS)) * g_ref[...]


def _rmsnorm(x, g, tm):
    M, K = x.shape
    return pl.pallas_call(
        _rmsnorm_kernel,
        name="final_rmsnorm",
        grid=(M // tm,),
        in_specs=[pl.BlockSpec((tm, K), lambda i: (i, 0)), pl.BlockSpec((1, K), lambda i: (0, 0))],
        out_specs=pl.BlockSpec((tm, K), lambda i: (i, 0)),
        out_shape=jax.ShapeDtypeStruct((M, K), F32),
        compiler_params=_cparams("parallel"),
    )(x, g)


def _ssd_tail(y, xs, z, d_exp, norm_w):
    y = (y + d_exp * xs) * _silu(z)
    half = y.shape[1] // SSD_GROUPS
    outs = []
    for g in range(SSD_GROUPS):
        yg = y[:, g * half:(g + 1) * half]
        ms = jnp.mean(yg * yg, axis=-1, keepdims=True)
        outs.append(yg * lax.rsqrt(ms + NORM_EPS))
    return jnp.concatenate(outs, axis=1) * norm_w


def _ssd_prompt_kernel(z_ref, xs_ref, bc_ref, dt_ref, cw_ref, cb_ref, dtb_ref, alog_ref, dexp_ref,
                       nw_ref, tril_ref, e16_ref, y_ref, st_ref, extx, extbc, state, ybuf):
    c = pl.program_id(1)
    Q = SSD_CHUNK
    NX = SSD_HEADS * SSD_HEAD_DIM

    @pl.when(c == 0)
    def _():
        extx[0:8, :] = jnp.zeros((8, NX), F32)
        extbc[0:8, :] = jnp.zeros((8, 512), F32)
        state[...] = jnp.zeros_like(state)

    extx[8:8 + Q, :] = xs_ref[...]
    extbc[8:8 + Q, :] = bc_ref[...]
    cw = cw_ref[...]
    cb = cb_ref[...]
    xc = cb[:, 0:NX]
    bcc = cb[:, NX:NX + 512]
    for j in range(SSD_CONV):
        xc = xc + extx[pl.ds(5 + j, Q), :] * cw[j:j + 1, 0:NX]
        bcc = bcc + extbc[pl.ds(5 + j, Q), :] * cw[j:j + 1, NX:NX + 512]
    extx[0:8, :] = extx[Q:Q + 8, :]
    extbc[0:8, :] = extbc[Q:Q + 8, :]
    xs = _silu(xc)
    bcs = _silu(bcc)

    dt = _softplus(dt_ref[...] + dtb_ref[...])
    a_neg = -jnp.exp(alog_ref[...])
    acs = _dot_exact(tril_ref[...], dt * a_neg)
    acs_t = acs.T
    e16 = e16_ref[...]
    eacs = jnp.exp(acs)
    dt_exp = _dot_split(dt, e16)
    eacs_exp = _dot_split(eacs, e16)
    wend_exp = _dot_split(jnp.exp(acs[Q - 1:Q, :] - acs) * dt, e16)
    xdt = (xs * dt_exp).astype(BF16)
    xw = (xs * wend_exp).astype(BF16)
    row = lax.broadcasted_iota(jnp.int32, (Q, Q), 0)
    col = lax.broadcasted_iota(jnp.int32, (Q, Q), 1)
    causal = row >= col
    HG = SSD_HEADS // SSD_GROUPS
    GW = HG * SSD_HEAD_DIM
    for g in range(SSD_GROUPS):
        b_g = bcs[:, g * SSD_STATE:(g + 1) * SSD_STATE]
        c_g = bcs[:, 256 + g * SSD_STATE:256 + (g + 1) * SSD_STATE].astype(BF16)
        cb_g = _dot_nt(c_g, b_g.astype(BF16))
        bt_g = b_g.T.astype(BF16)
        for hg in range(HG):
            h = g * HG + hg
            seg = acs[:, h:h + 1] - acs_t[h:h + 1, :]
            decay = jnp.where(causal, jnp.exp(seg), 0.0)
            scores = (cb_g * decay).astype(BF16)
            ybuf[:, h * 64:(h + 1) * 64] = _dot(scores, xdt[:, h * 64:(h + 1) * 64])
        st_g = state[g]
        y_off = _dot(c_g, st_g.astype(BF16)) * eacs_exp[:, g * GW:(g + 1) * GW]
        ybuf[:, g * GW:(g + 1) * GW] = ybuf[:, g * GW:(g + 1) * GW] + y_off
        state[g] = st_g * eacs_exp[Q - 1:Q, g * GW:(g + 1) * GW] + _dot(bt_g, xw[:, g * GW:(g + 1) * GW])

    y_ref[...] = _ssd_tail(ybuf[...], xs, z_ref[...], dexp_ref[...], nw_ref[...]).astype(y_ref.dtype)

    @pl.when(c == pl.num_programs(1) - 1)
    def _():
        st_ref[...] = state[...]


def _ssd_prompt(u, B, L, cw, cb, dtb, alog, dexp, nw):
    Q = SSD_CHUNK
    nc = L // Q
    row = lambda b, c: b * nc + c
    const = lambda shape: pl.BlockSpec(shape, lambda b, c: (0,) * len(shape))
    y, st = pl.pallas_call(
        _ssd_prompt_kernel,
        name="ssd_prompt",
        grid=(B, nc),
        in_specs=[pl.BlockSpec((Q, 1024), lambda b, c: (row(b, c), U_Z // 1024)),
                  pl.BlockSpec((Q, 1024), lambda b, c: (row(b, c), U_XS // 1024)),
                  pl.BlockSpec((Q, 512), lambda b, c: (row(b, c), U_BC // 512)),
                  pl.BlockSpec((Q, 128), lambda b, c: (row(b, c), U_DT // 128)),
                  const((SSD_CONV, 1536)), const((1, 1536)), const((1, 128)), const((1, 128)),
                  const((1, 1024)), const((1, 1024)), const((Q, Q)), const((256, 1024))],
        out_specs=[pl.BlockSpec((Q, 1024), lambda b, c: (row(b, c), 0)),
                   pl.BlockSpec((None, SSD_GROUPS, SSD_STATE, 512), lambda b, c: (b, 0, 0, 0))],
        out_shape=[jax.ShapeDtypeStruct((B * L, 1024), BF16),
                   jax.ShapeDtypeStruct((B, SSD_GROUPS, SSD_STATE, 512), F32)],
        scratch_shapes=[pltpu.VMEM((Q + 8, 1024), F32), pltpu.VMEM((Q + 8, 512), F32),
                        pltpu.VMEM((SSD_GROUPS, SSD_STATE, 512), F32), pltpu.VMEM((Q, 1024), F32)],
        compiler_params=_cparams("parallel", "arbitrary"),
    )(u, u, u, u, cw, cb, dtb, alog, dexp, nw, _tril_ones(Q), _twice(_head_expand(SSD_HEADS, 64)))
    st = st.reshape(B, SSD_GROUPS, SSD_STATE, SSD_HEADS // SSD_GROUPS, SSD_HEAD_DIM)
    st = jnp.transpose(st, (0, 1, 3, 4, 2)).reshape(B, SSD_HEADS, SSD_HEAD_DIM, SSD_STATE)
    return y, st


def _ssd_step_kernel(z_ref, xs_ref, bc_ref, dt_ref, cx_ref, cbc_ref, s_ref, cw_ref, cb_ref, dtb_ref,
                     alog_ref, dexp_ref, nw_ref, e2_ref, y_ref, so_ref, ybuf):
    NX = SSD_HEADS * SSD_HEAD_DIM
    cw = cw_ref[...]
    cb = cb_ref[...]
    cx = cx_ref[...]
    cbc = cbc_ref[...]
    xc = cb[:, 0:NX] + xs_ref[...] * cw[3:4, 0:NX]
    bcc = cb[:, NX:NX + 512] + bc_ref[...] * cw[3:4, NX:NX + 512]
    for j in range(SSD_CONV - 1):
        xc = xc + cx[j:j + 1, :] * cw[j:j + 1, 0:NX]
        bcc = bcc + cbc[j:j + 1, :] * cw[j:j + 1, NX:NX + 512]
    xs = _silu(xc)
    bcs = _silu(bcc)
    dt = _softplus(dt_ref[...] + dtb_ref[...])
    da = jnp.exp(dt * (-jnp.exp(alog_ref[...])))
    e2 = e2_ref[...]
    lane = lax.broadcasted_iota(jnp.int32, (64, LANES), 1)
    first = lane < 64
    HG = SSD_HEADS // SSD_GROUPS
    for q in range(SSD_HEADS // 2):
        xrow = xs[:, q * LANES:(q + 1) * LANES]
        diag = e2 * xrow
        ycols = []
        for s in range(2):
            h = 2 * q + s
            g = h // HG
            xcol = jnp.sum(jnp.where(first == (s == 0), diag, 0.0), axis=1, keepdims=True)
            b_row = bcs[:, g * SSD_STATE:(g + 1) * SSD_STATE]
            c_row = bcs[:, 256 + g * SSD_STATE:256 + (g + 1) * SSD_STATE]
            s_new = s_ref[h] * da[:, h:h + 1] + (xcol * dt[:, h:h + 1]) * b_row
            so_ref[h] = s_new
            ycols.append(jnp.sum(s_new * c_row, axis=1, keepdims=True))
        ypair = jnp.where(first, ycols[0], ycols[1])
        ybuf[:, q * LANES:(q + 1) * LANES] = jnp.sum(e2 * ypair, axis=0, keepdims=True)
    y_ref[...] = _ssd_tail(ybuf[...], xs, z_ref[...], dexp_ref[...], nw_ref[...]).astype(y_ref.dtype)


def _ssd_step(u, conv_x, conv_bc, s0, cw, cb, dtb, alog, dexp, nw):
    B = u.shape[0]
    const = lambda shape: pl.BlockSpec(shape, lambda b: (0,) * len(shape))
    return pl.pallas_call(
        _ssd_step_kernel,
        name="ssd_step",
        grid=(B,),
        in_specs=[pl.BlockSpec((None, 1, 1024), lambda b: (b, 0, U_Z // 1024)),
                  pl.BlockSpec((None, 1, 1024), lambda b: (b, 0, U_XS // 1024)),
                  pl.BlockSpec((None, 1, 512), lambda b: (b, 0, U_BC // 512)),
                  pl.BlockSpec((None, 1, 128), lambda b: (b, 0, U_DT // 128)),
                  pl.BlockSpec((None, 3, 1024), lambda b: (b, 0, 0)),
                  pl.BlockSpec((None, 3, 512), lambda b: (b, 0, 0)),
                  pl.BlockSpec((None, SSD_HEADS, 64, 128), lambda b: (b, 0, 0, 0)),
                  const((SSD_CONV, 1536)), const((1, 1536)), const((1, 128)), const((1, 128)),
                  const((1, 1024)), const((1, 1024)), const((64, 128))],
        out_specs=[pl.BlockSpec((None, 1, 1024), lambda b: (b, 0, 0)),
                   pl.BlockSpec((None, SSD_HEADS, 64, 128), lambda b: (b, 0, 0, 0))],
        out_shape=[jax.ShapeDtypeStruct((B, 1, 1024), BF16),
                   jax.ShapeDtypeStruct(s0.shape, F32)],
        scratch_shapes=[pltpu.VMEM((1, 1024), F32)],
        compiler_params=_cparams("parallel"),
    )(u, u, u, u, conv_x, conv_bc, s0, cw, cb, dtb, alog, dexp, nw, _pair_eye())


def _rwkv_prep_kernel(*refs, shifted, bpb):
    refs = list(refs)
    cur = [refs.pop(0) for _ in range(4)]
    prev = [refs.pop(0) for _ in range(4)]
    if shifted:
        first = [refs.pop(0) for _ in range(4)]
    mu = [refs.pop(0) for _ in range(4)]
    (w0_ref, w2_ref, a0_ref, a2_ref, g2_ref, kk_ref, ka_ref, bo_ref) = [refs.pop(0) for _ in range(8)]
    (r_o, w_o, k_o, v_o, kn_o, kka_o, g_o) = refs
    i = pl.program_id(0)

    def mixed(n):
        x = cur[n][...]
        if shifted:
            rolled = pltpu.roll(x, 1, 0)
            before = jnp.where(i % bpb == 0, first[n][...], prev[n][7:8, :])
            rid = lax.broadcasted_iota(jnp.int32, x.shape, 0)
            p = jnp.where(rid == 0, before, rolled)
        else:
            p = prev[n][...]
        return x + (p - x) * mu[n][...]

    r, k, v, lw = mixed(0), mixed(1), mixed(2), mixed(3)
    blk = lw[:, 0:LANES]
    lane = lax.broadcasted_iota(jnp.int32, blk.shape, 1)
    tw = jnp.where(lane < 64, jnp.tanh(blk), blk).astype(BF16)
    wpre = w0_ref[...] + _dot(tw, w2_ref[...])
    apre = a0_ref[...] + _dot(tw, a2_ref[...])
    wlog = -_softplus(-wpre) - 0.5
    a = jax.nn.sigmoid(apre)
    g = _dot(jax.nn.sigmoid(lw[:, LANES:2 * LANES]).astype(BF16), g2_ref[...])
    kk = k * kk_ref[...]
    kk2 = kk * kk
    bo = bo_ref[...]
    for q in range(RWKV_HEADS // 2):
        sl = slice(q * LANES, (q + 1) * LANES)
        n2 = _dot_exact(kk2[:, sl], bo)
        kn = kk[:, sl] / jnp.maximum(jnp.sqrt(n2), 1e-12)
        kn_o[:, sl] = -kn
        kka_o[:, sl] = kn * a[:, sl]
    r_o[...] = r
    w_o[...] = -jnp.exp(wlog)
    k_o[...] = k * (1.0 + (a - 1.0) * ka_ref[...])
    v_o[...] = v
    g_o[...] = g


def _rwkv_prep(u, prev, first, mus, ws, *, tm, bpb, shifted):
    M = u.shape[0]
    cols = [(1024, U_R // 1024), (1024, U_K // 1024), (1024, U_V // 1024), (256, U_LW // 256)]
    in_specs = [pl.BlockSpec((tm, c), functools.partial(lambda i, kb: (i, kb), kb=kb)) for c, kb in cols]
    args = [u] * 4
    if shifted:
        in_specs += [pl.BlockSpec((8, c), functools.partial(
            lambda i, kb: (jnp.maximum(i * (tm // 8) - 1, 0), kb), kb=kb)) for c, kb in cols]
        args += [u] * 4
        in_specs += [pl.BlockSpec((None, 1, c), lambda i: (i // bpb, 0, 0)) for c, _ in cols]
        args += list(first)
    else:
        in_specs += [pl.BlockSpec((tm, c), lambda i: (i, 0)) for c, _ in cols]
        args += list(prev)
    in_specs += [pl.BlockSpec((1, c), lambda i: (0, 0)) for c, _ in cols]
    args += list(mus)
    wshapes = [(1, 1024), (128, 1024), (1, 1024), (128, 1024), (128, 1024), (1, 1024), (1, 1024), (128, 128)]
    in_specs += [pl.BlockSpec(s, lambda i: (0, 0)) for s in wshapes]
    args += list(ws)
    return pl.pallas_call(
        functools.partial(_rwkv_prep_kernel, shifted=shifted, bpb=bpb),
        name="rwkv_prep",
        grid=(M // tm,),
        in_specs=in_specs,
        out_specs=[pl.BlockSpec((tm, 1024), lambda i: (i, 0))] * 7,
        out_shape=[jax.ShapeDtypeStruct((M, 1024), F32)] * 7,
        compiler_params=_cparams("parallel"),
    )(*args)


def _wkv_step_kernel(r_ref, lw_ref, k_ref, v_ref, kn_ref, ka_ref, s0_ref, e2_ref, bo_ref, o_ref, sT_ref, *, nb):
    e2 = e2_ref[...]
    bo = bo_ref[...]
    first = lax.broadcasted_iota(jnp.int32, (64, LANES), 1) < 64

    def pair_sum(x):
        sa = jnp.sum(jnp.where(first, x, 0.0), axis=1, keepdims=True)
        sb = jnp.sum(jnp.where(first, 0.0, x), axis=1, keepdims=True)
        return jnp.where(first, sa, sb)

    for b in range(nb):
        for p in range(RWKV_HEADS // 2):
            sl = slice(p * LANES, (p + 1) * LANES)
            S = s0_ref[b, p]
            sa = pair_sum(S * kn_ref[b, :, sl])
            vcol = _dot_exact(e2 * v_ref[b, :, sl], bo)
            S = S * jnp.exp(lw_ref[b, :, sl]) + sa * ka_ref[b, :, sl] + vcol * k_ref[b, :, sl]
            sT_ref[b, p] = S
            o = pair_sum(S * r_ref[b, :, sl])
            o_ref[b, :, sl] = jnp.sum(e2 * o, axis=0, keepdims=True)


def _wkv_step(r, lw, k, v, kn, ka, s0, *, nb):
    B = r.shape[0]
    s0p = s0.reshape(B, 8, 2, 64, 64).transpose(0, 1, 3, 2, 4).reshape(B, 8, 64, 128)
    seq = pl.BlockSpec((nb, 1, 1024), lambda b: (b, 0, 0))
    stt = pl.BlockSpec((nb, 8, 64, 128), lambda b: (b, 0, 0, 0))
    o, sT = pl.pallas_call(
        functools.partial(_wkv_step_kernel, nb=nb),
        grid=(B // nb,),
        in_specs=[seq] * 6 + [stt, pl.BlockSpec((64, 128), lambda b: (0, 0)),
                              pl.BlockSpec((128, 128), lambda b: (0, 0))],
        out_specs=[seq, stt],
        out_shape=[jax.ShapeDtypeStruct((B, 1, 1024), F32), jax.ShapeDtypeStruct((B, 8, 64, 128), F32)],
        compiler_params=_cparams("parallel"),
        name="wkv_step",
    )(r, lw, k, v, kn, ka, s0p, _pair_eye(), _block_ones())
    sT = sT.reshape(B, 8, 64, 2, 64).transpose(0, 1, 3, 2, 4).reshape(B, 16, 64, 64)
    return o, sT


WKV_CHUNK = 64


def _wkv_chunk_kernel(r_ref, lw_ref, k_ref, v_ref, kn_ref, ka_ref, tril_ref, o_ref, sT_ref, S_ref, *, nb):
    C = WKV_CHUNK
    c = pl.program_id(1)

    @pl.when(c == 0)
    def _():
        S_ref[...] = jnp.zeros_like(S_ref)

    tril = tril_ref[...]
    lane = lax.broadcasted_iota(jnp.int32, (C, LANES), 1)
    rowi = lax.broadcasted_iota(jnp.int32, (C, LANES), 0)
    first = lane < 64
    strict = rowi > (lane % 64)
    incl = rowi >= (lane % 64)
    r128 = lax.broadcasted_iota(jnp.int32, (LANES, LANES), 0)
    c128 = lax.broadcasted_iota(jnp.int32, (LANES, LANES), 1)
    diag_blocks = (r128 < 64) == (c128 < 64)
    eye = r128 == c128

    def bd(x):
        return jnp.concatenate([jnp.where(first, x, 0.0), jnp.where(first, 0.0, x)], axis=0)

    bf = lambda x: x.astype(BF16)
    pairs = [(b, p) for b in range(nb) for p in range(RWKV_HEADS // 2)]
    sls = [slice(p * LANES, (p + 1) * LANES) for _, p in pairs]
    load = lambda ref: [ref[b, :, sl] for (b, _), sl in zip(pairs, sls)]
    each = lambda f, *ls: [f(*a) for a in zip(*ls)]
    r_, lw, kt, vv, al, be = (load(ref) for ref in (r_ref, lw_ref, k_ref, v_ref, kn_ref, ka_ref))
    def cumulative(x):
        hi = x.astype(BF16)
        lo = (x - hi.astype(F32)).astype(BF16)
        return _dot(tril, jnp.concatenate([hi, lo], axis=0))

    cs = each(cumulative, lw)
    last = each(lambda x: x[C - 1:C, :], cs)
    e_inv = each(lambda x: jnp.exp(-x), cs)
    aq = each(lambda a, x, l: a * jnp.exp(x - l), al, cs, lw)
    rq = each(lambda a, x: a * jnp.exp(x), r_, cs)
    bk = each(jnp.multiply, be, e_inv)
    kk = each(jnp.multiply, kt, e_inv)
    g = each(lambda a, q, b_, k_: _dot_exact_nt(
        jnp.concatenate([a, q], axis=0),
        jnp.concatenate([jnp.where(first, b_, 0.0), jnp.where(first, 0.0, b_),
                         jnp.where(first, k_, 0.0), jnp.where(first, 0.0, k_)], axis=0)), aq, rq, bk, kk)
    m1 = each(lambda x: bf(jnp.where(strict, x[0:C, 0:LANES], 0.0)), g)
    m2 = each(lambda x: bf(jnp.where(strict, x[0:C, LANES:2 * LANES], 0.0)), g)
    n1 = each(lambda x: bf(jnp.where(incl, x[C:2 * C, 0:LANES], 0.0)), g)
    n2 = each(lambda x: bf(jnp.where(incl, x[C:2 * C, LANES:2 * LANES], 0.0)), g)
    s0 = [S_ref[b, p] for b, p in pairs]
    s0b = each(bf, s0)
    vbd = each(lambda x: bf(bd(x)), vv)
    x = each(lambda a, s, m, v_: _dot(bf(a), s) + _dot(m, v_), aq, s0b, m2, vbd)
    mp = m1
    steps = int(math.log2(C))
    for i in range(steps):
        x = each(lambda x_, m: x_ + _dot(m, bf(bd(x_))), x, mp)
        if i + 1 < steps:
            mp = each(lambda m: bf(_dot(m, bd(m))), mp)
    o = each(lambda q, s, a, x_, b_, v_: _dot(bf(q), s) + _dot(a, bf(bd(x_))) + _dot(b_, v_),
             rq, s0b, n1, x, n2, vbd)
    for (b, _), sl, o_ in zip(pairs, sls, o):
        o_ref[b, :, sl] = o_
    e_end = each(lambda l, x_: jnp.exp(l - x_), last, cs)
    kv_t = each(lambda b_, k_, e: jnp.concatenate([b_ * e, k_ * e], axis=0).T, be, kt, e_end)
    upd = each(lambda t, x_, v_: _dot(bf(t), bf(jnp.concatenate([x_, v_], axis=0))), kv_t, x, vv)
    gcol = each(lambda l: jnp.sum(jnp.where(eye, jnp.exp(l), 0.0), axis=1, keepdims=True), last)
    for (b, p), s, u, gc in zip(pairs, s0, upd, gcol):
        S_ref[b, p] = jnp.where(diag_blocks, gc * s + u, 0.0)

    @pl.when(c == pl.num_programs(1) - 1)
    def _():
        sT_ref[...] = S_ref[...]


def _wkv_chunked(r, lw, k, v, kn, ka, *, nb):
    B, L, _ = r.shape
    C = WKV_CHUNK
    seq = pl.BlockSpec((nb, C, 1024), lambda b, c: (b, c, 0))
    stt = pl.BlockSpec((nb, 8, LANES, LANES), lambda b, c: (b, 0, 0, 0))
    o, sT = pl.pallas_call(
        functools.partial(_wkv_chunk_kernel, nb=nb),
        grid=(B // nb, L // C),
        in_specs=[seq] * 6 + [pl.BlockSpec((C, 2 * C), lambda b, c: (0, 0))],
        out_specs=[seq, stt],
        out_shape=[jax.ShapeDtypeStruct((B, L, 1024), F32), jax.ShapeDtypeStruct((B, 8, LANES, LANES), F32)],
        scratch_shapes=[pltpu.VMEM((nb, 8, LANES, LANES), F32)],
        compiler_params=_cparams("parallel", "arbitrary"),
        name="wkv_chunked",
    )(r, lw, k, v, kn, ka, jnp.concatenate([_tril_ones(C)] * 2, axis=1).astype(BF16))
    blocks = jnp.stack([sT[:, :, 0:64, 0:64], sT[:, :, 64:128, 64:128]], axis=2)
    return o, jnp.swapaxes(blocks, -1, -2).reshape(B, 16, 64, 64)


def _rwkv_post_kernel(o_ref, r_ref, k_ref, v_ref, g_ref, lnw_ref, lnb_ref, rk_ref, bo_ref, y_ref):
    bo = bo_ref[...]
    inv = 1.0 / RWKV_HEAD_DIM
    for q in range(RWKV_HEADS // 2):
        sl = slice(q * LANES, (q + 1) * LANES)
        o = o_ref[:, sl]
        mean = _dot_split(o, bo) * inv
        d = o - mean
        var = _dot_split(d * d, bo) * inv
        on = d * lax.rsqrt(var + RWKV_LN_EPS) * lnw_ref[:, sl] + lnb_ref[:, sl]
        bonus = _dot_split(r_ref[:, sl] * k_ref[:, sl] * rk_ref[:, sl], bo) * v_ref[:, sl]
        y_ref[:, sl] = ((on + bonus) * g_ref[:, sl]).astype(y_ref.dtype)


def _rwkv_post(o, r, k, v, g, lnw, lnb, rk, tm):
    M = o.shape[0]
    blk = pl.BlockSpec((tm, 1024), lambda i: (i, 0))
    vec = pl.BlockSpec((1, 1024), lambda i: (0, 0))
    return pl.pallas_call(
        _rwkv_post_kernel,
        name="rwkv_post",
        grid=(M // tm,),
        in_specs=[blk] * 5 + [vec] * 3 + [pl.BlockSpec((256, 128), lambda i: (0, 0))],
        out_specs=blk,
        out_shape=jax.ShapeDtypeStruct((M, 1024), BF16),
        compiler_params=_cparams("parallel"),
    )(o, r, k, v, g, lnw, lnb, rk, _twice(_block_ones()))


def _rope_tables(pos):
    half = ROPE_DIM // 2
    inv = ROPE_THETA ** (-jnp.arange(half, dtype=F32) * 2.0 / ROPE_DIM)
    ang = pos.astype(F32)[:, None] * inv
    cos, sin = jnp.cos(ang), jnp.sin(ang)
    n = pos.shape[0]
    rest = ATT_HEAD_DIM - ROPE_DIM
    c = jnp.concatenate([cos, cos, jnp.ones((n, rest), F32)], axis=1)
    s_next = jnp.concatenate([-sin, jnp.zeros((n, half + rest), F32)], axis=1)
    s_prev = jnp.concatenate([jnp.zeros((n, half), F32), sin, jnp.zeros((n, rest), F32)], axis=1)
    return tuple(jnp.concatenate([t, t], axis=1) for t in (c, s_next, s_prev))


def _rope_apply(x, c, s_next, s_prev):
    n = x.shape[1]
    reps = n // LANES
    tile = lambda t: jnp.concatenate([t] * reps, axis=1)
    half = ROPE_DIM // 2
    return x * tile(c) + pltpu.roll(x, n - half, 1) * tile(s_next) + pltpu.roll(x, half, 1) * tile(s_prev)


ATT_BLOCK_ROWS = ATT_Q_BLOCK * max(d for _, d in ATT_GROUPS)
ATT_BATCH = 4


def _attn_prompt_kernel(q_ref, kc_ref, kp_ref, vc_ref, vp_ref, o_ref, lse_ref, *, d):
    i = pl.program_id(0)
    QB = ATT_Q_BLOCK
    R = q_ref.shape[0]
    row = lax.broadcasted_iota(jnp.int32, (QB, 2 * QB), 0)
    col = lax.broadcasted_iota(jnp.int32, (QB, 2 * QB), 1)
    band = (col >= row) & (col <= row + QB)
    band_first = band & ((i > 0) | (col >= QB))
    first = lax.broadcasted_iota(jnp.int32, (QB, LANES), 1) < ATT_HEAD_DIM
    first_kv = lax.broadcasted_iota(jnp.int32, (2 * QB, LANES), 1) < ATT_HEAD_DIM
    scale = ATT_HEAD_DIM ** -0.5
    span = QB * d
    blocks = [(rho, j) for rho in range(d) for j in range(R // span)]
    each = lambda f, *ls: [f(*a) for a in zip(*ls)]
    for b0 in range(0, len(blocks), ATT_BATCH):
        batch = blocks[b0:b0 + ATT_BATCH]
        ds = lambda start, n: pl.ds(start, n, stride=d) if d > 1 else pl.ds(start, n)
        qrows = [ds(rho + span * j, QB) for rho, j in batch]
        valid = [band_first if j == 0 else band for _, j in batch]

        def keys(cur_ref, prev_ref, rho, j):
            if j > 0:
                return cur_ref[ds(rho + span * (j - 1), 2 * QB), :]
            return jnp.concatenate([prev_ref[ds(R - span + rho, QB), :], cur_ref[ds(rho, QB), :]], axis=0)

        qb = [q_ref[r, :].astype(BF16) for r in qrows]
        k2 = [keys(kc_ref, kp_ref, rho, j).astype(BF16) for rho, j in batch]
        v2 = [keys(vc_ref, vp_ref, rho, j).astype(BF16) for rho, j in batch]
        halves = [slice(0, ATT_HEAD_DIM), slice(ATT_HEAD_DIM, LANES)]
        s = [[jnp.where(vm, _dot_nt(q[:, sl], k[:, sl]) * scale, -jnp.inf) for sl in halves]
             for q, k, vm in zip(qb, k2, valid)]
        m = [[jnp.max(jnp.maximum(x[:, 0:QB], x[:, QB:2 * QB]), axis=1, keepdims=True) for x in pair] for pair in s]
        p = [[jnp.exp(x - mx).astype(BF16) for x, mx in zip(ps, ms)] for ps, ms in zip(s, m)]
        va = each(lambda v: jnp.where(first_kv, v, 1.0), v2)
        vb = each(lambda v: jnp.where(first_kv, 1.0, v), v2)
        ea = each(lambda pp, v: _dot(pp[0], v), p, va)
        eb = each(lambda pp, v: _dot(pp[1], v), p, vb)
        num = each(lambda a, b_: jnp.where(first, a, b_), ea, eb)
        den = each(lambda a, b_: pltpu.roll(jnp.where(first, b_, a), ATT_HEAD_DIM, 1), ea, eb)
        for r, n_, d_, mm in zip(qrows, num, den, m):
            o_ref[r, :] = n_ / d_
            lse_ref[r, :] = jnp.where(first, mm[0], mm[1]) + jnp.log(d_)


def _attn_prompt(qkv, B, L, gi):
    window, d = ATT_GROUPS[gi]
    R = min(ATT_BLOCK_ROWS, L)
    assert window == ATT_Q_BLOCK * d and R % (ATT_Q_BLOCK * d) == 0 and L % R == 0
    nblk = L // R
    npair = ATT_DIM // LANES
    prev = lambda i: jnp.maximum(i - 1, 0)

    def col(which, f):
        return lambda i, b, hp: (b * nblk + f(i), gi * 3 * npair + which * npair + hp)

    same = lambda i: i
    blk = lambda f: pl.BlockSpec((R, LANES), f)
    out = pl.BlockSpec((R, LANES), lambda i, b, hp: (b * nblk + i, hp))
    return pl.pallas_call(
        functools.partial(_attn_prompt_kernel, d=d),
        grid=(nblk, B, npair),
        in_specs=[blk(col(0, same)), blk(col(1, same)), blk(col(1, prev)), blk(col(2, same)), blk(col(2, prev))],
        out_specs=[out, out],
        out_shape=[jax.ShapeDtypeStruct((B * L, ATT_DIM), F32)] * 2,
        compiler_params=_cparams("arbitrary", "arbitrary", "arbitrary"),
        name="attn_prompt_d%d" % d,
    )(qkv, qkv, qkv, qkv, qkv)


def _attn_out_kernel(o0, l0, o1, l1, o2, l2, w_ref, res_ref, gate_ref, out_ref, h_ref):
    @pl.when(pl.program_id(1) == 0)
    def _():
        m = jnp.maximum(jnp.maximum(l0[...], l1[...]), l2[...])
        a0, a1, a2 = jnp.exp(l0[...] - m), jnp.exp(l1[...] - m), jnp.exp(l2[...] - m)
        o = (a0 * o0[...] + a1 * o1[...] + a2 * o2[...]) / (a0 + a1 + a2)
        h_ref[...] = o.astype(BF16)

    out_ref[...] = res_ref[...] + gate_ref[...] * _dot(h_ref[...], w_ref[...])


def _attn_out(ols, w, res, mod, kg, *, tm, tn, bpb):
    M = res.shape[0]
    N = w.shape[1]
    r = mod.shape[1]
    nj = N // tn
    part = pl.BlockSpec((tm, 512), lambda i, j: (i, 0))
    return pl.pallas_call(
        _attn_out_kernel,
        name="attn_out",
        grid=(M // tm, nj),
        in_specs=[part] * 6 + [pl.BlockSpec((512, tn), lambda i, j: (0, j)),
                               pl.BlockSpec((tm, tn), lambda i, j: (i, j)),
                               pl.BlockSpec((None, r, tn), lambda i, j: (i // bpb, 0, kg * nj + j))],
        out_specs=pl.BlockSpec((tm, tn), lambda i, j: (i, j)),
        out_shape=jax.ShapeDtypeStruct((M, N), F32),
        scratch_shapes=[pltpu.VMEM((tm, 512), BF16)],
        compiler_params=_cparams("parallel", "arbitrary"),
    )(*ols, w, res, mod)


def _attn_step_kernel(q_ref, kn_ref, vn_ref, c0_ref, c1_ref, c2_ref, o_ref):
    caches = (c0_ref, c1_ref, c2_ref)
    scale = ATT_HEAD_DIM ** -0.5
    hl = lax.broadcasted_iota(jnp.int32, (ATT_HEAD_DIM, ATT_HEADS), 1)
    ng = len(ATT_GROUPS)
    combos = [(h, gi) for h in range(ATT_HEADS) for gi in range(ng)]
    each = lambda f, *ls: [f(*a) for a in zip(*ls)]
    col = lambda ref: [ref[gi][:, h:h + 1] for h, gi in combos]
    q, kn, vn = col(q_ref), col(kn_ref), col(vn_ref)
    vis = [lax.broadcasted_iota(jnp.int32, (1, w), 1) % d == 0 for w, d in ATT_GROUPS]
    s = [jnp.where(vis[gi], jnp.sum(caches[gi][0, h] * q_, axis=0, keepdims=True) * scale, -jnp.inf)
         for (h, gi), q_ in zip(combos, q)]
    s_new = each(lambda a, b: jnp.sum(a * b, axis=0, keepdims=True) * scale, q, kn)
    m = each(lambda a, b: jnp.maximum(jnp.max(a, axis=1, keepdims=True), b), s, s_new)
    p = each(lambda a, b: jnp.exp(a - b), s, m)
    p_new = each(lambda a, b: jnp.exp(a - b), s_new, m)
    l = each(lambda a, b: jnp.sum(a, axis=1, keepdims=True) + b, p, p_new)
    acc = [jnp.sum(caches[gi][1, h] * p_, axis=1, keepdims=True) + pn * v_
           for (h, gi), p_, pn, v_ in zip(combos, p, p_new, vn)]
    og = each(lambda a, b: a / b, acc, l)
    lse = each(lambda a, b: a + jnp.log(b), m, l)
    out = jnp.zeros((ATT_HEAD_DIM, ATT_HEADS), F32)
    for h in range(ATT_HEADS):
        os_, ls_ = og[h * ng:(h + 1) * ng], lse[h * ng:(h + 1) * ng]
        mm = jnp.maximum(jnp.maximum(ls_[0], ls_[1]), ls_[2])
        ws = [jnp.exp(x - mm) for x in ls_]
        o = (ws[0] * os_[0] + ws[1] * os_[1] + ws[2] * os_[2]) / (ws[0] + ws[1] + ws[2])
        out = jnp.where(hl == h, o, out)
    o_ref[...] = out


def _window_minor(c):
    return jnp.transpose(c, (0, 1, 3, 4, 5, 2))


def _attn_step(q3, kn3, vn3, caches_t, layer):
    B = q3.shape[0]
    for (window, d), c in zip(ATT_GROUPS, caches_t):
        assert c.shape[-1] == window and window // d == 128
    cols = lambda t: jnp.swapaxes(t, -1, -2)
    specs = [pl.BlockSpec((None, None) + c.shape[2:], lambda b: (layer, b, 0, 0, 0, 0)) for c in caches_t]
    new = pl.BlockSpec((None, len(ATT_GROUPS), ATT_HEAD_DIM, ATT_HEADS), lambda b: (b, 0, 0, 0))
    o = pl.pallas_call(
        _attn_step_kernel,
        grid=(B,),
        in_specs=[new] * 3 + specs,
        out_specs=pl.BlockSpec((None, ATT_HEAD_DIM, ATT_HEADS), lambda b: (b, 0, 0)),
        out_shape=jax.ShapeDtypeStruct((B, ATT_HEAD_DIM, ATT_HEADS), F32),
        compiler_params=_cparams("parallel"),
        name="attn_step",
    )(cols(q3), cols(kn3), cols(vn3), *caches_t)
    return jnp.swapaxes(o, -1, -2)


def _roll_kernel(c0, c1, c2, n0, n1, n2, o0, o1, o2):
    for c_ref, n_ref, o_ref in ((c0, n0, o0), (c1, n1, o1), (c2, n2, o2)):
        w = c_ref.shape[-1]
        rows = c_ref.shape[0] * c_ref.shape[1] * c_ref.shape[2]
        x = c_ref[...].reshape(rows, w)
        lane = lax.broadcasted_iota(jnp.int32, (rows, w), 1)
        y = jnp.where(lane == w - 1, n_ref[...].reshape(rows, 1), pltpu.roll(x, w - 1, 1))
        o_ref[...] = y.reshape(o_ref.shape)


def _roll_windows(caches_t, rows_t):
    NC, B = caches_t[0].shape[:2]
    spec = lambda a: pl.BlockSpec((None, None) + a.shape[2:], lambda i, b: (i, b, 0, 0, 0, 0))
    return pl.pallas_call(
        _roll_kernel,
        grid=(NC, B),
        in_specs=[spec(c) for c in caches_t] + [spec(r) for r in rows_t],
        out_specs=[spec(c) for c in caches_t],
        out_shape=[jax.ShapeDtypeStruct(c.shape, c.dtype) for c in caches_t],
        compiler_params=_cparams("parallel", "parallel"),
        name="roll_windows",
    )(*caches_t, *rows_t)


def _hyb_params(i, hyb_w_in, hyb_w_out, ssd_conv_w, ssd_conv_b, ssd_dt_bias, ssd_a_log, ssd_d, ssd_norm_w,
                rwkv_mu, rwkv_w0, rwkv_w2, rwkv_a0, rwkv_a2, rwkv_g2, rwkv_k_k, rwkv_k_a, rwkv_r_k,
                rwkv_ln_w, rwkv_ln_b):
    w = hyb_w_in[i]
    rw0 = 2576
    w_perm = jnp.concatenate(
        [w[:, 0:1024], w[:, 1024:2048], w[:, rw0:rw0 + 3072], w[:, 2048:2560], w[:, rw0 + 3072:rw0 + 3328],
         w[:, 2560:2576], jnp.zeros((D_MODEL, U_COLS - U_DT - 16), F32)], axis=1).astype(BF16)
    pad128 = lambda v: jnp.concatenate([v, jnp.zeros((LANES - v.shape[0],), F32)])[None, :]
    z64 = jnp.zeros((64, 1024), F32)
    mu = rwkv_mu[i]
    p = dict(
        w_in=w_perm, w_out=hyb_w_out[i].astype(BF16),
        cw=ssd_conv_w[i], cb=ssd_conv_b[i][None, :], dtb=pad128(ssd_dt_bias[i]), alog=pad128(ssd_a_log[i]),
        dexp=jnp.repeat(ssd_d[i], SSD_HEAD_DIM)[None, :], nw=ssd_norm_w[i][None, :],
        mus=[mu[None, 0:1024], mu[None, 1024:2048], mu[None, 2048:3072], mu[None, 3072:3328]],
        prep_w=[rwkv_w0[i][None, :], jnp.concatenate([rwkv_w2[i], z64]).astype(BF16),
                rwkv_a0[i][None, :], jnp.concatenate([z64, rwkv_a2[i]]).astype(BF16),
                rwkv_g2[i].astype(BF16), rwkv_k_k[i][None, :], rwkv_k_a[i][None, :], _block_ones()],
        lnw=rwkv_ln_w[i][None, :], lnb=rwkv_ln_b[i][None, :], rk=rwkv_r_k[i].reshape(1, 1024),
    )
    return p


def _raw_conv_rows(u_rows):
    return jnp.concatenate([u_rows[..., U_XS:U_XS + 1024], u_rows[..., U_BC:U_BC + 512]], axis=-1)


def _raw_rw_rows(u_rows):
    return jnp.concatenate([u_rows[..., U_R:U_R + 3072], u_rows[..., U_LW:U_LW + 256]], axis=-1)


def _run_prompt(x, mods, P, hyb, att, norm_final, B, L):
    T = B * L
    big, half = PROMPT_ROWS, PROMPT_ROWS // 2
    lin = lambda *a, tm, **kw: _linear(*a, tm=tm, bpb=L // tm, **kw)
    new = dict(ssd=[], conv=[], wkv=[], shift=[], win=[[], [], []])
    tabs = _rope_tables(jnp.arange(L))
    for l in range(DEPTH):
        mod = mods[l]
        i = l // 2
        gmix = P['norm_mix'][l][None, :]
        if l % 2 == 0:
            hp = hyb[i]
            u = lin(x, hp['w_in'], tm=big, tn=1024, pro='normmod', norm=(gmix, mod, 1, 0))
            y_ssd, s_ssd = _ssd_prompt(u, B, L, hp['cw'], hp['cb'], hp['dtb'], hp['alog'], hp['dexp'], hp['nw'])
            zeros = [jnp.zeros((B, 1, c), F32) for c in (1024, 1024, 1024, 256)]
            r, w, k, v, kn, ka, g = _rwkv_prep(u, None, zeros, hp['mus'], hp['prep_w'], tm=256, bpb=L // 256,
                                               shifted=True)
            sh = lambda t: t.reshape(B, L, 1024)
            o, s_wkv = _wkv_chunked(sh(r), sh(w), sh(k), sh(v), sh(kn), sh(ka), nb=B)
            y_rwkv = _rwkv_post(o.reshape(T, 1024), r, k, v, g, hp['lnw'], hp['lnb'], hp['rk'], 256)
            x = lin(y_ssd, hp['w_out'][:1024], tm=big, tn=1024, epi='resgate', res=x, gate=(mod, 2),
                    second=(y_rwkv, hp['w_out'][1024:]))
            u3 = u.reshape(B, L, U_COLS)
            new['ssd'].append(s_ssd)
            new['conv'].append(_raw_conv_rows(u3[:, L - (SSD_CONV - 1):]))
            new['wkv'].append(s_wkv)
            new['shift'].append(_raw_rw_rows(u3[:, L - 1]))
        else:
            ap = att[i]
            qkv = lin(x, ap['w_qkv'], tm=big, tn=3 * ATT_DIM, pro='normmod', epi='rope',
                      norm=(gmix, mod, 1, 0), rope=tabs)
            ols = []
            q3 = qkv.reshape(B, L, len(ATT_GROUPS) * 3 * ATT_DIM)
            for gi, (window, d) in enumerate(ATT_GROUPS):
                ols += _attn_prompt(qkv, B, L, gi)
                keep = min(window, L)
                k0 = (gi * 3 + 1) * ATT_DIM
                kv = q3[:, L - keep:, k0:k0 + 2 * ATT_DIM]
                new['win'][gi].append(kv.reshape(B, keep, 2, ATT_HEADS, ATT_HEAD_DIM))
            x = _attn_out(ols, ap['w_out'], x, mod, 2, tm=half, tn=1024, bpb=L // half)
        gmlp = P['norm_mlp'][l][None, :]
        hid = lin(x, P['w1'][l], tm=big, tn=1024, pro='normmod', epi='relu2', norm=(gmlp, mod, 4, 3),
                  out_dtype=BF16)
        x = lin(hid, P['w2'][l], tm=half, tn=1024, epi='resgate', res=x, gate=(mod, 5))
    y = _rmsnorm(x, norm_final[None, :], half).reshape(B, L, D_MODEL)
    return y, new


def _run_sample(x, mods, P, hyb, att, norm_final, states, B):
    state_ssd, state_conv, state_wkv, state_shift, caches = states
    tm = B
    new = dict(ssd=[], conv=[], wkv=[], shift=[], win=[[], [], []])
    tabs = _rope_tables(jnp.full((1,), PAST_LEN, jnp.int32))
    caches_t = [_window_minor(c) for c in caches]
    for l in range(DEPTH):
        mod = mods[l]
        i = l // 2
        gmix = P['norm_mix'][l][None, :]
        if l % 2 == 0:
            hp = hyb[i]
            u = _linear(x, hp['w_in'], tm=tm, tn=512, pro='normmod', norm=(gmix, mod, 1, 0))
            u3 = u.reshape(B, 1, U_COLS)
            cbuf = state_conv[i]
            y_ssd, s_ssd = _ssd_step(u3, cbuf[:, :, 0:1024], cbuf[:, :, 1024:1536], state_ssd[i],
                                     hp['cw'], hp['cb'], hp['dtb'], hp['alog'], hp['dexp'], hp['nw'])
            sb = state_shift[i]
            prev = [sb[:, 0:1024], sb[:, 1024:2048], sb[:, 2048:3072], sb[:, 3072:3328]]
            r, w, k, v, kn, ka, g = _rwkv_prep(u, prev, None, hp['mus'], hp['prep_w'], tm=tm, bpb=1,
                                               shifted=False)
            sh = lambda t: t.reshape(B, 1, 1024)
            o, s_wkv = _wkv_step(sh(r), sh(w), sh(k), sh(v), sh(kn), sh(ka), state_wkv[i], nb=2)
            y_rwkv = _rwkv_post(o.reshape(B, 1024), r, k, v, g, hp['lnw'], hp['lnb'], hp['rk'], tm)
            x = _linear(y_ssd.reshape(B, 1024), hp['w_out'][:1024], tm=tm, tn=512, epi='resgate', res=x,
                        gate=(mod, 2), second=(y_rwkv, hp['w_out'][1024:]))
            new['ssd'].append(s_ssd)
            new['conv'].append(jnp.concatenate([cbuf[:, 1:], _raw_conv_rows(u3)], axis=1))
            new['wkv'].append(s_wkv)
            new['shift'].append(_raw_rw_rows(u))
        else:
            ap = att[i]
            qkv = _linear(x, ap['w_qkv'], tm=tm, tn=3 * ATT_DIM, pro='normmod', epi='rope',
                          norm=(gmix, mod, 1, 0), rope=tabs)
            parts = qkv.reshape(B, len(ATT_GROUPS), 3, ATT_HEADS, ATT_HEAD_DIM)
            o = _attn_step(parts[:, :, 0], parts[:, :, 1], parts[:, :, 2], caches_t, i)
            x = _linear(o.reshape(B, ATT_DIM), ap['w_out'], tm=tm, tn=512, epi='resgate', res=x, gate=(mod, 2))
            for gi in range(len(ATT_GROUPS)):
                new['win'][gi].append(parts[:, gi, 1:3][:, None])
        gmlp = P['norm_mlp'][l][None, :]
        hid = _linear(x, P['w1'][l], tm=tm, tn=512, pro='normmod', epi='relu2', norm=(gmlp, mod, 4, 3),
                      out_dtype=BF16)
        x = _linear(hid, P['w2'][l], tm=tm, tn=512, epi='resgate', res=x, gate=(mod, 5))
    y = _rmsnorm(x, norm_final[None, :], tm).reshape(B, 1, D_MODEL)
    return y, new


def kernel(x_prompt, x_sample, state_ssd, state_ssd_conv, state_wkv, state_wkv_shift, cache_win0, cache_win1, cache_win2, c_prompt, c_sample, norm_mix, norm_mlp, norm_final, ada_w, ada_b, mlp_w1, mlp_w2, hyb_w_in, hyb_w_out, ssd_conv_w, ssd_conv_b, ssd_dt_bias, ssd_a_log, ssd_d, ssd_norm_w, rwkv_mu, rwkv_w0, rwkv_w2, rwkv_a0, rwkv_a2, rwkv_g2, rwkv_k_k, rwkv_k_a, rwkv_r_k, rwkv_ln_w, rwkv_ln_b, att_w_qkv, att_w_out):
    Bp, L, _ = x_prompt.shape
    Bs = x_sample.shape[0]
    assert x_sample.shape[1] == 1

    nrow = Bp + Bs
    npad = -nrow % 16
    c_all = jnp.concatenate([c_prompt, c_sample, jnp.zeros((npad, D_MODEL), F32)], axis=0)
    mods_p, mods_s = [], []
    for l in range(DEPTH):
        mod = _linear(c_all, ada_w, w_layer=l, tm=nrow + npad, tn=512, pro='silu', epi='bias',
                      bias=ada_b[l][None, :])
        mods_p.append(mod[:Bp].reshape(Bp, 1, N_MOD * D_MODEL))
        mods_s.append(mod[Bp:nrow].reshape(1, Bs, N_MOD * D_MODEL))

    P = dict(norm_mix=norm_mix, norm_mlp=norm_mlp,
             w1=[mlp_w1[l].astype(BF16) for l in range(DEPTH)],
             w2=[mlp_w2[l].astype(BF16) for l in range(DEPTH)])
    hyb = [_hyb_params(i, hyb_w_in, hyb_w_out, ssd_conv_w, ssd_conv_b, ssd_dt_bias, ssd_a_log, ssd_d,
                       ssd_norm_w, rwkv_mu, rwkv_w0, rwkv_w2, rwkv_a0, rwkv_a2, rwkv_g2, rwkv_k_k,
                       rwkv_k_a, rwkv_r_k, rwkv_ln_w, rwkv_ln_b) for i in range(hyb_w_in.shape[0])]
    att = [dict(w_qkv=att_w_qkv[i].astype(BF16), w_out=att_w_out[i].astype(BF16))
           for i in range(att_w_qkv.shape[0])]

    y_p, new_p = _run_prompt(x_prompt.reshape(Bp * L, D_MODEL), mods_p, P, hyb, att, norm_final, Bp, L)
    y_s, new_s = _run_sample(x_sample.reshape(Bs, D_MODEL), mods_s, P, hyb, att, norm_final,
                             (state_ssd, state_ssd_conv, state_wkv, state_wkv_shift,
                              (cache_win0, cache_win1, cache_win2)), Bs)
    st = jnp.stack
    caches = (cache_win0, cache_win1, cache_win2)
    rolled = _roll_windows([_window_minor(c) for c in caches],
                           [_window_minor(st(new_s['win'][g])) for g in range(len(caches))])
    win_s = [jnp.transpose(t, (0, 1, 5, 2, 3, 4)) for t in rolled]
    return (y_p, y_s, st(new_p['ssd']), st(new_s['ssd']), st(new_p['conv']), st(new_s['conv']),
            st(new_p['wkv']), st(new_s['wkv']), st(new_p['shift']), st(new_s['shift']),
            st(new_p['win'][0]), win_s[0], st(new_p['win'][1]), win_s[1],
            st(new_p['win'][2]), win_s[2])
```

```python
import functools
import math

import numpy as np
import jax
import jax.numpy as jnp
from jax import lax
from jax.experimental import pallas as pl
from jax.experimental.pallas import tpu as pltpu

F32 = jnp.float32
BF16 = jnp.bfloat16
HIGHEST = lax.Precision.HIGHEST

D_MODEL = 1024
DEPTH = 4
PAST_LEN = 8192
NORM_EPS = 1e-6
N_MOD = 6
SSD_HEADS = 16
SSD_HEAD_DIM = 64
SSD_GROUPS = 2
SSD_STATE = 128
SSD_CONV = 4
SSD_CHUNK = 128
RWKV_HEADS = 16
RWKV_HEAD_DIM = 64
RWKV_LN_EPS = 64e-5
ATT_GROUPS = ((128, 1), (512, 4), (2048, 16))
ATT_HEADS = 8
ATT_HEAD_DIM = 64
ATT_DIM = ATT_HEADS * ATT_HEAD_DIM
ATT_Q_BLOCK = 128
ROPE_THETA = 500000.0
ROPE_DIM = ATT_HEAD_DIM // 4
MLP_HIDDEN = 4 * D_MODEL

U_COLS = 6144
U_Z, U_XS, U_R, U_K, U_V, U_BC, U_LW, U_DT = 0, 1024, 2048, 3072, 4096, 5120, 5632, 5888

LANES = 128
VMEM_LIMIT = 48 * 1024 * 1024
PROMPT_ROWS = 1024


def _cparams(*sem):
    return pltpu.CompilerParams(dimension_semantics=sem, vmem_limit_bytes=VMEM_LIMIT)


def _dot(a, b):
    return jnp.dot(a, b, preferred_element_type=F32)


def _dot_exact(a, b):
    return jnp.dot(a, b, preferred_element_type=F32, precision=HIGHEST)


def _dot_split(x, w2):
    hi = x.astype(BF16)
    lo = (x - hi.astype(F32)).astype(BF16)
    return _dot(jnp.concatenate([hi, lo], axis=1), w2)


def _twice(w):
    return jnp.concatenate([w, w], axis=0).astype(BF16)


def _dot_exact_nt(a, b):
    return lax.dot_general(a, b, (((1,), (1,)), ((), ())), preferred_element_type=F32, precision=HIGHEST)


def _dot_nt(a, b):
    return lax.dot_general(a, b, (((1,), (1,)), ((), ())), preferred_element_type=F32)


def _silu(x):
    return x * jax.nn.sigmoid(x)


def _softplus(x):
    return jnp.maximum(x, 0.0) + jnp.log1p(jnp.exp(-jnp.abs(x)))


def _block_ones():
    i = np.arange(LANES)
    return jnp.asarray((i[:, None] // 64 == i[None, :] // 64).astype(np.float32))


def _pair_eye():
    i = np.arange(64)
    j = np.arange(LANES)
    return jnp.asarray((i[:, None] == (j[None, :] % 64)).astype(np.float32))


def _head_expand(nheads, width):
    e = np.zeros((LANES, nheads * width), np.float32)
    for h in range(nheads):
        e[h, h * width:(h + 1) * width] = 1.0
    return jnp.asarray(e)


def _tril_ones(n):
    return jnp.asarray(np.tril(np.ones((n, n), np.float32)))


def _linear_kernel(*refs, pro, epi, two):
    refs = list(refs)
    x_ref = refs.pop(0)
    if pro == 'normmod':
        g_ref, sc_ref, sh_ref = refs.pop(0), refs.pop(0), refs.pop(0)
    w_ref = refs.pop(0)
    if two:
        x2_ref, w2_ref = refs.pop(0), refs.pop(0)
    if epi == 'bias':
        b_ref = refs.pop(0)
    if epi in ('resgate', 'resgate_norm'):
        res_ref, gate_ref = refs.pop(0), refs.pop(0)
    if epi == 'resgate_norm':
        gf_ref = refs.pop(0)
    if epi == 'rope':
        tab_refs = [refs.pop(0) for _ in range(3)]
    o_ref = refs.pop(0)

    if pro == 'cast':
        h = x_ref[...].astype(BF16)
    else:
        h_ref = refs.pop(0)

        @pl.when(pl.program_id(1) == 0)
        def _():
            x = x_ref[...].astype(F32)
            if pro == 'silu':
                hh = _silu(x)
            else:
                ms = jnp.mean(x * x, axis=-1, keepdims=True)
                y = (x * lax.rsqrt(ms + NORM_EPS)) * g_ref[...]
                hh = y * (1.0 + sc_ref[...]) + sh_ref[...]
            h_ref[...] = hh.astype(BF16)

        h = h_ref[...]
    if epi == 'rope':
        tabs = [t[...] for t in tab_refs]
        for c in range(3):
            cols = slice(c * ATT_DIM, (c + 1) * ATT_DIM)
            part = _dot(h, w_ref[:, cols])
            o_ref[:, cols] = _rope_apply(part, *tabs) if c < 2 else part
        return
    acc = _dot(h, w_ref[...].astype(BF16))
    if two:
        acc = acc + _dot(x2_ref[...].astype(BF16), w2_ref[...])
    if epi == 'bias':
        acc = acc + b_ref[...]
    elif epi == 'relu2':
        acc = jnp.square(jnp.maximum(acc, 0.0))
    elif epi in ('resgate', 'resgate_norm'):
        acc = res_ref[...] + gate_ref[...] * acc
        if epi == 'resgate_norm':
            ms = jnp.mean(acc * acc, axis=-1, keepdims=True)
            acc = (acc * lax.rsqrt(ms + NORM_EPS)) * gf_ref[...]
    o_ref[...] = acc.astype(o_ref.dtype)


def _linear(x, w, *, tm, tn, pro='cast', epi='none', norm=None, bias=None, res=None, gate=None,
            bpb=1, out_dtype=F32, second=None, rope=None, w_layer=0, final_norm=None):
    M, K = x.shape
    N = w.shape[-1]
    assert M % tm == 0 and N % tn == 0
    in_specs = [pl.BlockSpec((tm, K), lambda i, j: (i, 0))]
    args = [x]
    scratch = []
    if pro == 'normmod':
        g, mod, ksc, ksh = norm
        r = mod.shape[1]
        in_specs += [pl.BlockSpec((1, K), lambda i, j: (0, 0)),
                     pl.BlockSpec((None, r, K), lambda i, j: (i // bpb, 0, ksc)),
                     pl.BlockSpec((None, r, K), lambda i, j: (i // bpb, 0, ksh))]
        args += [g, mod, mod]
    if pro != 'cast':
        scratch = [pltpu.VMEM((tm, K), BF16)]
    if w.ndim == 3:
        in_specs.append(pl.BlockSpec((None, K, tn), lambda i, j: (w_layer, 0, j)))
    else:
        in_specs.append(pl.BlockSpec((K, tn), lambda i, j: (0, j)))
    args.append(w)
    if second is not None:
        x2, w2 = second
        K2 = x2.shape[1]
        in_specs += [pl.BlockSpec((tm, K2), lambda i, j: (i, 0)), pl.BlockSpec((K2, tn), lambda i, j: (0, j))]
        args += [x2, w2]
    if epi == 'bias':
        in_specs.append(pl.BlockSpec((1, tn), lambda i, j: (0, j)))
        args.append(bias)
    if epi in ('resgate', 'resgate_norm'):
        mod, kg = gate
        r = mod.shape[1]
        nj = N // tn
        in_specs += [pl.BlockSpec((tm, tn), lambda i, j: (i, j)),
                     pl.BlockSpec((None, r, tn), lambda i, j: (i // bpb, 0, kg * nj + j))]
        args += [res, mod]
    if epi == 'resgate_norm':
        assert tn == N
        in_specs.append(pl.BlockSpec((1, N), lambda i, j: (0, 0)))
        args.append(final_norm)
    if epi == 'rope':
        assert tn == 3 * ATT_DIM
        if rope[0].shape[0] == 1:
            in_specs += [pl.BlockSpec((1, LANES), lambda i, j: (0, 0))] * 3
        else:
            in_specs += [pl.BlockSpec((tm, LANES), lambda i, j: (i % bpb, 0))] * 3
        args += list(rope)
    return pl.pallas_call(
        functools.partial(_linear_kernel, pro=pro, epi=epi, two=second is not None),
        name="linear_%s_%s" % (pro, epi),
        grid=(M // tm, N // tn),
        in_specs=in_specs,
        out_specs=pl.BlockSpec((tm, tn), lambda i, j: (i, j)),
        out_shape=jax.ShapeDtypeStruct((M, N), out_dtype),
        scratch_shapes=scratch,
        compiler_params=_cparams("parallel", "arbitrary"),
    )(*args)


def _ssd_tail(y, xs, z, d_exp, norm_w):
    y = (y + d_exp * xs) * _silu(z)
    half = y.shape[1] // SSD_GROUPS
    outs = []
    for g in range(SSD_GROUPS):
        yg = y[:, g * half:(g + 1) * half]
        ms = jnp.mean(yg * yg, axis=-1, keepdims=True)
        outs.append(yg * lax.rsqrt(ms + NORM_EPS))
    return jnp.concatenate(outs, axis=1) * norm_w


def _ssd_prompt_kernel(z_ref, xs_ref, bc_ref, dt_ref, cw_ref, cb_ref, dtb_ref, alog_ref, dexp_ref,
                       nw_ref, tril_ref, e16_ref, y_ref, st_ref, extx, extbc, state, ybuf):
    c = pl.program_id(1)
    Q = SSD_CHUNK
    NX = SSD_HEADS * SSD_HEAD_DIM

    @pl.when(c == 0)
    def _():
        extx[0:8, :] = jnp.zeros((8, NX), F32)
        extbc[0:8, :] = jnp.zeros((8, 512), F32)
        state[...] = jnp.zeros_like(state)

    extx[8:8 + Q, :] = xs_ref[...]
    extbc[8:8 + Q, :] = bc_ref[...]
    cw = cw_ref[...]
    cb = cb_ref[...]
    xc = cb[:, 0:NX]
    bcc = cb[:, NX:NX + 512]
    for j in range(SSD_CONV):
        xc = xc + extx[pl.ds(5 + j, Q), :] * cw[j:j + 1, 0:NX]
        bcc = bcc + extbc[pl.ds(5 + j, Q), :] * cw[j:j + 1, NX:NX + 512]
    extx[0:8, :] = extx[Q:Q + 8, :]
    extbc[0:8, :] = extbc[Q:Q + 8, :]
    xs = _silu(xc)
    bcs = _silu(bcc)

    dt = _softplus(dt_ref[...] + dtb_ref[...])
    a_neg = -jnp.exp(alog_ref[...])
    acs = _dot_exact(tril_ref[...], dt * a_neg)
    acs_t = acs.T
    e16 = e16_ref[...]
    eacs = jnp.exp(acs)
    dt_exp = _dot_split(dt, e16)
    eacs_exp = _dot_split(eacs, e16)
    wend_exp = _dot_split(jnp.exp(acs[Q - 1:Q, :] - acs) * dt, e16)
    xdt = (xs * dt_exp).astype(BF16)
    xw = (xs * wend_exp).astype(BF16)
    row = lax.broadcasted_iota(jnp.int32, (Q, Q), 0)
    col = lax.broadcasted_iota(jnp.int32, (Q, Q), 1)
    causal = row >= col
    HG = SSD_HEADS // SSD_GROUPS
    GW = HG * SSD_HEAD_DIM
    for g in range(SSD_GROUPS):
        b_g = bcs[:, g * SSD_STATE:(g + 1) * SSD_STATE]
        c_g = bcs[:, 256 + g * SSD_STATE:256 + (g + 1) * SSD_STATE].astype(BF16)
        cb_g = _dot_nt(c_g, b_g.astype(BF16))
        bt_g = b_g.T.astype(BF16)
        for hg in range(HG):
            h = g * HG + hg
            seg = acs[:, h:h + 1] - acs_t[h:h + 1, :]
            decay = jnp.where(causal, jnp.exp(seg), 0.0)
            scores = (cb_g * decay).astype(BF16)
            ybuf[:, h * 64:(h + 1) * 64] = _dot(scores, xdt[:, h * 64:(h + 1) * 64])
        st_g = state[g]
        y_off = _dot(c_g, st_g.astype(BF16)) * eacs_exp[:, g * GW:(g + 1) * GW]
        ybuf[:, g * GW:(g + 1) * GW] = ybuf[:, g * GW:(g + 1) * GW] + y_off
        state[g] = st_g * eacs_exp[Q - 1:Q, g * GW:(g + 1) * GW] + _dot(bt_g, xw[:, g * GW:(g + 1) * GW])

    y_ref[...] = _ssd_tail(ybuf[...], xs, z_ref[...], dexp_ref[...], nw_ref[...]).astype(y_ref.dtype)

    @pl.when(c == pl.num_programs(1) - 1)
    def _():
        st_ref[...] = state[...]


def _ssd_prompt(u, B, L, cw, cb, dtb, alog, dexp, nw):
    Q = SSD_CHUNK
    nc = L // Q
    row = lambda b, c: b * nc + c
    const = lambda shape: pl.BlockSpec(shape, lambda b, c: (0,) * len(shape))
    y, st = pl.pallas_call(
        _ssd_prompt_kernel,
        name="ssd_prompt",
        grid=(B, nc),
        in_specs=[pl.BlockSpec((Q, 1024), lambda b, c: (row(b, c), U_Z // 1024)),
                  pl.BlockSpec((Q, 1024), lambda b, c: (row(b, c), U_XS // 1024)),
                  pl.BlockSpec((Q, 512), lambda b, c: (row(b, c), U_BC // 512)),
                  pl.BlockSpec((Q, 128), lambda b, c: (row(b, c), U_DT // 128)),
                  const((SSD_CONV, 1536)), const((1, 1536)), const((1, 128)), const((1, 128)),
                  const((1, 1024)), const((1, 1024)), const((Q, Q)), const((256, 1024))],
        out_specs=[pl.BlockSpec((Q, 1024), lambda b, c: (row(b, c), 0)),
                   pl.BlockSpec((None, SSD_GROUPS, SSD_STATE, 512), lambda b, c: (b, 0, 0, 0))],
        out_shape=[jax.ShapeDtypeStruct((B * L, 1024), BF16),
                   jax.ShapeDtypeStruct((B, SSD_GROUPS, SSD_STATE, 512), F32)],
        scratch_shapes=[pltpu.VMEM((Q + 8, 1024), F32), pltpu.VMEM((Q + 8, 512), F32),
                        pltpu.VMEM((SSD_GROUPS, SSD_STATE, 512), F32), pltpu.VMEM((Q, 1024), F32)],
        compiler_params=_cparams("parallel", "arbitrary"),
    )(u, u, u, u, cw, cb, dtb, alog, dexp, nw, _tril_ones(Q), _twice(_head_expand(SSD_HEADS, 64)))
    st = st.reshape(B, SSD_GROUPS, SSD_STATE, SSD_HEADS // SSD_GROUPS, SSD_HEAD_DIM)
    st = jnp.transpose(st, (0, 1, 3, 4, 2)).reshape(B, SSD_HEADS, SSD_HEAD_DIM, SSD_STATE)
    return y, st


def _ssd_step_kernel(z_ref, xs_ref, bc_ref, dt_ref, cx_ref, cbc_ref, s_ref, cw_ref, cb_ref, dtb_ref,
                     alog_ref, dexp_ref, nw_ref, e2_ref, y_ref, so_ref, ybuf):
    NX = SSD_HEADS * SSD_HEAD_DIM
    cw = cw_ref[...]
    cb = cb_ref[...]
    cx = cx_ref[...]
    cbc = cbc_ref[...]
    xc = cb[:, 0:NX] + xs_ref[...] * cw[3:4, 0:NX]
    bcc = cb[:, NX:NX + 512] + bc_ref[...] * cw[3:4, NX:NX + 512]
    for j in range(SSD_CONV - 1):
        xc = xc + cx[j:j + 1, :] * cw[j:j + 1, 0:NX]
        bcc = bcc + cbc[j:j + 1, :] * cw[j:j + 1, NX:NX + 512]
    xs = _silu(xc)
    bcs = _silu(bcc)
    dt = _softplus(dt_ref[...] + dtb_ref[...])
    da = jnp.exp(dt * (-jnp.exp(alog_ref[...])))
    e2 = e2_ref[...]
    lane = lax.broadcasted_iota(jnp.int32, (64, LANES), 1)
    first = lane < 64
    HG = SSD_HEADS // SSD_GROUPS
    for q in range(SSD_HEADS // 2):
        xrow = xs[:, q * LANES:(q + 1) * LANES]
        diag = e2 * xrow
        ycols = []
        for s in range(2):
            h = 2 * q + s
            g = h // HG
            xcol = jnp.sum(jnp.where(first == (s == 0), diag, 0.0), axis=1, keepdims=True)
            b_row = bcs[:, g * SSD_STATE:(g + 1) * SSD_STATE]
            c_row = bcs[:, 256 + g * SSD_STATE:256 + (g + 1) * SSD_STATE]
            s_new = s_ref[h] * da[:, h:h + 1] + (xcol * dt[:, h:h + 1]) * b_row
            so_ref[h] = s_new
            ycols.append(jnp.sum(s_new * c_row, axis=1, keepdims=True))
        ypair = jnp.where(first, ycols[0], ycols[1])
        ybuf[:, q * LANES:(q + 1) * LANES] = jnp.sum(e2 * ypair, axis=0, keepdims=True)
    y_ref[...] = _ssd_tail(ybuf[...], xs, z_ref[...], dexp_ref[...], nw_ref[...]).astype(y_ref.dtype)


def _ssd_step(u, conv_x, conv_bc, s0, cw, cb, dtb, alog, dexp, nw):
    B = u.shape[0]
    const = lambda shape: pl.BlockSpec(shape, lambda b: (0,) * len(shape))
    return pl.pallas_call(
        _ssd_step_kernel,
        name="ssd_step",
        grid=(B,),
        in_specs=[pl.BlockSpec((None, 1, 1024), lambda b: (b, 0, U_Z // 1024)),
                  pl.BlockSpec((None, 1, 1024), lambda b: (b, 0, U_XS // 1024)),
                  pl.BlockSpec((None, 1, 512), lambda b: (b, 0, U_BC // 512)),
                  pl.BlockSpec((None, 1, 128), lambda b: (b, 0, U_DT // 128)),
                  pl.BlockSpec((None, 3, 1024), lambda b: (b, 0, 0)),
                  pl.BlockSpec((None, 3, 512), lambda b: (b, 0, 0)),
                  pl.BlockSpec((None, SSD_HEADS, 64, 128), lambda b: (b, 0, 0, 0)),
                  const((SSD_CONV, 1536)), const((1, 1536)), const((1, 128)), const((1, 128)),
                  const((1, 1024)), const((1, 1024)), const((64, 128))],
        out_specs=[pl.BlockSpec((None, 1, 1024), lambda b: (b, 0, 0)),
                   pl.BlockSpec((None, SSD_HEADS, 64, 128), lambda b: (b, 0, 0, 0))],
        out_shape=[jax.ShapeDtypeStruct((B, 1, 1024), BF16),
                   jax.ShapeDtypeStruct(s0.shape, F32)],
        scratch_shapes=[pltpu.VMEM((1, 1024), F32)],
        compiler_params=_cparams("parallel"),
    )(u, u, u, u, conv_x, conv_bc, s0, cw, cb, dtb, alog, dexp, nw, _pair_eye())


def _rwkv_prep_kernel(*refs, shifted, bpb):
    refs = list(refs)
    cur = [refs.pop(0) for _ in range(4)]
    prev = [refs.pop(0) for _ in range(4)]
    if shifted:
        first = [refs.pop(0) for _ in range(4)]
    mu = [refs.pop(0) for _ in range(4)]
    (w0_ref, w2_ref, a0_ref, a2_ref, g2_ref, kk_ref, ka_ref, bo_ref) = [refs.pop(0) for _ in range(8)]
    (r_o, w_o, k_o, v_o, kn_o, kka_o, g_o) = refs
    i = pl.program_id(0)

    def mixed(n):
        x = cur[n][...]
        if shifted:
            rolled = pltpu.roll(x, 1, 0)
            before = jnp.where(i % bpb == 0, first[n][...], prev[n][7:8, :])
            rid = lax.broadcasted_iota(jnp.int32, x.shape, 0)
            p = jnp.where(rid == 0, before, rolled)
        else:
            p = prev[n][...]
        return x + (p - x) * mu[n][...]

    r, k, v, lw = mixed(0), mixed(1), mixed(2), mixed(3)
    blk = lw[:, 0:LANES]
    lane = lax.broadcasted_iota(jnp.int32, blk.shape, 1)
    tw = jnp.where(lane < 64, jnp.tanh(blk), blk).astype(BF16)
    wpre = w0_ref[...] + _dot(tw, w2_ref[...])
    apre = a0_ref[...] + _dot(tw, a2_ref[...])
    wlog = -_softplus(-wpre) - 0.5
    a = jax.nn.sigmoid(apre)
    g = _dot(jax.nn.sigmoid(lw[:, LANES:2 * LANES]).astype(BF16), g2_ref[...])
    kk = k * kk_ref[...]
    kk2 = kk * kk
    bo = bo_ref[...]
    for q in range(RWKV_HEADS // 2):
        sl = slice(q * LANES, (q + 1) * LANES)
        n2 = _dot_exact(kk2[:, sl], bo)
        kn = kk[:, sl] / jnp.maximum(jnp.sqrt(n2), 1e-12)
        kn_o[:, sl] = -kn
        kka_o[:, sl] = kn * a[:, sl]
    r_o[...] = r
    w_o[...] = -jnp.exp(wlog)
    k_o[...] = k * (1.0 + (a - 1.0) * ka_ref[...])
    v_o[...] = v
    g_o[...] = g


def _rwkv_prep(u, prev, first, mus, ws, *, tm, bpb, shifted):
    M = u.shape[0]
    cols = [(1024, U_R // 1024), (1024, U_K // 1024), (1024, U_V // 1024), (256, U_LW // 256)]
    in_specs = [pl.BlockSpec((tm, c), functools.partial(lambda i, kb: (i, kb), kb=kb)) for c, kb in cols]
    args = [u] * 4
    if shifted:
        in_specs += [pl.BlockSpec((8, c), functools.partial(
            lambda i, kb: (jnp.maximum(i * (tm // 8) - 1, 0), kb), kb=kb)) for c, kb in cols]
        args += [u] * 4
        in_specs += [pl.BlockSpec((None, 1, c), lambda i: (i // bpb, 0, 0)) for c, _ in cols]
        args += list(first)
    else:
        in_specs += [pl.BlockSpec((tm, c), lambda i: (i, 0)) for c, _ in cols]
        args += list(prev)
    in_specs += [pl.BlockSpec((1, c), lambda i: (0, 0)) for c, _ in cols]
    args += list(mus)
    wshapes = [(1, 1024), (128, 1024), (1, 1024), (128, 1024), (128, 1024), (1, 1024), (1, 1024), (128, 128)]
    in_specs += [pl.BlockSpec(s, lambda i: (0, 0)) for s in wshapes]
    args += list(ws)
    return pl.pallas_call(
        functools.partial(_rwkv_prep_kernel, shifted=shifted, bpb=bpb),
        name="rwkv_prep",
        grid=(M // tm,),
        in_specs=in_specs,
        out_specs=[pl.BlockSpec((tm, 1024), lambda i: (i, 0))] * 7,
        out_shape=[jax.ShapeDtypeStruct((M, 1024), F32)] * 7,
        compiler_params=_cparams("parallel"),
    )(*args)


def _wkv_step_kernel(r_ref, lw_ref, k_ref, v_ref, kn_ref, ka_ref, s0_ref, o_ref, sT_ref, *, nb):
    n = RWKV_HEAD_DIM
    eye = lax.broadcasted_iota(jnp.int32, (n, n), 0) == lax.broadcasted_iota(jnp.int32, (n, n), 1)
    for b in range(nb):
        for h in range(RWKV_HEADS):
            sl = slice(h * n, (h + 1) * n)
            S = s0_ref[b, h]
            sa = jnp.sum(S * kn_ref[b, :, sl], axis=1, keepdims=True)
            vcol = jnp.sum(jnp.where(eye, v_ref[b, :, sl], 0.0), axis=1, keepdims=True)
            S = S * jnp.exp(lw_ref[b, :, sl]) + sa * ka_ref[b, :, sl] + vcol * k_ref[b, :, sl]
            sT_ref[b, h] = S
            ocol = jnp.sum(S * r_ref[b, :, sl], axis=1, keepdims=True)
            o_ref[b, :, sl] = jnp.sum(jnp.where(eye, ocol, 0.0), axis=0, keepdims=True)


def _wkv_step(r, lw, k, v, kn, ka, s0, *, nb):
    B = r.shape[0]
    seq = pl.BlockSpec((nb, 1, 1024), lambda b: (b, 0, 0))
    stt = pl.BlockSpec((nb,) + s0.shape[1:], lambda b: (b, 0, 0, 0))
    return pl.pallas_call(
        functools.partial(_wkv_step_kernel, nb=nb),
        grid=(B // nb,),
        in_specs=[seq] * 6 + [stt],
        out_specs=[seq, stt],
        out_shape=[jax.ShapeDtypeStruct((B, 1, 1024), F32), jax.ShapeDtypeStruct(s0.shape, F32)],
        compiler_params=_cparams("parallel"),
        name="wkv_step",
    )(r, lw, k, v, kn, ka, s0)


WKV_CHUNK = 64


def _wkv_chunk_kernel(r_ref, lw_ref, k_ref, v_ref, kn_ref, ka_ref, tril_ref, o_ref, sT_ref, S_ref, *, nb):
    C = WKV_CHUNK
    c = pl.program_id(1)

    @pl.when(c == 0)
    def _():
        S_ref[...] = jnp.zeros_like(S_ref)

    tril = tril_ref[...]
    lane = lax.broadcasted_iota(jnp.int32, (C, LANES), 1)
    rowi = lax.broadcasted_iota(jnp.int32, (C, LANES), 0)
    first = lane < 64
    strict = rowi > (lane % 64)
    incl = rowi >= (lane % 64)
    r128 = lax.broadcasted_iota(jnp.int32, (LANES, LANES), 0)
    c128 = lax.broadcasted_iota(jnp.int32, (LANES, LANES), 1)
    diag_blocks = (r128 < 64) == (c128 < 64)
    eye = r128 == c128

    def bd(x):
        return jnp.concatenate([jnp.where(first, x, 0.0), jnp.where(first, 0.0, x)], axis=0)

    bf = lambda x: x.astype(BF16)
    pairs = [(b, p) for b in range(nb) for p in range(RWKV_HEADS // 2)]
    sls = [slice(p * LANES, (p + 1) * LANES) for _, p in pairs]
    load = lambda ref: [ref[b, :, sl] for (b, _), sl in zip(pairs, sls)]
    each = lambda f, *ls: [f(*a) for a in zip(*ls)]
    r_, lw, kt, vv, al, be = (load(ref) for ref in (r_ref, lw_ref, k_ref, v_ref, kn_ref, ka_ref))
    def cumulative(x):
        hi = x.astype(BF16)
        lo = (x - hi.astype(F32)).astype(BF16)
        return _dot(tril, jnp.concatenate([hi, lo], axis=0))

    cs = each(cumulative, lw)
    last = each(lambda x: x[C - 1:C, :], cs)
    e_inv = each(lambda x: jnp.exp(-x), cs)
    aq = each(lambda a, x, l: a * jnp.exp(x - l), al, cs, lw)
    rq = each(lambda a, x: a * jnp.exp(x), r_, cs)
    bk = each(jnp.multiply, be, e_inv)
    kk = each(jnp.multiply, kt, e_inv)
    g = each(lambda a, q, b_, k_: _dot_exact_nt(
        jnp.concatenate([a, q], axis=0),
        jnp.concatenate([jnp.where(first, b_, 0.0), jnp.where(first, 0.0, b_),
                         jnp.where(first, k_, 0.0), jnp.where(first, 0.0, k_)], axis=0)), aq, rq, bk, kk)
    m1 = each(lambda x: bf(jnp.where(strict, x[0:C, 0:LANES], 0.0)), g)
    m2 = each(lambda x: bf(jnp.where(strict, x[0:C, LANES:2 * LANES], 0.0)), g)
    n1 = each(lambda x: bf(jnp.where(incl, x[C:2 * C, 0:LANES], 0.0)), g)
    n2 = each(lambda x: bf(jnp.where(incl, x[C:2 * C, LANES:2 * LANES], 0.0)), g)
    s0 = [S_ref[b, p] for b, p in pairs]
    s0b = each(bf, s0)
    vbd = each(lambda x: bf(bd(x)), vv)
    x = each(lambda a, s, m, v_: _dot(bf(a), s) + _dot(m, v_), aq, s0b, m2, vbd)
    mp = m1
    steps = int(math.log2(C))
    for i in range(steps):
        x = each(lambda x_, m: x_ + _dot(m, bf(bd(x_))), x, mp)
        if i + 1 < steps:
            mp = each(lambda m: bf(_dot(m, bd(m))), mp)
    o = each(lambda q, s, a, x_, b_, v_: _dot(bf(q), s) + _dot(a, bf(bd(x_))) + _dot(b_, v_),
             rq, s0b, n1, x, n2, vbd)
    for (b, _), sl, o_ in zip(pairs, sls, o):
        o_ref[b, :, sl] = o_
    e_end = each(lambda l, x_: jnp.exp(l - x_), last, cs)
    kv_t = each(lambda b_, k_, e: jnp.concatenate([b_ * e, k_ * e], axis=0).T, be, kt, e_end)
    upd = each(lambda t, x_, v_: _dot(bf(t), bf(jnp.concatenate([x_, v_], axis=0))), kv_t, x, vv)
    gcol = each(lambda l: jnp.sum(jnp.where(eye, jnp.exp(l), 0.0), axis=1, keepdims=True), last)
    for (b, p), s, u, gc in zip(pairs, s0, upd, gcol):
        S_ref[b, p] = jnp.where(diag_blocks, gc * s + u, 0.0)

    @pl.when(c == pl.num_programs(1) - 1)
    def _():
        sT_ref[...] = S_ref[...]


def _wkv_chunked(r, lw, k, v, kn, ka, *, nb):
    B, L, _ = r.shape
    C = WKV_CHUNK
    seq = pl.BlockSpec((nb, C, 1024), lambda b, c: (b, c, 0))
    stt = pl.BlockSpec((nb, 8, LANES, LANES), lambda b, c: (b, 0, 0, 0))
    o, sT = pl.pallas_call(
        functools.partial(_wkv_chunk_kernel, nb=nb),
        grid=(B // nb, L // C),
        in_specs=[seq] * 6 + [pl.BlockSpec((C, 2 * C), lambda b, c: (0, 0))],
        out_specs=[seq, stt],
        out_shape=[jax.ShapeDtypeStruct((B, L, 1024), F32), jax.ShapeDtypeStruct((B, 8, LANES, LANES), F32)],
        scratch_shapes=[pltpu.VMEM((nb, 8, LANES, LANES), F32)],
        compiler_params=_cparams("parallel", "arbitrary"),
        name="wkv_chunked",
    )(r, lw, k, v, kn, ka, jnp.concatenate([_tril_ones(C)] * 2, axis=1).astype(BF16))
    blocks = jnp.stack([sT[:, :, 0:64, 0:64], sT[:, :, 64:128, 64:128]], axis=2)
    return o, jnp.swapaxes(blocks, -1, -2).reshape(B, 16, 64, 64)


def _rwkv_post_kernel(o_ref, r_ref, k_ref, v_ref, g_ref, lnw_ref, lnb_ref, rk_ref, bo_ref, y_ref):
    bo = bo_ref[...]
    inv = 1.0 / RWKV_HEAD_DIM
    for q in range(RWKV_HEADS // 2):
        sl = slice(q * LANES, (q + 1) * LANES)
        o = o_ref[:, sl]
        mean = _dot_split(o, bo) * inv
        d = o - mean
        var = _dot_split(d * d, bo) * inv
        on = d * lax.rsqrt(var + RWKV_LN_EPS) * lnw_ref[:, sl] + lnb_ref[:, sl]
        bonus = _dot_split(r_ref[:, sl] * k_ref[:, sl] * rk_ref[:, sl], bo) * v_ref[:, sl]
        y_ref[:, sl] = ((on + bonus) * g_ref[:, sl]).astype(y_ref.dtype)


def _rwkv_post(o, r, k, v, g, lnw, lnb, rk, tm):
    M = o.shape[0]
    blk = pl.BlockSpec((tm, 1024), lambda i: (i, 0))
    vec = pl.BlockSpec((1, 1024), lambda i: (0, 0))
    return pl.pallas_call(
        _rwkv_post_kernel,
        name="rwkv_post",
        grid=(M // tm,),
        in_specs=[blk] * 5 + [vec] * 3 + [pl.BlockSpec((256, 128), lambda i: (0, 0))],
        out_specs=blk,
        out_shape=jax.ShapeDtypeStruct((M, 1024), BF16),
        compiler_params=_cparams("parallel"),
    )(o, r, k, v, g, lnw, lnb, rk, _twice(_block_ones()))


def _rope_tables(pos):
    half = ROPE_DIM // 2
    inv = ROPE_THETA ** (-jnp.arange(half, dtype=F32) * 2.0 / ROPE_DIM)
    ang = pos.astype(F32)[:, None] * inv
    cos, sin = jnp.cos(ang), jnp.sin(ang)
    n = pos.shape[0]
    rest = ATT_HEAD_DIM - ROPE_DIM
    c = jnp.concatenate([cos, cos, jnp.ones((n, rest), F32)], axis=1)
    s_next = jnp.concatenate([-sin, jnp.zeros((n, half + rest), F32)], axis=1)
    s_prev = jnp.concatenate([jnp.zeros((n, half), F32), sin, jnp.zeros((n, rest), F32)], axis=1)
    return tuple(jnp.concatenate([t, t], axis=1) for t in (c, s_next, s_prev))


def _rope_apply(x, c, s_next, s_prev):
    n = x.shape[1]
    reps = n // LANES
    tile = lambda t: jnp.concatenate([t] * reps, axis=1)
    half = ROPE_DIM // 2
    return x * tile(c) + pltpu.roll(x, n - half, 1) * tile(s_next) + pltpu.roll(x, half, 1) * tile(s_prev)


ATT_BLOCK_ROWS = ATT_Q_BLOCK * max(d for _, d in ATT_GROUPS)
ATT_BATCH = 4


def _attn_prompt_kernel(q_ref, kc_ref, kp_ref, vc_ref, vp_ref, o_ref, lse_ref, *, d):
    i = pl.program_id(0)
    QB = ATT_Q_BLOCK
    R = q_ref.shape[0]
    row = lax.broadcasted_iota(jnp.int32, (QB, 2 * QB), 0)
    col = lax.broadcasted_iota(jnp.int32, (QB, 2 * QB), 1)
    band = (col >= row) & (col <= row + QB)
    band_first = band & ((i > 0) | (col >= QB))
    first = lax.broadcasted_iota(jnp.int32, (QB, LANES), 1) < ATT_HEAD_DIM
    first_kv = lax.broadcasted_iota(jnp.int32, (2 * QB, LANES), 1) < ATT_HEAD_DIM
    scale = ATT_HEAD_DIM ** -0.5
    span = QB * d
    blocks = [(rho, j) for rho in range(d) for j in range(R // span)]
    each = lambda f, *ls: [f(*a) for a in zip(*ls)]
    for b0 in range(0, len(blocks), ATT_BATCH):
        batch = blocks[b0:b0 + ATT_BATCH]
        ds = lambda start, n: pl.ds(start, n, stride=d) if d > 1 else pl.ds(start, n)
        qrows = [ds(rho + span * j, QB) for rho, j in batch]
        valid = [band_first if j == 0 else band for _, j in batch]

        def keys(cur_ref, prev_ref, rho, j):
            if j > 0:
                return cur_ref[ds(rho + span * (j - 1), 2 * QB), :]
            return jnp.concatenate([prev_ref[ds(R - span + rho, QB), :], cur_ref[ds(rho, QB), :]], axis=0)

        qb = [q_ref[r, :].astype(BF16) for r in qrows]
        k2 = [keys(kc_ref, kp_ref, rho, j).astype(BF16) for rho, j in batch]
        v2 = [keys(vc_ref, vp_ref, rho, j).astype(BF16) for rho, j in batch]
        halves = [slice(0, ATT_HEAD_DIM), slice(ATT_HEAD_DIM, LANES)]
        s = [[jnp.where(vm, _dot_nt(q[:, sl], k[:, sl]) * scale, -jnp.inf) for sl in halves]
             for q, k, vm in zip(qb, k2, valid)]
        m = [[jnp.max(jnp.maximum(x[:, 0:QB], x[:, QB:2 * QB]), axis=1, keepdims=True) for x in pair] for pair in s]
        p = [[jnp.exp(x - mx).astype(BF16) for x, mx in zip(ps, ms)] for ps, ms in zip(s, m)]
        va = each(lambda v: jnp.where(first_kv, v, 1.0), v2)
        vb = each(lambda v: jnp.where(first_kv, 1.0, v), v2)
        ea = each(lambda pp, v: _dot(pp[0], v), p, va)
        eb = each(lambda pp, v: _dot(pp[1], v), p, vb)
        num = each(lambda a, b_: jnp.where(first, a, b_), ea, eb)
        den = each(lambda a, b_: pltpu.roll(jnp.where(first, b_, a), ATT_HEAD_DIM, 1), ea, eb)
        for r, n_, d_, mm in zip(qrows, num, den, m):
            o_ref[r, :] = n_ / d_
            lse_ref[r, :] = jnp.where(first, mm[0], mm[1]) + jnp.log(d_)


def _attn_prompt(qkv, B, L, gi):
    window, d = ATT_GROUPS[gi]
    R = min(ATT_BLOCK_ROWS, L)
    assert window == ATT_Q_BLOCK * d and R % (ATT_Q_BLOCK * d) == 0 and L % R == 0
    nblk = L // R
    npair = ATT_DIM // LANES
    prev = lambda i: jnp.maximum(i - 1, 0)

    def col(which, f):
        return lambda i, b, hp: (b * nblk + f(i), gi * 3 * npair + which * npair + hp)

    same = lambda i: i
    blk = lambda f: pl.BlockSpec((R, LANES), f)
    out = pl.BlockSpec((R, LANES), lambda i, b, hp: (b * nblk + i, hp))
    return pl.pallas_call(
        functools.partial(_attn_prompt_kernel, d=d),
        grid=(nblk, B, npair),
        in_specs=[blk(col(0, same)), blk(col(1, same)), blk(col(1, prev)), blk(col(2, same)), blk(col(2, prev))],
        out_specs=[out, out],
        out_shape=[jax.ShapeDtypeStruct((B * L, ATT_DIM), F32)] * 2,
        compiler_params=_cparams("arbitrary", "arbitrary", "arbitrary"),
        name="attn_prompt_d%d" % d,
    )(qkv, qkv, qkv, qkv, qkv)


def _attn_out_kernel(o0, l0, o1, l1, o2, l2, w_ref, res_ref, gate_ref, out_ref, h_ref):
    @pl.when(pl.program_id(1) == 0)
    def _():
        m = jnp.maximum(jnp.maximum(l0[...], l1[...]), l2[...])
        a0, a1, a2 = jnp.exp(l0[...] - m), jnp.exp(l1[...] - m), jnp.exp(l2[...] - m)
        o = (a0 * o0[...] + a1 * o1[...] + a2 * o2[...]) / (a0 + a1 + a2)
        h_ref[...] = o.astype(BF16)

    out_ref[...] = res_ref[...] + gate_ref[...] * _dot(h_ref[...], w_ref[...])


def _attn_out(ols, w, res, mod, kg, *, tm, tn, bpb):
    M = res.shape[0]
    N = w.shape[1]
    r = mod.shape[1]
    nj = N // tn
    part = pl.BlockSpec((tm, 512), lambda i, j: (i, 0))
    return pl.pallas_call(
        _attn_out_kernel,
        name="attn_out",
        grid=(M // tm, nj),
        in_specs=[part] * 6 + [pl.BlockSpec((512, tn), lambda i, j: (0, j)),
                               pl.BlockSpec((tm, tn), lambda i, j: (i, j)),
                               pl.BlockSpec((None, r, tn), lambda i, j: (i // bpb, 0, kg * nj + j))],
        out_specs=pl.BlockSpec((tm, tn), lambda i, j: (i, j)),
        out_shape=jax.ShapeDtypeStruct((M, N), F32),
        scratch_shapes=[pltpu.VMEM((tm, 512), BF16)],
        compiler_params=_cparams("parallel", "arbitrary"),
    )(*ols, w, res, mod)


def _attn_step_kernel(q_ref, kn_ref, vn_ref, c0_ref, c1_ref, c2_ref, o_ref):
    caches = (c0_ref, c1_ref, c2_ref)
    scale = ATT_HEAD_DIM ** -0.5
    hl = lax.broadcasted_iota(jnp.int32, (ATT_HEAD_DIM, ATT_HEADS), 1)
    ng = len(ATT_GROUPS)
    combos = [(h, gi) for h in range(ATT_HEADS) for gi in range(ng)]
    each = lambda f, *ls: [f(*a) for a in zip(*ls)]
    col = lambda ref: [ref[gi][:, h:h + 1] for h, gi in combos]
    q, kn, vn = col(q_ref), col(kn_ref), col(vn_ref)
    vis = [lax.broadcasted_iota(jnp.int32, (1, w), 1) % d == 0 for w, d in ATT_GROUPS]
    s = [jnp.where(vis[gi], jnp.sum(caches[gi][0, h] * q_, axis=0, keepdims=True) * scale, -jnp.inf)
         for (h, gi), q_ in zip(combos, q)]
    s_new = each(lambda a, b: jnp.sum(a * b, axis=0, keepdims=True) * scale, q, kn)
    m = each(lambda a, b: jnp.maximum(jnp.max(a, axis=1, keepdims=True), b), s, s_new)
    p = each(lambda a, b: jnp.exp(a - b), s, m)
    p_new = each(lambda a, b: jnp.exp(a - b), s_new, m)
    l = each(lambda a, b: jnp.sum(a, axis=1, keepdims=True) + b, p, p_new)
    acc = [jnp.sum(caches[gi][1, h] * p_, axis=1, keepdims=True) + pn * v_
           for (h, gi), p_, pn, v_ in zip(combos, p, p_new, vn)]
    og = each(lambda a, b: a / b, acc, l)
    lse = each(lambda a, b: a + jnp.log(b), m, l)
    out = jnp.zeros((ATT_HEAD_DIM, ATT_HEADS), F32)
    for h in range(ATT_HEADS):
        os_, ls_ = og[h * ng:(h + 1) * ng], lse[h * ng:(h + 1) * ng]
        mm = jnp.maximum(jnp.maximum(ls_[0], ls_[1]), ls_[2])
        ws = [jnp.exp(x - mm) for x in ls_]
        o = (ws[0] * os_[0] + ws[1] * os_[1] + ws[2] * os_[2]) / (ws[0] + ws[1] + ws[2])
        out = jnp.where(hl == h, o, out)
    o_ref[...] = out


def _window_minor(c):
    return jnp.transpose(c, (0, 1, 3, 4, 5, 2))


def _attn_step(q3, kn3, vn3, caches_t, layer):
    B = q3.shape[0]
    for (window, d), c in zip(ATT_GROUPS, caches_t):
        assert c.shape[-1] == window and window // d == 128
    cols = lambda t: jnp.swapaxes(t, -1, -2)
    specs = [pl.BlockSpec((None, None) + c.shape[2:], lambda b: (layer, b, 0, 0, 0, 0)) for c in caches_t]
    new = pl.BlockSpec((None, len(ATT_GROUPS), ATT_HEAD_DIM, ATT_HEADS), lambda b: (b, 0, 0, 0))
    o = pl.pallas_call(
        _attn_step_kernel,
        grid=(B,),
        in_specs=[new] * 3 + specs,
        out_specs=pl.BlockSpec((None, ATT_HEAD_DIM, ATT_HEADS), lambda b: (b, 0, 0)),
        out_shape=jax.ShapeDtypeStruct((B, ATT_HEAD_DIM, ATT_HEADS), F32),
        compiler_params=_cparams("parallel"),
        name="attn_step",
    )(cols(q3), cols(kn3), cols(vn3), *caches_t)
    return jnp.swapaxes(o, -1, -2)


def _roll_kernel(c0, c1, c2, n0, n1, n2, o0, o1, o2):
    for c_ref, n_ref, o_ref in ((c0, n0, o0), (c1, n1, o1), (c2, n2, o2)):
        w = c_ref.shape[-1]
        rows = c_ref.shape[0] * c_ref.shape[1] * c_ref.shape[2]
        x = c_ref[...].reshape(rows, w)
        lane = lax.broadcasted_iota(jnp.int32, (rows, w), 1)
        y = jnp.where(lane == w - 1, n_ref[...].reshape(rows, 1), pltpu.roll(x, w - 1, 1))
        o_ref[...] = y.reshape(o_ref.shape)


def _roll_windows(caches_t, rows_t):
    NC, B = caches_t[0].shape[:2]
    spec = lambda a: pl.BlockSpec((None, None) + a.shape[2:], lambda i, b: (i, b, 0, 0, 0, 0))
    return pl.pallas_call(
        _roll_kernel,
        grid=(NC, B),
        in_specs=[spec(c) for c in caches_t] + [spec(r) for r in rows_t],
        out_specs=[spec(c) for c in caches_t],
        out_shape=[jax.ShapeDtypeStruct(c.shape, c.dtype) for c in caches_t],
        compiler_params=_cparams("parallel", "parallel"),
        name="roll_windows",
    )(*caches_t, *rows_t)


def _hyb_params(i, hyb_w_in, hyb_w_out, ssd_conv_w, ssd_conv_b, ssd_dt_bias, ssd_a_log, ssd_d, ssd_norm_w,
                rwkv_mu, rwkv_w0, rwkv_w2, rwkv_a0, rwkv_a2, rwkv_g2, rwkv_k_k, rwkv_k_a, rwkv_r_k,
                rwkv_ln_w, rwkv_ln_b):
    w = hyb_w_in[i]
    rw0 = 2576
    w_perm = jnp.concatenate(
        [w[:, 0:1024], w[:, 1024:2048], w[:, rw0:rw0 + 3072], w[:, 2048:2560], w[:, rw0 + 3072:rw0 + 3328],
         w[:, 2560:2576], jnp.zeros((D_MODEL, U_COLS - U_DT - 16), F32)], axis=1).astype(BF16)
    pad128 = lambda v: jnp.concatenate([v, jnp.zeros((LANES - v.shape[0],), F32)])[None, :]
    z64 = jnp.zeros((64, 1024), F32)
    mu = rwkv_mu[i]
    p = dict(
        w_in=w_perm, w_out=hyb_w_out[i].astype(BF16),
        cw=ssd_conv_w[i], cb=ssd_conv_b[i][None, :], dtb=pad128(ssd_dt_bias[i]), alog=pad128(ssd_a_log[i]),
        dexp=jnp.repeat(ssd_d[i], SSD_HEAD_DIM)[None, :], nw=ssd_norm_w[i][None, :],
        mus=[mu[None, 0:1024], mu[None, 1024:2048], mu[None, 2048:3072], mu[None, 3072:3328]],
        prep_w=[rwkv_w0[i][None, :], jnp.concatenate([rwkv_w2[i], z64]).astype(BF16),
                rwkv_a0[i][None, :], jnp.concatenate([z64, rwkv_a2[i]]).astype(BF16),
                rwkv_g2[i].astype(BF16), rwkv_k_k[i][None, :], rwkv_k_a[i][None, :], _block_ones()],
        lnw=rwkv_ln_w[i][None, :], lnb=rwkv_ln_b[i][None, :], rk=rwkv_r_k[i].reshape(1, 1024),
    )
    return p


def _raw_conv_rows(u_rows):
    return jnp.concatenate([u_rows[..., U_XS:U_XS + 1024], u_rows[..., U_BC:U_BC + 512]], axis=-1)


def _raw_rw_rows(u_rows):
    return jnp.concatenate([u_rows[..., U_R:U_R + 3072], u_rows[..., U_LW:U_LW + 256]], axis=-1)


def _run_prompt(x, mods, P, hyb, att, norm_final, B, L):
    T = B * L
    big, half = PROMPT_ROWS, PROMPT_ROWS // 2
    lin = lambda *a, tm, **kw: _linear(*a, tm=tm, bpb=L // tm, **kw)
    new = dict(ssd=[], conv=[], wkv=[], shift=[], win=[[], [], []])
    tabs = _rope_tables(jnp.arange(L))
    for l in range(DEPTH):
        mod = mods[l]
        i = l // 2
        gmix = P['norm_mix'][l][None, :]
        if l % 2 == 0:
            hp = hyb[i]
            u = lin(x, hp['w_in'], tm=big, tn=1024, pro='normmod', norm=(gmix, mod, 1, 0))
            y_ssd, s_ssd = _ssd_prompt(u, B, L, hp['cw'], hp['cb'], hp['dtb'], hp['alog'], hp['dexp'], hp['nw'])
            zeros = [jnp.zeros((B, 1, c), F32) for c in (1024, 1024, 1024, 256)]
            r, w, k, v, kn, ka, g = _rwkv_prep(u, None, zeros, hp['mus'], hp['prep_w'], tm=256, bpb=L // 256,
                                               shifted=True)
            sh = lambda t: t.reshape(B, L, 1024)
            o, s_wkv = _wkv_chunked(sh(r), sh(w), sh(k), sh(v), sh(kn), sh(ka), nb=B)
            y_rwkv = _rwkv_post(o.reshape(T, 1024), r, k, v, g, hp['lnw'], hp['lnb'], hp['rk'], 256)
            x = lin(y_ssd, hp['w_out'][:1024], tm=big, tn=1024, epi='resgate', res=x, gate=(mod, 2),
                    second=(y_rwkv, hp['w_out'][1024:]))
            u3 = u.reshape(B, L, U_COLS)
            new['ssd'].append(s_ssd)
            new['conv'].append(_raw_conv_rows(u3[:, L - (SSD_CONV - 1):]))
            new['wkv'].append(s_wkv)
            new['shift'].append(_raw_rw_rows(u3[:, L - 1]))
        else:
            ap = att[i]
            qkv = lin(x, ap['w_qkv'], tm=big, tn=3 * ATT_DIM, pro='normmod', epi='rope',
                      norm=(gmix, mod, 1, 0), rope=tabs)
            ols = []
            q3 = qkv.reshape(B, L, len(ATT_GROUPS) * 3 * ATT_DIM)
            for gi, (window, d) in enumerate(ATT_GROUPS):
                ols += _attn_prompt(qkv, B, L, gi)
                keep = min(window, L)
                k0 = (gi * 3 + 1) * ATT_DIM
                kv = q3[:, L - keep:, k0:k0 + 2 * ATT_DIM]
                new['win'][gi].append(kv.reshape(B, keep, 2, ATT_HEADS, ATT_HEAD_DIM))
            x = _attn_out(ols, ap['w_out'], x, mod, 2, tm=half, tn=1024, bpb=L // half)
        gmlp = P['norm_mlp'][l][None, :]
        hid = lin(x, P['w1'][l], tm=big, tn=1024, pro='normmod', epi='relu2', norm=(gmlp, mod, 4, 3),
                  out_dtype=BF16)
        last = l == DEPTH - 1
        x = lin(hid, P['w2'][l], tm=half, tn=1024, epi='resgate_norm' if last else 'resgate', res=x,
                gate=(mod, 5), final_norm=norm_final[None, :] if last else None)
    y = x.reshape(B, L, D_MODEL)
    return y, new


def _run_sample(x, mods, P, hyb, att, norm_final, states, B):
    state_ssd, state_conv, state_wkv, state_shift, caches = states
    tm = B
    new = dict(ssd=[], conv=[], wkv=[], shift=[], win=[[], [], []])
    tabs = _rope_tables(jnp.full((1,), PAST_LEN, jnp.int32))
    caches_t = [_window_minor(c) for c in caches]
    for l in range(DEPTH):
        mod = mods[l]
        i = l // 2
        gmix = P['norm_mix'][l][None, :]
        if l % 2 == 0:
            hp = hyb[i]
            u = _linear(x, hp['w_in'], tm=tm, tn=512, pro='normmod', norm=(gmix, mod, 1, 0))
            u3 = u.reshape(B, 1, U_COLS)
            cbuf = state_conv[i]
            y_ssd, s_ssd = _ssd_step(u3, cbuf[:, :, 0:1024], cbuf[:, :, 1024:1536], state_ssd[i],
                                     hp['cw'], hp['cb'], hp['dtb'], hp['alog'], hp['dexp'], hp['nw'])
            sb = state_shift[i]
            prev = [sb[:, 0:1024], sb[:, 1024:2048], sb[:, 2048:3072], sb[:, 3072:3328]]
            r, w, k, v, kn, ka, g = _rwkv_prep(u, prev, None, hp['mus'], hp['prep_w'], tm=tm, bpb=1,
                                               shifted=False)
            sh = lambda t: t.reshape(B, 1, 1024)
            o, s_wkv = _wkv_step(sh(r), sh(w), sh(k), sh(v), sh(kn), sh(ka), state_wkv[i], nb=2)
            y_rwkv = _rwkv_post(o.reshape(B, 1024), r, k, v, g, hp['lnw'], hp['lnb'], hp['rk'], tm)
            x = _linear(y_ssd.reshape(B, 1024), hp['w_out'][:1024], tm=tm, tn=512, epi='resgate', res=x,
                        gate=(mod, 2), second=(y_rwkv, hp['w_out'][1024:]))
            new['ssd'].append(s_ssd)
            new['conv'].append(jnp.concatenate([cbuf[:, 1:], _raw_conv_rows(u3)], axis=1))
            new['wkv'].append(s_wkv)
            new['shift'].append(_raw_rw_rows(u))
        else:
            ap = att[i]
            qkv = _linear(x, ap['w_qkv'], tm=tm, tn=3 * ATT_DIM, pro='normmod', epi='rope',
                          norm=(gmix, mod, 1, 0), rope=tabs)
            parts = qkv.reshape(B, len(ATT_GROUPS), 3, ATT_HEADS, ATT_HEAD_DIM)
            o = _attn_step(parts[:, :, 0], parts[:, :, 1], parts[:, :, 2], caches_t, i)
            x = _linear(o.reshape(B, ATT_DIM), ap['w_out'], tm=tm, tn=512, epi='resgate', res=x, gate=(mod, 2))
            for gi in range(len(ATT_GROUPS)):
                new['win'][gi].append(parts[:, gi, 1:3][:, None])
        gmlp = P['norm_mlp'][l][None, :]
        hid = _linear(x, P['w1'][l], tm=tm, tn=512, pro='normmod', epi='relu2', norm=(gmlp, mod, 4, 3),
                      out_dtype=BF16)
        last = l == DEPTH - 1
        x = _linear(hid, P['w2'][l], tm=tm, tn=1024 if last else 512, epi='resgate_norm' if last else 'resgate',
                    res=x, gate=(mod, 5), final_norm=norm_final[None, :] if last else None)
    y = x.reshape(B, 1, D_MODEL)
    return y, new


def kernel(x_prompt, x_sample, state_ssd, state_ssd_conv, state_wkv, state_wkv_shift, cache_win0, cache_win1, cache_win2, c_prompt, c_sample, norm_mix, norm_mlp, norm_final, ada_w, ada_b, mlp_w1, mlp_w2, hyb_w_in, hyb_w_out, ssd_conv_w, ssd_conv_b, ssd_dt_bias, ssd_a_log, ssd_d, ssd_norm_w, rwkv_mu, rwkv_w0, rwkv_w2, rwkv_a0, rwkv_a2, rwkv_g2, rwkv_k_k, rwkv_k_a, rwkv_r_k, rwkv_ln_w, rwkv_ln_b, att_w_qkv, att_w_out):
    Bp, L, _ = x_prompt.shape
    Bs = x_sample.shape[0]
    assert x_sample.shape[1] == 1

    nrow = Bp + Bs
    npad = -nrow % 16
    c_all = jnp.concatenate([c_prompt, c_sample, jnp.zeros((npad, D_MODEL), F32)], axis=0)
    mods_p, mods_s = [], []
    for l in range(DEPTH):
        mod = _linear(c_all, ada_w, w_layer=l, tm=nrow + npad, tn=512, pro='silu', epi='bias',
                      bias=ada_b[l][None, :])
        mods_p.append(mod[:Bp].reshape(Bp, 1, N_MOD * D_MODEL))
        mods_s.append(mod[Bp:nrow].reshape(1, Bs, N_MOD * D_MODEL))

    P = dict(norm_mix=norm_mix, norm_mlp=norm_mlp,
             w1=[mlp_w1[l].astype(BF16) for l in range(DEPTH)],
             w2=[mlp_w2[l].astype(BF16) for l in range(DEPTH)])
    hyb = [_hyb_params(i, hyb_w_in, hyb_w_out, ssd_conv_w, ssd_conv_b, ssd_dt_bias, ssd_a_log, ssd_d,
                       ssd_norm_w, rwkv_mu, rwkv_w0, rwkv_w2, rwkv_a0, rwkv_a2, rwkv_g2, rwkv_k_k,
                       rwkv_k_a, rwkv_r_k, rwkv_ln_w, rwkv_ln_b) for i in range(hyb_w_in.shape[0])]
    att = [dict(w_qkv=att_w_qkv[i].astype(BF16), w_out=att_w_out[i].astype(BF16))
           for i in range(att_w_qkv.shape[0])]

    y_p, new_p = _run_prompt(x_prompt.reshape(Bp * L, D_MODEL), mods_p, P, hyb, att, norm_final, Bp, L)
    y_s, new_s = _run_sample(x_sample.reshape(Bs, D_MODEL), mods_s, P, hyb, att, norm_final,
                             (state_ssd, state_ssd_conv, state_wkv, state_wkv_shift,
                              (cache_win0, cache_win1, cache_win2)), Bs)
    st = jnp.stack
    caches = (cache_win0, cache_win1, cache_win2)
    rolled = _roll_windows([_window_minor(c) for c in caches],
                           [_window_minor(st(new_s['win'][g])) for g in range(len(caches))])
    win_s = [jnp.transpose(t, (0, 1, 5, 2, 3, 4)) for t in rolled]
    return (y_p, y_s, st(new_p['ssd']), st(new_s['ssd']), st(new_p['conv']), st(new_s['conv']),
            st(new_p['wkv']), st(new_s['wkv']), st(new_p['shift']), st(new_s['shift']),
            st(new_p['win'][0]), win_s[0], st(new_p['win'][1]), win_s[1],
            st(new_p['win'][2]), win_s[2])
```

```python
import functools
import math

import numpy as np
import jax
import jax.numpy as jnp
from jax import lax
from jax.experimental import pallas as pl
from jax.experimental.pallas import tpu as pltpu

F32 = jnp.float32
BF16 = jnp.bfloat16
HIGHEST = lax.Precision.HIGHEST

D_MODEL = 1024
DEPTH = 4
PAST_LEN = 8192
NORM_EPS = 1e-6
N_MOD = 6
SSD_HEADS = 16
SSD_HEAD_DIM = 64
SSD_GROUPS = 2
SSD_STATE = 128
SSD_CONV = 4
SSD_CHUNK = 128
RWKV_HEADS = 16
RWKV_HEAD_DIM = 64
RWKV_LN_EPS = 64e-5
ATT_GROUPS = ((128, 1), (512, 4), (2048, 16))
ATT_HEADS = 8
ATT_HEAD_DIM = 64
ATT_DIM = ATT_HEADS * ATT_HEAD_DIM
ATT_Q_BLOCK = 128
ROPE_THETA = 500000.0
ROPE_DIM = ATT_HEAD_DIM // 4
MLP_HIDDEN = 4 * D_MODEL

U_COLS = 6144
U_Z, U_XS, U_R, U_K, U_V, U_BC, U_LW, U_DT = 0, 1024, 2048, 3072, 4096, 5120, 5632, 5888

LANES = 128
VMEM_LIMIT = 48 * 1024 * 1024
PROMPT_ROWS = 1024


def _cparams(*sem):
    return pltpu.CompilerParams(dimension_semantics=sem, vmem_limit_bytes=VMEM_LIMIT)


def _dot(a, b):
    return jnp.dot(a, b, preferred_element_type=F32)


def _dot_exact(a, b):
    return jnp.dot(a, b, preferred_element_type=F32, precision=HIGHEST)


def _dot_split(x, w2):
    hi = x.astype(BF16)
    lo = (x - hi.astype(F32)).astype(BF16)
    return _dot(jnp.concatenate([hi, lo], axis=1), w2)


def _twice(w):
    return jnp.concatenate([w, w], axis=0).astype(BF16)


def _dot_exact_nt(a, b):
    return lax.dot_general(a, b, (((1,), (1,)), ((), ())), preferred_element_type=F32, precision=HIGHEST)


def _dot_nt(a, b):
    return lax.dot_general(a, b, (((1,), (1,)), ((), ())), preferred_element_type=F32)


def _silu(x):
    return x * jax.nn.sigmoid(x)


def _softplus(x):
    return jnp.maximum(x, 0.0) + jnp.log1p(jnp.exp(-jnp.abs(x)))


def _block_ones():
    i = np.arange(LANES)
    return jnp.asarray((i[:, None] // 64 == i[None, :] // 64).astype(np.float32))


def _pair_eye():
    i = np.arange(64)
    j = np.arange(LANES)
    return jnp.asarray((i[:, None] == (j[None, :] % 64)).astype(np.float32))


def _head_expand(nheads, width):
    e = np.zeros((LANES, nheads * width), np.float32)
    for h in range(nheads):
        e[h, h * width:(h + 1) * width] = 1.0
    return jnp.asarray(e)


def _tril_ones(n):
    return jnp.asarray(np.tril(np.ones((n, n), np.float32)))


def _linear_kernel(*refs, pro, epi, two):
    refs = list(refs)
    x_ref = refs.pop(0)
    if pro == 'normmod':
        g_ref, sc_ref, sh_ref = refs.pop(0), refs.pop(0), refs.pop(0)
    w_ref = refs.pop(0)
    if two:
        x2_ref, w2_ref = refs.pop(0), refs.pop(0)
    if epi == 'bias':
        b_ref = refs.pop(0)
    if epi in ('resgate', 'resgate_norm'):
        res_ref, gate_ref = refs.pop(0), refs.pop(0)
    if epi == 'resgate_norm':
        gf_ref = refs.pop(0)
    if epi == 'rope':
        tab_refs = [refs.pop(0) for _ in range(3)]
    o_ref = refs.pop(0)

    if pro == 'cast':
        h = x_ref[...].astype(BF16)
    else:
        h_ref = refs.pop(0)

        @pl.when(pl.program_id(1) == 0)
        def _():
            x = x_ref[...].astype(F32)
            if pro == 'silu':
                hh = _silu(x)
            else:
                ms = jnp.mean(x * x, axis=-1, keepdims=True)
                y = (x * lax.rsqrt(ms + NORM_EPS)) * g_ref[...]
                hh = y * (1.0 + sc_ref[...]) + sh_ref[...]
            h_ref[...] = hh.astype(BF16)

        h = h_ref[...]
    if epi == 'rope':
        tabs = [t[...] for t in tab_refs]
        for c in range(3):
            cols = slice(c * ATT_DIM, (c + 1) * ATT_DIM)
            part = _dot(h, w_ref[:, cols])
            o_ref[:, cols] = _rope_apply(part, *tabs) if c < 2 else part
        return
    acc = _dot(h, w_ref[...].astype(BF16))
    if two:
        acc = acc + _dot(x2_ref[...].astype(BF16), w2_ref[...])
    if epi == 'bias':
        acc = acc + b_ref[...]
    elif epi == 'relu2':
        acc = jnp.square(jnp.maximum(acc, 0.0))
    elif epi in ('resgate', 'resgate_norm'):
        acc = res_ref[...] + gate_ref[...] * acc
        if epi == 'resgate_norm':
            ms = jnp.mean(acc * acc, axis=-1, keepdims=True)
            acc = (acc * lax.rsqrt(ms + NORM_EPS)) * gf_ref[...]
    o_ref[...] = acc.astype(o_ref.dtype)


def _linear(x, w, *, tm, tn, pro='cast', epi='none', norm=None, bias=None, res=None, gate=None,
            bpb=1, out_dtype=F32, second=None, rope=None, w_layer=0, final_norm=None):
    M, K = x.shape
    N = w.shape[-1]
    assert M % tm == 0 and N % tn == 0
    in_specs = [pl.BlockSpec((tm, K), lambda i, j: (i, 0))]
    args = [x]
    scratch = []
    if pro == 'normmod':
        g, mod, ksc, ksh = norm
        r = mod.shape[1]
        in_specs += [pl.BlockSpec((1, K), lambda i, j: (0, 0)),
                     pl.BlockSpec((None, r, K), lambda i, j: (i // bpb, 0, ksc)),
                     pl.BlockSpec((None, r, K), lambda i, j: (i // bpb, 0, ksh))]
        args += [g, mod, mod]
    if pro != 'cast':
        scratch = [pltpu.VMEM((tm, K), BF16)]
    if w.ndim == 3:
        in_specs.append(pl.BlockSpec((None, K, tn), lambda i, j: (w_layer, 0, j)))
    else:
        in_specs.append(pl.BlockSpec((K, tn), lambda i, j: (0, j)))
    args.append(w)
    if second is not None:
        x2, w2 = second
        K2 = x2.shape[1]
        in_specs += [pl.BlockSpec((tm, K2), lambda i, j: (i, 0)), pl.BlockSpec((K2, tn), lambda i, j: (0, j))]
        args += [x2, w2]
    if epi == 'bias':
        in_specs.append(pl.BlockSpec((1, tn), lambda i, j: (0, j)))
        args.append(bias)
    if epi in ('resgate', 'resgate_norm'):
        mod, kg = gate
        r = mod.shape[1]
        nj = N // tn
        in_specs += [pl.BlockSpec((tm, tn), lambda i, j: (i, j)),
                     pl.BlockSpec((None, r, tn), lambda i, j: (i // bpb, 0, kg * nj + j))]
        args += [res, mod]
    if epi == 'resgate_norm':
        assert tn == N
        in_specs.append(pl.BlockSpec((1, N), lambda i, j: (0, 0)))
        args.append(final_norm)
    if epi == 'rope':
        assert tn == 3 * ATT_DIM
        if rope[0].shape[0] == 1:
            in_specs += [pl.BlockSpec((1, LANES), lambda i, j: (0, 0))] * 3
        else:
            in_specs += [pl.BlockSpec((tm, LANES), lambda i, j: (i % bpb, 0))] * 3
        args += list(rope)
    return pl.pallas_call(
        functools.partial(_linear_kernel, pro=pro, epi=epi, two=second is not None),
        name="linear_%s_%s" % (pro, epi),
        grid=(M // tm, N // tn),
        in_specs=in_specs,
        out_specs=pl.BlockSpec((tm, tn), lambda i, j: (i, j)),
        out_shape=jax.ShapeDtypeStruct((M, N), out_dtype),
        scratch_shapes=scratch,
        compiler_params=_cparams("parallel", "arbitrary"),
    )(*args)


def _ssd_tail(y, xs, z, d_exp, norm_w):
    y = (y + d_exp * xs) * _silu(z)
    half = y.shape[1] // SSD_GROUPS
    outs = []
    for g in range(SSD_GROUPS):
        yg = y[:, g * half:(g + 1) * half]
        ms = jnp.mean(yg * yg, axis=-1, keepdims=True)
        outs.append(yg * lax.rsqrt(ms + NORM_EPS))
    return jnp.concatenate(outs, axis=1) * norm_w


def _ssd_prompt_kernel(z_ref, xs_ref, bc_ref, dt_ref, cw_ref, cb_ref, dtb_ref, alog_ref, dexp_ref,
                       nw_ref, tril_ref, e16_ref, y_ref, st_ref, extx, extbc, state, ybuf):
    c = pl.program_id(1)
    Q = SSD_CHUNK
    NX = SSD_HEADS * SSD_HEAD_DIM

    @pl.when(c == 0)
    def _():
        extx[0:8, :] = jnp.zeros((8, NX), F32)
        extbc[0:8, :] = jnp.zeros((8, 512), F32)
        state[...] = jnp.zeros_like(state)

    extx[8:8 + Q, :] = xs_ref[...]
    extbc[8:8 + Q, :] = bc_ref[...]
    cw = cw_ref[...]
    cb = cb_ref[...]
    xc = cb[:, 0:NX]
    bcc = cb[:, NX:NX + 512]
    for j in range(SSD_CONV):
        xc = xc + extx[pl.ds(5 + j, Q), :] * cw[j:j + 1, 0:NX]
        bcc = bcc + extbc[pl.ds(5 + j, Q), :] * cw[j:j + 1, NX:NX + 512]
    extx[0:8, :] = extx[Q:Q + 8, :]
    extbc[0:8, :] = extbc[Q:Q + 8, :]
    xs = _silu(xc)
    bcs = _silu(bcc)

    dt = _softplus(dt_ref[...] + dtb_ref[...])
    a_neg = -jnp.exp(alog_ref[...])
    acs = _dot_exact(tril_ref[...], dt * a_neg)
    acs_t = acs.T
    e16 = e16_ref[...]
    eacs = jnp.exp(acs)
    dt_exp = _dot_split(dt, e16)
    eacs_exp = _dot_split(eacs, e16)
    wend_exp = _dot_split(jnp.exp(acs[Q - 1:Q, :] - acs) * dt, e16)
    xdt = (xs * dt_exp).astype(BF16)
    xw = (xs * wend_exp).astype(BF16)
    row = lax.broadcasted_iota(jnp.int32, (Q, Q), 0)
    col = lax.broadcasted_iota(jnp.int32, (Q, Q), 1)
    causal = row >= col
    HG = SSD_HEADS // SSD_GROUPS
    GW = HG * SSD_HEAD_DIM
    for g in range(SSD_GROUPS):
        b_g = bcs[:, g * SSD_STATE:(g + 1) * SSD_STATE]
        c_g = bcs[:, 256 + g * SSD_STATE:256 + (g + 1) * SSD_STATE].astype(BF16)
        cb_g = _dot_nt(c_g, b_g.astype(BF16))
        bt_g = b_g.T.astype(BF16)
        for hg in range(HG):
            h = g * HG + hg
            seg = acs[:, h:h + 1] - acs_t[h:h + 1, :]
            decay = jnp.where(causal, jnp.exp(seg), 0.0)
            scores = (cb_g * decay).astype(BF16)
            ybuf[:, h * 64:(h + 1) * 64] = _dot(scores, xdt[:, h * 64:(h + 1) * 64])
        st_g = state[g]
        y_off = _dot(c_g, st_g.astype(BF16)) * eacs_exp[:, g * GW:(g + 1) * GW]
        ybuf[:, g * GW:(g + 1) * GW] = ybuf[:, g * GW:(g + 1) * GW] + y_off
        state[g] = st_g * eacs_exp[Q - 1:Q, g * GW:(g + 1) * GW] + _dot(bt_g, xw[:, g * GW:(g + 1) * GW])

    y_ref[...] = _ssd_tail(ybuf[...], xs, z_ref[...], dexp_ref[...], nw_ref[...]).astype(y_ref.dtype)

    @pl.when(c == pl.num_programs(1) - 1)
    def _():
        st_ref[...] = state[...]


def _ssd_prompt(u, B, L, cw, cb, dtb, alog, dexp, nw):
    Q = SSD_CHUNK
    nc = L // Q
    row = lambda b, c: b * nc + c
    const = lambda shape: pl.BlockSpec(shape, lambda b, c: (0,) * len(shape))
    y, st = pl.pallas_call(
        _ssd_prompt_kernel,
        name="ssd_prompt",
        grid=(B, nc),
        in_specs=[pl.BlockSpec((Q, 1024), lambda b, c: (row(b, c), U_Z // 1024)),
                  pl.BlockSpec((Q, 1024), lambda b, c: (row(b, c), U_XS // 1024)),
                  pl.BlockSpec((Q, 512), lambda b, c: (row(b, c), U_BC // 512)),
                  pl.BlockSpec((Q, 128), lambda b, c: (row(b, c), U_DT // 128)),
                  const((SSD_CONV, 1536)), const((1, 1536)), const((1, 128)), const((1, 128)),
                  const((1, 1024)), const((1, 1024)), const((Q, Q)), const((256, 1024))],
        out_specs=[pl.BlockSpec((Q, 1024), lambda b, c: (row(b, c), 0)),
                   pl.BlockSpec((None, SSD_GROUPS, SSD_STATE, 512), lambda b, c: (b, 0, 0, 0))],
        out_shape=[jax.ShapeDtypeStruct((B * L, 1024), BF16),
                   jax.ShapeDtypeStruct((B, SSD_GROUPS, SSD_STATE, 512), F32)],
        scratch_shapes=[pltpu.VMEM((Q + 8, 1024), F32), pltpu.VMEM((Q + 8, 512), F32),
                        pltpu.VMEM((SSD_GROUPS, SSD_STATE, 512), F32), pltpu.VMEM((Q, 1024), F32)],
        compiler_params=_cparams("parallel", "arbitrary"),
    )(u, u, u, u, cw, cb, dtb, alog, dexp, nw, _tril_ones(Q), _twice(_head_expand(SSD_HEADS, 64)))
    st = st.reshape(B, SSD_GROUPS, SSD_STATE, SSD_HEADS // SSD_GROUPS, SSD_HEAD_DIM)
    st = jnp.transpose(st, (0, 1, 3, 4, 2)).reshape(B, SSD_HEADS, SSD_HEAD_DIM, SSD_STATE)
    return y, st


def _ssd_step_kernel(z_ref, xs_ref, bc_ref, dt_ref, cx_ref, cbc_ref, s_ref, cw_ref, cb_ref, dtb_ref,
                     alog_ref, dexp_ref, nw_ref, e2_ref, y_ref, so_ref, ybuf):
    NX = SSD_HEADS * SSD_HEAD_DIM
    cw = cw_ref[...]
    cb = cb_ref[...]
    cx = cx_ref[...]
    cbc = cbc_ref[...]
    xc = cb[:, 0:NX] + xs_ref[...] * cw[3:4, 0:NX]
    bcc = cb[:, NX:NX + 512] + bc_ref[...] * cw[3:4, NX:NX + 512]
    for j in range(SSD_CONV - 1):
        xc = xc + cx[j:j + 1, :] * cw[j:j + 1, 0:NX]
        bcc = bcc + cbc[j:j + 1, :] * cw[j:j + 1, NX:NX + 512]
    xs = _silu(xc)
    bcs = _silu(bcc)
    dt = _softplus(dt_ref[...] + dtb_ref[...])
    da = jnp.exp(dt * (-jnp.exp(alog_ref[...])))
    e2 = e2_ref[...]
    lane = lax.broadcasted_iota(jnp.int32, (64, LANES), 1)
    first = lane < 64
    HG = SSD_HEADS // SSD_GROUPS
    heads = list(range(SSD_HEADS))
    diag = [e2 * xs[:, q * LANES:(q + 1) * LANES] for q in range(SSD_HEADS // 2)]
    xcol = [jnp.sum(jnp.where(first == (h % 2 == 0), diag[h // 2], 0.0), axis=1, keepdims=True)
            for h in heads]
    b_row = [bcs[:, (h // HG) * SSD_STATE:(h // HG + 1) * SSD_STATE] for h in heads]
    c_row = [bcs[:, 256 + (h // HG) * SSD_STATE:256 + (h // HG + 1) * SSD_STATE] for h in heads]
    s_new = [s_ref[h] * da[:, h:h + 1] + (xcol[h] * dt[:, h:h + 1]) * b_row[h] for h in heads]
    for h in heads:
        so_ref[h] = s_new[h]
    ycol = [jnp.sum(s_new[h] * c_row[h], axis=1, keepdims=True) for h in heads]
    for q in range(SSD_HEADS // 2):
        ypair = jnp.where(first, ycol[2 * q], ycol[2 * q + 1])
        ybuf[:, q * LANES:(q + 1) * LANES] = jnp.sum(e2 * ypair, axis=0, keepdims=True)
    y_ref[...] = _ssd_tail(ybuf[...], xs, z_ref[...], dexp_ref[...], nw_ref[...]).astype(y_ref.dtype)


def _ssd_step(u, conv_x, conv_bc, s0, cw, cb, dtb, alog, dexp, nw):
    B = u.shape[0]
    const = lambda shape: pl.BlockSpec(shape, lambda b: (0,) * len(shape))
    return pl.pallas_call(
        _ssd_step_kernel,
        name="ssd_step",
        grid=(B,),
        in_specs=[pl.BlockSpec((None, 1, 1024), lambda b: (b, 0, U_Z // 1024)),
                  pl.BlockSpec((None, 1, 1024), lambda b: (b, 0, U_XS // 1024)),
                  pl.BlockSpec((None, 1, 512), lambda b: (b, 0, U_BC // 512)),
                  pl.BlockSpec((None, 1, 128), lambda b: (b, 0, U_DT // 128)),
                  pl.BlockSpec((None, 3, 1024), lambda b: (b, 0, 0)),
                  pl.BlockSpec((None, 3, 512), lambda b: (b, 0, 0)),
                  pl.BlockSpec((None, SSD_HEADS, 64, 128), lambda b: (b, 0, 0, 0)),
                  const((SSD_CONV, 1536)), const((1, 1536)), const((1, 128)), const((1, 128)),
                  const((1, 1024)), const((1, 1024)), const((64, 128))],
        out_specs=[pl.BlockSpec((None, 1, 1024), lambda b: (b, 0, 0)),
                   pl.BlockSpec((None, SSD_HEADS, 64, 128), lambda b: (b, 0, 0, 0))],
        out_shape=[jax.ShapeDtypeStruct((B, 1, 1024), BF16),
                   jax.ShapeDtypeStruct(s0.shape, F32)],
        scratch_shapes=[pltpu.VMEM((1, 1024), F32)],
        compiler_params=_cparams("parallel"),
    )(u, u, u, u, conv_x, conv_bc, s0, cw, cb, dtb, alog, dexp, nw, _pair_eye())


def _rwkv_prep_kernel(*refs, shifted, bpb):
    refs = list(refs)
    cur = [refs.pop(0) for _ in range(4)]
    prev = [refs.pop(0) for _ in range(4)]
    if shifted:
        first = [refs.pop(0) for _ in range(4)]
    mu = [refs.pop(0) for _ in range(4)]
    (w0_ref, w2_ref, a0_ref, a2_ref, g2_ref, kk_ref, ka_ref, bo_ref) = [refs.pop(0) for _ in range(8)]
    (r_o, w_o, k_o, v_o, kn_o, kka_o, g_o) = refs
    i = pl.program_id(0)

    def mixed(n):
        x = cur[n][...]
        if shifted:
            rolled = pltpu.roll(x, 1, 0)
            before = jnp.where(i % bpb == 0, first[n][...], prev[n][7:8, :])
            rid = lax.broadcasted_iota(jnp.int32, x.shape, 0)
            p = jnp.where(rid == 0, before, rolled)
        else:
            p = prev[n][...]
        return x + (p - x) * mu[n][...]

    r, k, v, lw = mixed(0), mixed(1), mixed(2), mixed(3)
    blk = lw[:, 0:LANES]
    lane = lax.broadcasted_iota(jnp.int32, blk.shape, 1)
    tw = jnp.where(lane < 64, jnp.tanh(blk), blk).astype(BF16)
    wpre = w0_ref[...] + _dot(tw, w2_ref[...])
    apre = a0_ref[...] + _dot(tw, a2_ref[...])
    wlog = -_softplus(-wpre) - 0.5
    a = jax.nn.sigmoid(apre)
    g = _dot(jax.nn.sigmoid(lw[:, LANES:2 * LANES]).astype(BF16), g2_ref[...])
    kk = k * kk_ref[...]
    kk2 = kk * kk
    bo = bo_ref[...]
    for q in range(RWKV_HEADS // 2):
        sl = slice(q * LANES, (q + 1) * LANES)
        n2 = _dot_exact(kk2[:, sl], bo)
        kn = kk[:, sl] / jnp.maximum(jnp.sqrt(n2), 1e-12)
        kn_o[:, sl] = -kn
        kka_o[:, sl] = kn * a[:, sl]
    r_o[...] = r
    w_o[...] = -jnp.exp(wlog)
    k_o[...] = k * (1.0 + (a - 1.0) * ka_ref[...])
    v_o[...] = v
    g_o[...] = g


def _rwkv_prep(u, prev, first, mus, ws, *, tm, bpb, shifted):
    M = u.shape[0]
    cols = [(1024, U_R // 1024), (1024, U_K // 1024), (1024, U_V // 1024), (256, U_LW // 256)]
    in_specs = [pl.BlockSpec((tm, c), functools.partial(lambda i, kb: (i, kb), kb=kb)) for c, kb in cols]
    args = [u] * 4
    if shifted:
        in_specs += [pl.BlockSpec((8, c), functools.partial(
            lambda i, kb: (jnp.maximum(i * (tm // 8) - 1, 0), kb), kb=kb)) for c, kb in cols]
        args += [u] * 4
        in_specs += [pl.BlockSpec((None, 1, c), lambda i: (i // bpb, 0, 0)) for c, _ in cols]
        args += list(first)
    else:
        in_specs += [pl.BlockSpec((tm, c), lambda i: (i, 0)) for c, _ in cols]
        args += list(prev)
    in_specs += [pl.BlockSpec((1, c), lambda i: (0, 0)) for c, _ in cols]
    args += list(mus)
    wshapes = [(1, 1024), (128, 1024), (1, 1024), (128, 1024), (128, 1024), (1, 1024), (1, 1024), (128, 128)]
    in_specs += [pl.BlockSpec(s, lambda i: (0, 0)) for s in wshapes]
    args += list(ws)
    return pl.pallas_call(
        functools.partial(_rwkv_prep_kernel, shifted=shifted, bpb=bpb),
        name="rwkv_prep",
        grid=(M // tm,),
        in_specs=in_specs,
        out_specs=[pl.BlockSpec((tm, 1024), lambda i: (i, 0))] * 7,
        out_shape=[jax.ShapeDtypeStruct((M, 1024), F32)] * 7,
        compiler_params=_cparams("parallel"),
    )(*args)


def _wkv_step_kernel(r_ref, lw_ref, k_ref, v_ref, kn_ref, ka_ref, s0_ref, e2_ref, bo_ref, o_ref, sT_ref, *, nb):
    n = RWKV_HEAD_DIM
    e2 = e2_ref[...]
    bo = bo_ref[...]
    first = lax.broadcasted_iota(jnp.int32, (n, LANES), 1) < n

    def pair_sum(x):
        sa = jnp.sum(jnp.where(first, x, 0.0), axis=1, keepdims=True)
        sb = jnp.sum(jnp.where(first, 0.0, x), axis=1, keepdims=True)
        return jnp.where(first, sa, sb)

    for b in range(nb):
        for p in range(RWKV_HEADS // 2):
            sl = slice(p * LANES, (p + 1) * LANES)
            S = jnp.concatenate([s0_ref[b, 2 * p], s0_ref[b, 2 * p + 1]], axis=1)
            sa = pair_sum(S * kn_ref[b, :, sl])
            vcol = _dot_exact(e2 * v_ref[b, :, sl], bo)
            S = S * jnp.exp(lw_ref[b, :, sl]) + sa * ka_ref[b, :, sl] + vcol * k_ref[b, :, sl]
            sT_ref[b, 2 * p] = S[:, 0:n]
            sT_ref[b, 2 * p + 1] = S[:, n:2 * n]
            o = pair_sum(S * r_ref[b, :, sl])
            o_ref[b, :, sl] = jnp.sum(e2 * o, axis=0, keepdims=True)


def _wkv_step(r, lw, k, v, kn, ka, s0, *, nb):
    B = r.shape[0]
    seq = pl.BlockSpec((nb, 1, 1024), lambda b: (b, 0, 0))
    stt = pl.BlockSpec((nb,) + s0.shape[1:], lambda b: (b, 0, 0, 0))
    return pl.pallas_call(
        functools.partial(_wkv_step_kernel, nb=nb),
        grid=(B // nb,),
        in_specs=[seq] * 6 + [stt, pl.BlockSpec((64, 128), lambda b: (0, 0)),
                              pl.BlockSpec((128, 128), lambda b: (0, 0))],
        out_specs=[seq, stt],
        out_shape=[jax.ShapeDtypeStruct((B, 1, 1024), F32), jax.ShapeDtypeStruct(s0.shape, F32)],
        compiler_params=_cparams("parallel"),
        name="wkv_step",
    )(r, lw, k, v, kn, ka, s0, _pair_eye(), _block_ones())


WKV_CHUNK = 64


def _wkv_chunk_kernel(r_ref, lw_ref, k_ref, v_ref, kn_ref, ka_ref, tril_ref, o_ref, sT_ref, S_ref, *, nb):
    C = WKV_CHUNK
    c = pl.program_id(1)

    @pl.when(c == 0)
    def _():
        S_ref[...] = jnp.zeros_like(S_ref)

    tril = tril_ref[...]
    lane = lax.broadcasted_iota(jnp.int32, (C, LANES), 1)
    rowi = lax.broadcasted_iota(jnp.int32, (C, LANES), 0)
    first = lane < 64
    strict = rowi > (lane % 64)
    incl = rowi >= (lane % 64)
    r128 = lax.broadcasted_iota(jnp.int32, (LANES, LANES), 0)
    c128 = lax.broadcasted_iota(jnp.int32, (LANES, LANES), 1)
    diag_blocks = (r128 < 64) == (c128 < 64)
    eye = r128 == c128

    def bd(x):
        return jnp.concatenate([jnp.where(first, x, 0.0), jnp.where(first, 0.0, x)], axis=0)

    bf = lambda x: x.astype(BF16)
    pairs = [(b, p) for b in range(nb) for p in range(RWKV_HEADS // 2)]
    sls = [slice(p * LANES, (p + 1) * LANES) for _, p in pairs]
    load = lambda ref: [ref[b, :, sl] for (b, _), sl in zip(pairs, sls)]
    each = lambda f, *ls: [f(*a) for a in zip(*ls)]
    r_, lw, kt, vv, al, be = (load(ref) for ref in (r_ref, lw_ref, k_ref, v_ref, kn_ref, ka_ref))
    def cumulative(x):
        hi = x.astype(BF16)
        lo = (x - hi.astype(F32)).astype(BF16)
        return _dot(tril, jnp.concatenate([hi, lo], axis=0))

    cs = each(cumulative, lw)
    last = each(lambda x: x[C - 1:C, :], cs)
    e_inv = each(lambda x: jnp.exp(-x), cs)
    aq = each(lambda a, x, l: a * jnp.exp(x - l), al, cs, lw)
    rq = each(lambda a, x: a * jnp.exp(x), r_, cs)
    bk = each(jnp.multiply, be, e_inv)
    kk = each(jnp.multiply, kt, e_inv)
    g = each(lambda a, q, b_, k_: _dot_exact_nt(
        jnp.concatenate([a, q], axis=0),
        jnp.concatenate([jnp.where(first, b_, 0.0), jnp.where(first, 0.0, b_),
                         jnp.where(first, k_, 0.0), jnp.where(first, 0.0, k_)], axis=0)), aq, rq, bk, kk)
    m1 = each(lambda x: bf(jnp.where(strict, x[0:C, 0:LANES], 0.0)), g)
    m2 = each(lambda x: bf(jnp.where(strict, x[0:C, LANES:2 * LANES], 0.0)), g)
    n1 = each(lambda x: bf(jnp.where(incl, x[C:2 * C, 0:LANES], 0.0)), g)
    n2 = each(lambda x: bf(jnp.where(incl, x[C:2 * C, LANES:2 * LANES], 0.0)), g)
    s0 = [S_ref[b, p] for b, p in pairs]
    s0b = each(bf, s0)
    vbd = each(lambda x: bf(bd(x)), vv)
    x = each(lambda a, s, m, v_: _dot(bf(a), s) + _dot(m, v_), aq, s0b, m2, vbd)
    mp = m1
    steps = int(math.log2(C))
    for i in range(steps):
        x = each(lambda x_, m: x_ + _dot(m, bf(bd(x_))), x, mp)
        if i + 1 < steps:
            mp = each(lambda m: bf(_dot(m, bd(m))), mp)
    o = each(lambda q, s, a, x_, b_, v_: _dot(bf(q), s) + _dot(a, bf(bd(x_))) + _dot(b_, v_),
             rq, s0b, n1, x, n2, vbd)
    for (b, _), sl, o_ in zip(pairs, sls, o):
        o_ref[b, :, sl] = o_
    e_end = each(lambda l, x_: jnp.exp(l - x_), last, cs)
    kv_t = each(lambda b_, k_, e: jnp.concatenate([b_ * e, k_ * e], axis=0).T, be, kt, e_end)
    upd = each(lambda t, x_, v_: _dot(bf(t), bf(jnp.concatenate([x_, v_], axis=0))), kv_t, x, vv)
    gcol = each(lambda l: jnp.sum(jnp.where(eye, jnp.exp(l), 0.0), axis=1, keepdims=True), last)
    for (b, p), s, u, gc in zip(pairs, s0, upd, gcol):
        S_ref[b, p] = jnp.where(diag_blocks, gc * s + u, 0.0)

    @pl.when(c == pl.num_programs(1) - 1)
    def _():
        sT_ref[...] = S_ref[...]


def _wkv_chunked(r, lw, k, v, kn, ka, *, nb):
    B, L, _ = r.shape
    C = WKV_CHUNK
    seq = pl.BlockSpec((nb, C, 1024), lambda b, c: (b, c, 0))
    stt = pl.BlockSpec((nb, 8, LANES, LANES), lambda b, c: (b, 0, 0, 0))
    o, sT = pl.pallas_call(
        functools.partial(_wkv_chunk_kernel, nb=nb),
        grid=(B // nb, L // C),
        in_specs=[seq] * 6 + [pl.BlockSpec((C, 2 * C), lambda b, c: (0, 0))],
        out_specs=[seq, stt],
        out_shape=[jax.ShapeDtypeStruct((B, L, 1024), F32), jax.ShapeDtypeStruct((B, 8, LANES, LANES), F32)],
        scratch_shapes=[pltpu.VMEM((nb, 8, LANES, LANES), F32)],
        compiler_params=_cparams("parallel", "arbitrary"),
        name="wkv_chunked",
    )(r, lw, k, v, kn, ka, jnp.concatenate([_tril_ones(C)] * 2, axis=1).astype(BF16))
    blocks = jnp.stack([sT[:, :, 0:64, 0:64], sT[:, :, 64:128, 64:128]], axis=2)
    return o, jnp.swapaxes(blocks, -1, -2).reshape(B, 16, 64, 64)


def _rwkv_post_kernel(o_ref, r_ref, k_ref, v_ref, g_ref, lnw_ref, lnb_ref, rk_ref, bo_ref, y_ref):
    bo = bo_ref[...]
    inv = 1.0 / RWKV_HEAD_DIM
    for q in range(RWKV_HEADS // 2):
        sl = slice(q * LANES, (q + 1) * LANES)
        o = o_ref[:, sl]
        mean = _dot_split(o, bo) * inv
        d = o - mean
        var = _dot_split(d * d, bo) * inv
        on = d * lax.rsqrt(var + RWKV_LN_EPS) * lnw_ref[:, sl] + lnb_ref[:, sl]
        bonus = _dot_split(r_ref[:, sl] * k_ref[:, sl] * rk_ref[:, sl], bo) * v_ref[:, sl]
        y_ref[:, sl] = ((on + bonus) * g_ref[:, sl]).astype(y_ref.dtype)


def _rwkv_post(o, r, k, v, g, lnw, lnb, rk, tm):
    M = o.shape[0]
    blk = pl.BlockSpec((tm, 1024), lambda i: (i, 0))
    vec = pl.BlockSpec((1, 1024), lambda i: (0, 0))
    return pl.pallas_call(
        _rwkv_post_kernel,
        name="rwkv_post",
        grid=(M // tm,),
        in_specs=[blk] * 5 + [vec] * 3 + [pl.BlockSpec((256, 128), lambda i: (0, 0))],
        out_specs=blk,
        out_shape=jax.ShapeDtypeStruct((M, 1024), BF16),
        compiler_params=_cparams("parallel"),
    )(o, r, k, v, g, lnw, lnb, rk, _twice(_block_ones()))


def _rope_tables(pos):
    half = ROPE_DIM // 2
    inv = ROPE_THETA ** (-jnp.arange(half, dtype=F32) * 2.0 / ROPE_DIM)
    ang = pos.astype(F32)[:, None] * inv
    cos, sin = jnp.cos(ang), jnp.sin(ang)
    n = pos.shape[0]
    rest = ATT_HEAD_DIM - ROPE_DIM
    c = jnp.concatenate([cos, cos, jnp.ones((n, rest), F32)], axis=1)
    s_next = jnp.concatenate([-sin, jnp.zeros((n, half + rest), F32)], axis=1)
    s_prev = jnp.concatenate([jnp.zeros((n, half), F32), sin, jnp.zeros((n, rest), F32)], axis=1)
    return tuple(jnp.concatenate([t, t], axis=1) for t in (c, s_next, s_prev))


def _rope_apply(x, c, s_next, s_prev):
    n = x.shape[1]
    reps = n // LANES
    tile = lambda t: jnp.concatenate([t] * reps, axis=1)
    half = ROPE_DIM // 2
    return x * tile(c) + pltpu.roll(x, n - half, 1) * tile(s_next) + pltpu.roll(x, half, 1) * tile(s_prev)


ATT_BLOCK_ROWS = ATT_Q_BLOCK * max(d for _, d in ATT_GROUPS)
ATT_BATCH = 4


def _attn_prompt_kernel(q_ref, kc_ref, kp_ref, vc_ref, vp_ref, o_ref, lse_ref, *, d):
    i = pl.program_id(0)
    QB = ATT_Q_BLOCK
    R = q_ref.shape[0]
    row = lax.broadcasted_iota(jnp.int32, (QB, 2 * QB), 0)
    col = lax.broadcasted_iota(jnp.int32, (QB, 2 * QB), 1)
    band = (col >= row) & (col <= row + QB)
    band_first = band & ((i > 0) | (col >= QB))
    first = lax.broadcasted_iota(jnp.int32, (QB, LANES), 1) < ATT_HEAD_DIM
    first_kv = lax.broadcasted_iota(jnp.int32, (2 * QB, LANES), 1) < ATT_HEAD_DIM
    scale = ATT_HEAD_DIM ** -0.5
    span = QB * d
    blocks = [(rho, j) for rho in range(d) for j in range(R // span)]
    each = lambda f, *ls: [f(*a) for a in zip(*ls)]
    for b0 in range(0, len(blocks), ATT_BATCH):
        batch = blocks[b0:b0 + ATT_BATCH]
        ds = lambda start, n: pl.ds(start, n, stride=d) if d > 1 else pl.ds(start, n)
        qrows = [ds(rho + span * j, QB) for rho, j in batch]
        valid = [band_first if j == 0 else band for _, j in batch]

        def keys(cur_ref, prev_ref, rho, j):
            if j > 0:
                return cur_ref[ds(rho + span * (j - 1), 2 * QB), :]
            return jnp.concatenate([prev_ref[ds(R - span + rho, QB), :], cur_ref[ds(rho, QB), :]], axis=0)

        qb = [q_ref[r, :].astype(BF16) for r in qrows]
        k2 = [keys(kc_ref, kp_ref, rho, j).astype(BF16) for rho, j in batch]
        v2 = [keys(vc_ref, vp_ref, rho, j).astype(BF16) for rho, j in batch]
        halves = [slice(0, ATT_HEAD_DIM), slice(ATT_HEAD_DIM, LANES)]
        s = [[jnp.where(vm, _dot_nt(q[:, sl], k[:, sl]) * scale, -jnp.inf) for sl in halves]
             for q, k, vm in zip(qb, k2, valid)]
        m = [[jnp.max(jnp.maximum(x[:, 0:QB], x[:, QB:2 * QB]), axis=1, keepdims=True) for x in pair] for pair in s]
        p = [[jnp.exp(x - mx).astype(BF16) for x, mx in zip(ps, ms)] for ps, ms in zip(s, m)]
        va = each(lambda v: jnp.where(first_kv, v, 1.0), v2)
        vb = each(lambda v: jnp.where(first_kv, 1.0, v), v2)
        ea = each(lambda pp, v: _dot(pp[0], v), p, va)
        eb = each(lambda pp, v: _dot(pp[1], v), p, vb)
        num = each(lambda a, b_: jnp.where(first, a, b_), ea, eb)
        den = each(lambda a, b_: pltpu.roll(jnp.where(first, b_, a), ATT_HEAD_DIM, 1), ea, eb)
        for r, n_, d_, mm in zip(qrows, num, den, m):
            o_ref[r, :] = n_ / d_
            lse_ref[r, :] = jnp.where(first, mm[0], mm[1]) + jnp.log(d_)


def _attn_prompt(qkv, B, L, gi):
    window, d = ATT_GROUPS[gi]
    R = min(ATT_BLOCK_ROWS, L)
    assert window == ATT_Q_BLOCK * d and R % (ATT_Q_BLOCK * d) == 0 and L % R == 0
    nblk = L // R
    npair = ATT_DIM // LANES
    prev = lambda i: jnp.maximum(i - 1, 0)

    def col(which, f):
        return lambda i, b, hp: (b * nblk + f(i), gi * 3 * npair + which * npair + hp)

    same = lambda i: i
    blk = lambda f: pl.BlockSpec((R, LANES), f)
    out = pl.BlockSpec((R, LANES), lambda i, b, hp: (b * nblk + i, hp))
    return pl.pallas_call(
        functools.partial(_attn_prompt_kernel, d=d),
        grid=(nblk, B, npair),
        in_specs=[blk(col(0, same)), blk(col(1, same)), blk(col(1, prev)), blk(col(2, same)), blk(col(2, prev))],
        out_specs=[out, out],
        out_shape=[jax.ShapeDtypeStruct((B * L, ATT_DIM), F32)] * 2,
        compiler_params=_cparams("arbitrary", "arbitrary", "arbitrary"),
        name="attn_prompt_d%d" % d,
    )(qkv, qkv, qkv, qkv, qkv)


def _attn_out_kernel(o0, l0, o1, l1, o2, l2, w_ref, res_ref, gate_ref, out_ref, h_ref):
    @pl.when(pl.program_id(1) == 0)
    def _():
        m = jnp.maximum(jnp.maximum(l0[...], l1[...]), l2[...])
        a0, a1, a2 = jnp.exp(l0[...] - m), jnp.exp(l1[...] - m), jnp.exp(l2[...] - m)
        o = (a0 * o0[...] + a1 * o1[...] + a2 * o2[...]) / (a0 + a1 + a2)
        h_ref[...] = o.astype(BF16)

    out_ref[...] = res_ref[...] + gate_ref[...] * _dot(h_ref[...], w_ref[...])


def _attn_out(ols, w, res, mod, kg, *, tm, tn, bpb):
    M = res.shape[0]
    N = w.shape[1]
    r = mod.shape[1]
    nj = N // tn
    part = pl.BlockSpec((tm, 512), lambda i, j: (i, 0))
    return pl.pallas_call(
        _attn_out_kernel,
        name="attn_out",
        grid=(M // tm, nj),
        in_specs=[part] * 6 + [pl.BlockSpec((512, tn), lambda i, j: (0, j)),
                               pl.BlockSpec((tm, tn), lambda i, j: (i, j)),
                               pl.BlockSpec((None, r, tn), lambda i, j: (i // bpb, 0, kg * nj + j))],
        out_specs=pl.BlockSpec((tm, tn), lambda i, j: (i, j)),
        out_shape=jax.ShapeDtypeStruct((M, N), F32),
        scratch_shapes=[pltpu.VMEM((tm, 512), BF16)],
        compiler_params=_cparams("parallel", "arbitrary"),
    )(*ols, w, res, mod)


def _attn_step_kernel(q_ref, kn_ref, vn_ref, c0_ref, c1_ref, c2_ref, o_ref):
    caches = (c0_ref, c1_ref, c2_ref)
    scale = ATT_HEAD_DIM ** -0.5
    hl = lax.broadcasted_iota(jnp.int32, (ATT_HEAD_DIM, ATT_HEADS), 1)
    ng = len(ATT_GROUPS)
    combos = [(h, gi) for h in range(ATT_HEADS) for gi in range(ng)]
    each = lambda f, *ls: [f(*a) for a in zip(*ls)]
    col = lambda ref: [ref[gi][:, h:h + 1] for h, gi in combos]
    q, kn, vn = col(q_ref), col(kn_ref), col(vn_ref)
    vis = [lax.broadcasted_iota(jnp.int32, (1, w), 1) % d == 0 for w, d in ATT_GROUPS]
    s = [jnp.where(vis[gi], jnp.sum(caches[gi][0, h] * q_, axis=0, keepdims=True) * scale, -jnp.inf)
         for (h, gi), q_ in zip(combos, q)]
    s_new = each(lambda a, b: jnp.sum(a * b, axis=0, keepdims=True) * scale, q, kn)
    m = each(lambda a, b: jnp.maximum(jnp.max(a, axis=1, keepdims=True), b), s, s_new)
    p = each(lambda a, b: jnp.exp(a - b), s, m)
    p_new = each(lambda a, b: jnp.exp(a - b), s_new, m)
    l = each(lambda a, b: jnp.sum(a, axis=1, keepdims=True) + b, p, p_new)
    acc = [jnp.sum(caches[gi][1, h] * p_, axis=1, keepdims=True) + pn * v_
           for (h, gi), p_, pn, v_ in zip(combos, p, p_new, vn)]
    og = each(lambda a, b: a / b, acc, l)
    lse = each(lambda a, b: a + jnp.log(b), m, l)
    out = jnp.zeros((ATT_HEAD_DIM, ATT_HEADS), F32)
    for h in range(ATT_HEADS):
        os_, ls_ = og[h * ng:(h + 1) * ng], lse[h * ng:(h + 1) * ng]
        mm = jnp.maximum(jnp.maximum(ls_[0], ls_[1]), ls_[2])
        ws = [jnp.exp(x - mm) for x in ls_]
        o = (ws[0] * os_[0] + ws[1] * os_[1] + ws[2] * os_[2]) / (ws[0] + ws[1] + ws[2])
        out = jnp.where(hl == h, o, out)
    o_ref[...] = out


def _window_minor(c):
    return jnp.transpose(c, (0, 1, 3, 4, 5, 2))


def _attn_step(q3, kn3, vn3, caches_t, layer):
    B = q3.shape[0]
    for (window, d), c in zip(ATT_GROUPS, caches_t):
        assert c.shape[-1] == window and window // d == 128
    cols = lambda t: jnp.swapaxes(t, -1, -2)
    specs = [pl.BlockSpec((None, None) + c.shape[2:], lambda b: (layer, b, 0, 0, 0, 0)) for c in caches_t]
    new = pl.BlockSpec((None, len(ATT_GROUPS), ATT_HEAD_DIM, ATT_HEADS), lambda b: (b, 0, 0, 0))
    o = pl.pallas_call(
        _attn_step_kernel,
        grid=(B,),
        in_specs=[new] * 3 + specs,
        out_specs=pl.BlockSpec((None, ATT_HEAD_DIM, ATT_HEADS), lambda b: (b, 0, 0)),
        out_shape=jax.ShapeDtypeStruct((B, ATT_HEAD_DIM, ATT_HEADS), F32),
        compiler_params=_cparams("parallel"),
        name="attn_step",
    )(cols(q3), cols(kn3), cols(vn3), *caches_t)
    return jnp.swapaxes(o, -1, -2)


def _roll_kernel(c0, c1, c2, n0, n1, n2, o0, o1, o2):
    for c_ref, n_ref, o_ref in ((c0, n0, o0), (c1, n1, o1), (c2, n2, o2)):
        w = c_ref.shape[-1]
        rows = c_ref.shape[0] * c_ref.shape[1] * c_ref.shape[2]
        x = c_ref[...].reshape(rows, w)
        lane = lax.broadcasted_iota(jnp.int32, (rows, w), 1)
        y = jnp.where(lane == w - 1, n_ref[...].reshape(rows, 1), pltpu.roll(x, w - 1, 1))
        o_ref[...] = y.reshape(o_ref.shape)


def _roll_windows(caches_t, rows_t):
    NC, B = caches_t[0].shape[:2]
    spec = lambda a: pl.BlockSpec((None, None) + a.shape[2:], lambda i, b: (i, b, 0, 0, 0, 0))
    return pl.pallas_call(
        _roll_kernel,
        grid=(NC, B),
        in_specs=[spec(c) for c in caches_t] + [spec(r) for r in rows_t],
        out_specs=[spec(c) for c in caches_t],
        out_shape=[jax.ShapeDtypeStruct(c.shape, c.dtype) for c in caches_t],
        compiler_params=_cparams("parallel", "parallel"),
        name="roll_windows",
    )(*caches_t, *rows_t)


def _hyb_params(i, hyb_w_in, hyb_w_out, ssd_conv_w, ssd_conv_b, ssd_dt_bias, ssd_a_log, ssd_d, ssd_norm_w,
                rwkv_mu, rwkv_w0, rwkv_w2, rwkv_a0, rwkv_a2, rwkv_g2, rwkv_k_k, rwkv_k_a, rwkv_r_k,
                rwkv_ln_w, rwkv_ln_b):
    w = hyb_w_in[i]
    rw0 = 2576
    w_perm = jnp.concatenate(
        [w[:, 0:1024], w[:, 1024:2048], w[:, rw0:rw0 + 3072], w[:, 2048:2560], w[:, rw0 + 3072:rw0 + 3328],
         w[:, 2560:2576], jnp.zeros((D_MODEL, U_COLS - U_DT - 16), F32)], axis=1).astype(BF16)
    pad128 = lambda v: jnp.concatenate([v, jnp.zeros((LANES - v.shape[0],), F32)])[None, :]
    z64 = jnp.zeros((64, 1024), F32)
    mu = rwkv_mu[i]
    p = dict(
        w_in=w_perm, w_out=hyb_w_out[i].astype(BF16),
        cw=ssd_conv_w[i], cb=ssd_conv_b[i][None, :], dtb=pad128(ssd_dt_bias[i]), alog=pad128(ssd_a_log[i]),
        dexp=jnp.repeat(ssd_d[i], SSD_HEAD_DIM)[None, :], nw=ssd_norm_w[i][None, :],
        mus=[mu[None, 0:1024], mu[None, 1024:2048], mu[None, 2048:3072], mu[None, 3072:3328]],
        prep_w=[rwkv_w0[i][None, :], jnp.concatenate([rwkv_w2[i], z64]).astype(BF16),
                rwkv_a0[i][None, :], jnp.concatenate([z64, rwkv_a2[i]]).astype(BF16),
                rwkv_g2[i].astype(BF16), rwkv_k_k[i][None, :], rwkv_k_a[i][None, :], _block_ones()],
        lnw=rwkv_ln_w[i][None, :], lnb=rwkv_ln_b[i][None, :], rk=rwkv_r_k[i].reshape(1, 1024),
    )
    return p


def _raw_conv_rows(u_rows):
    return jnp.concatenate([u_rows[..., U_XS:U_XS + 1024], u_rows[..., U_BC:U_BC + 512]], axis=-1)


def _raw_rw_rows(u_rows):
    return jnp.concatenate([u_rows[..., U_R:U_R + 3072], u_rows[..., U_LW:U_LW + 256]], axis=-1)


def _run_prompt(x, mods, P, hyb, att, norm_final, B, L):
    T = B * L
    big, half = PROMPT_ROWS, PROMPT_ROWS // 2
    lin = lambda *a, tm, **kw: _linear(*a, tm=tm, bpb=L // tm, **kw)
    new = dict(ssd=[], conv=[], wkv=[], shift=[], win=[[], [], []])
    tabs = _rope_tables(jnp.arange(L))
    for l in range(DEPTH):
        mod = mods[l]
        i = l // 2
        gmix = P['norm_mix'][l][None, :]
        if l % 2 == 0:
            hp = hyb[i]
            u = lin(x, hp['w_in'], tm=big, tn=2048, pro='normmod', norm=(gmix, mod, 1, 0))
            y_ssd, s_ssd = _ssd_prompt(u, B, L, hp['cw'], hp['cb'], hp['dtb'], hp['alog'], hp['dexp'], hp['nw'])
            zeros = [jnp.zeros((B, 1, c), F32) for c in (1024, 1024, 1024, 256)]
            r, w, k, v, kn, ka, g = _rwkv_prep(u, None, zeros, hp['mus'], hp['prep_w'], tm=256, bpb=L // 256,
                                               shifted=True)
            sh = lambda t: t.reshape(B, L, 1024)
            o, s_wkv = _wkv_chunked(sh(r), sh(w), sh(k), sh(v), sh(kn), sh(ka), nb=B)
            y_rwkv = _rwkv_post(o.reshape(T, 1024), r, k, v, g, hp['lnw'], hp['lnb'], hp['rk'], 256)
            x = lin(y_ssd, hp['w_out'][:1024], tm=big, tn=1024, epi='resgate', res=x, gate=(mod, 2),
                    second=(y_rwkv, hp['w_out'][1024:]))
            u3 = u.reshape(B, L, U_COLS)
            new['ssd'].append(s_ssd)
            new['conv'].append(_raw_conv_rows(u3[:, L - (SSD_CONV - 1):]))
            new['wkv'].append(s_wkv)
            new['shift'].append(_raw_rw_rows(u3[:, L - 1]))
        else:
            ap = att[i]
            qkv = lin(x, ap['w_qkv'], tm=big, tn=3 * ATT_DIM, pro='normmod', epi='rope',
                      norm=(gmix, mod, 1, 0), rope=tabs)
            ols = []
            q3 = qkv.reshape(B, L, len(ATT_GROUPS) * 3 * ATT_DIM)
            for gi, (window, d) in enumerate(ATT_GROUPS):
                ols += _attn_prompt(qkv, B, L, gi)
                keep = min(window, L)
                k0 = (gi * 3 + 1) * ATT_DIM
                kv = q3[:, L - keep:, k0:k0 + 2 * ATT_DIM]
                new['win'][gi].append(kv.reshape(B, keep, 2, ATT_HEADS, ATT_HEAD_DIM))
            x = _attn_out(ols, ap['w_out'], x, mod, 2, tm=half, tn=1024, bpb=L // half)
        gmlp = P['norm_mlp'][l][None, :]
        hid = lin(x, P['w1'][l], tm=big, tn=2048, pro='normmod', epi='relu2', norm=(gmlp, mod, 4, 3),
                  out_dtype=BF16)
        last = l == DEPTH - 1
        x = lin(hid, P['w2'][l], tm=half, tn=1024, epi='resgate_norm' if last else 'resgate', res=x,
                gate=(mod, 5), final_norm=norm_final[None, :] if last else None)
    y = x.reshape(B, L, D_MODEL)
    return y, new


def _run_sample(x, mods, P, hyb, att, norm_final, states, B):
    state_ssd, state_conv, state_wkv, state_shift, caches = states
    tm = B
    new = dict(ssd=[], conv=[], wkv=[], shift=[], win=[[], [], []])
    tabs = _rope_tables(jnp.full((1,), PAST_LEN, jnp.int32))
    caches_t = [_window_minor(c) for c in caches]
    for l in range(DEPTH):
        mod = mods[l]
        i = l // 2
        gmix = P['norm_mix'][l][None, :]
        if l % 2 == 0:
            hp = hyb[i]
            u = _linear(x, hp['w_in'], tm=tm, tn=512, pro='normmod', norm=(gmix, mod, 1, 0))
            u3 = u.reshape(B, 1, U_COLS)
            cbuf = state_conv[i]
            y_ssd, s_ssd = _ssd_step(u3, cbuf[:, :, 0:1024], cbuf[:, :, 1024:1536], state_ssd[i],
                                     hp['cw'], hp['cb'], hp['dtb'], hp['alog'], hp['dexp'], hp['nw'])
            sb = state_shift[i]
            prev = [sb[:, 0:1024], sb[:, 1024:2048], sb[:, 2048:3072], sb[:, 3072:3328]]
            r, w, k, v, kn, ka, g = _rwkv_prep(u, prev, None, hp['mus'], hp['prep_w'], tm=tm, bpb=1,
                                               shifted=False)
            sh = lambda t: t.reshape(B, 1, 1024)
            o, s_wkv = _wkv_step(sh(r), sh(w), sh(k), sh(v), sh(kn), sh(ka), state_wkv[i], nb=2)
            y_rwkv = _rwkv_post(o.reshape(B, 1024), r, k, v, g, hp['lnw'], hp['lnb'], hp['rk'], tm)
            x = _linear(y_ssd.reshape(B, 1024), hp['w_out'][:1024], tm=tm, tn=512, epi='resgate', res=x,
                        gate=(mod, 2), second=(y_rwkv, hp['w_out'][1024:]))
            new['ssd'].append(s_ssd)
            new['conv'].append(jnp.concatenate([cbuf[:, 1:], _raw_conv_rows(u3)], axis=1))
            new['wkv'].append(s_wkv)
            new['shift'].append(_raw_rw_rows(u))
        else:
            ap = att[i]
            qkv = _linear(x, ap['w_qkv'], tm=tm, tn=3 * ATT_DIM, pro='normmod', epi='rope',
                          norm=(gmix, mod, 1, 0), rope=tabs)
            parts = qkv.reshape(B, len(ATT_GROUPS), 3, ATT_HEADS, ATT_HEAD_DIM)
            o = _attn_step(parts[:, :, 0], parts[:, :, 1], parts[:, :, 2], caches_t, i)
            x = _linear(o.reshape(B, ATT_DIM), ap['w_out'], tm=tm, tn=512, epi='resgate', res=x, gate=(mod, 2))
            for gi in range(len(ATT_GROUPS)):
                new['win'][gi].append(parts[:, gi, 1:3][:, None])
        gmlp = P['norm_mlp'][l][None, :]
        hid = _linear(x, P['w1'][l], tm=tm, tn=512, pro='normmod', epi='relu2', norm=(gmlp, mod, 4, 3),
                      out_dtype=BF16)
        last = l == DEPTH - 1
        x = _linear(hid, P['w2'][l], tm=tm, tn=1024 if last else 512, epi='resgate_norm' if last else 'resgate',
                    res=x, gate=(mod, 5), final_norm=norm_final[None, :] if last else None)
    y = x.reshape(B, 1, D_MODEL)
    return y, new


def kernel(x_prompt, x_sample, state_ssd, state_ssd_conv, state_wkv, state_wkv_shift, cache_win0, cache_win1, cache_win2, c_prompt, c_sample, norm_mix, norm_mlp, norm_final, ada_w, ada_b, mlp_w1, mlp_w2, hyb_w_in, hyb_w_out, ssd_conv_w, ssd_conv_b, ssd_dt_bias, ssd_a_log, ssd_d, ssd_norm_w, rwkv_mu, rwkv_w0, rwkv_w2, rwkv_a0, rwkv_a2, rwkv_g2, rwkv_k_k, rwkv_k_a, rwkv_r_k, rwkv_ln_w, rwkv_ln_b, att_w_qkv, att_w_out):
    Bp, L, _ = x_prompt.shape
    Bs = x_sample.shape[0]
    assert x_sample.shape[1] == 1

    nrow = Bp + Bs
    npad = -nrow % 16
    c_all = jnp.concatenate([c_prompt, c_sample, jnp.zeros((npad, D_MODEL), F32)], axis=0)
    mods_p, mods_s = [], []
    for l in range(DEPTH):
        mod = _linear(c_all, ada_w, w_layer=l, tm=nrow + npad, tn=512, pro='silu', epi='bias',
                      bias=ada_b[l][None, :])
        mods_p.append(mod[:Bp].reshape(Bp, 1, N_MOD * D_MODEL))
        mods_s.append(mod[Bp:nrow].reshape(1, Bs, N_MOD * D_MODEL))

    P = dict(norm_mix=norm_mix, norm_mlp=norm_mlp,
             w1=[mlp_w1[l].astype(BF16) for l in range(DEPTH)],
             w2=[mlp_w2[l].astype(BF16) for l in range(DEPTH)])
    hyb = [_hyb_params(i, hyb_w_in, hyb_w_out, ssd_conv_w, ssd_conv_b, ssd_dt_bias, ssd_a_log, ssd_d,
                       ssd_norm_w, rwkv_mu, rwkv_w0, rwkv_w2, rwkv_a0, rwkv_a2, rwkv_g2, rwkv_k_k,
                       rwkv_k_a, rwkv_r_k, rwkv_ln_w, rwkv_ln_b) for i in range(hyb_w_in.shape[0])]
    att = [dict(w_qkv=att_w_qkv[i].astype(BF16), w_out=att_w_out[i].astype(BF16))
           for i in range(att_w_qkv.shape[0])]

    y_p, new_p = _run_prompt(x_prompt.reshape(Bp * L, D_MODEL), mods_p, P, hyb, att, norm_final, Bp, L)
    y_s, new_s = _run_sample(x_sample.reshape(Bs, D_MODEL), mods_s, P, hyb, att, norm_final,
                             (state_ssd, state_ssd_conv, state_wkv, state_wkv_shift,
                              (cache_win0, cache_win1, cache_win2)), Bs)
    st = jnp.stack
    caches = (cache_win0, cache_win1, cache_win2)
    rolled = _roll_windows([_window_minor(c) for c in caches],
                           [_window_minor(st(new_s['win'][g])) for g in range(len(caches))])
    win_s = [jnp.transpose(t, (0, 1, 5, 2, 3, 4)) for t in rolled]
    return (y_p, y_s, st(new_p['ssd']), st(new_s['ssd']), st(new_p['conv']), st(new_s['conv']),
            st(new_p['wkv']), st(new_s['wkv']), st(new_p['shift']), st(new_s['shift']),
            st(new_p['win'][0]), win_s[0], st(new_p['win'][1]), win_s[1],
            st(new_p['win'][2]), win_s[2])
```

```python
import functools
import math

import numpy as np
import jax
import jax.numpy as jnp
from jax import lax
from jax.experimental import pallas as pl
from jax.experimental.pallas import tpu as pltpu

F32 = jnp.float32
BF16 = jnp.bfloat16
HIGHEST = lax.Precision.HIGHEST

D_MODEL = 1024
DEPTH = 4
PAST_LEN = 8192
NORM_EPS = 1e-6
N_MOD = 6
SSD_HEADS = 16
SSD_HEAD_DIM = 64
SSD_GROUPS = 2
SSD_STATE = 128
SSD_CONV = 4
SSD_CHUNK = 128
RWKV_HEADS = 16
RWKV_HEAD_DIM = 64
RWKV_LN_EPS = 64e-5
ATT_GROUPS = ((128, 1), (512, 4), (2048, 16))
ATT_HEADS = 8
ATT_HEAD_DIM = 64
ATT_DIM = ATT_HEADS * ATT_HEAD_DIM
ATT_Q_BLOCK = 128
ROPE_THETA = 500000.0
ROPE_DIM = ATT_HEAD_DIM // 4
MLP_HIDDEN = 4 * D_MODEL

U_COLS = 6144
U_Z, U_XS, U_R, U_K, U_V, U_BC, U_LW, U_DT = 0, 1024, 2048, 3072, 4096, 5120, 5632, 5888

LANES = 128
VMEM_LIMIT = 48 * 1024 * 1024
PROMPT_ROWS = 1024


def _cparams(*sem):
    return pltpu.CompilerParams(dimension_semantics=sem, vmem_limit_bytes=VMEM_LIMIT)


def _dot(a, b):
    return jnp.dot(a, b, preferred_element_type=F32)


def _dot_exact(a, b):
    return jnp.dot(a, b, preferred_element_type=F32, precision=HIGHEST)


def _dot_split(x, w2):
    hi = x.astype(BF16)
    lo = (x - hi.astype(F32)).astype(BF16)
    return _dot(jnp.concatenate([hi, lo], axis=1), w2)


def _twice(w):
    return jnp.concatenate([w, w], axis=0).astype(BF16)


def _dot_exact_nt(a, b):
    return lax.dot_general(a, b, (((1,), (1,)), ((), ())), preferred_element_type=F32, precision=HIGHEST)


def _dot_nt(a, b):
    return lax.dot_general(a, b, (((1,), (1,)), ((), ())), preferred_element_type=F32)


def _silu(x):
    return x * jax.nn.sigmoid(x)


def _softplus(x):
    return jnp.maximum(x, 0.0) + jnp.log1p(jnp.exp(-jnp.abs(x)))


def _block_ones():
    i = np.arange(LANES)
    return jnp.asarray((i[:, None] // 64 == i[None, :] // 64).astype(np.float32))


def _pair_eye():
    i = np.arange(64)
    j = np.arange(LANES)
    return jnp.asarray((i[:, None] == (j[None, :] % 64)).astype(np.float32))


def _head_expand(nheads, width):
    e = np.zeros((LANES, nheads * width), np.float32)
    for h in range(nheads):
        e[h, h * width:(h + 1) * width] = 1.0
    return jnp.asarray(e)


def _tril_ones(n):
    return jnp.asarray(np.tril(np.ones((n, n), np.float32)))


def _linear_kernel(*refs, pro, epi, two):
    refs = list(refs)
    x_ref = refs.pop(0)
    if pro == 'normmod':
        g_ref, sc_ref, sh_ref = refs.pop(0), refs.pop(0), refs.pop(0)
    w_ref = refs.pop(0)
    if two:
        x2_ref, w2_ref = refs.pop(0), refs.pop(0)
    if epi == 'bias':
        b_ref = refs.pop(0)
    if epi in ('resgate', 'resgate_norm'):
        res_ref, gate_ref = refs.pop(0), refs.pop(0)
    if epi == 'resgate_norm':
        gf_ref = refs.pop(0)
    if epi == 'rope':
        tab_refs = [refs.pop(0) for _ in range(3)]
    o_ref = refs.pop(0)

    if pro == 'cast':
        h = x_ref[...].astype(BF16)
    else:
        h_ref = refs.pop(0)

        @pl.when(pl.program_id(1) == 0)
        def _():
            x = x_ref[...].astype(F32)
            if pro == 'silu':
                hh = _silu(x)
            else:
                ms = jnp.mean(x * x, axis=-1, keepdims=True)
                y = (x * lax.rsqrt(ms + NORM_EPS)) * g_ref[...]
                hh = y * (1.0 + sc_ref[...]) + sh_ref[...]
            h_ref[...] = hh.astype(BF16)

        h = h_ref[...]
    if epi == 'rope':
        tabs = [t[...] for t in tab_refs]
        for c in range(3):
            cols = slice(c * ATT_DIM, (c + 1) * ATT_DIM)
            part = _dot(h, w_ref[:, cols])
            o_ref[:, cols] = _rope_apply(part, *tabs) if c < 2 else part
        return
    acc = _dot(h, w_ref[...].astype(BF16))
    if two:
        acc = acc + _dot(x2_ref[...].astype(BF16), w2_ref[...])
    if epi == 'bias':
        acc = acc + b_ref[...]
    elif epi == 'relu2':
        acc = jnp.square(jnp.maximum(acc, 0.0))
    elif epi in ('resgate', 'resgate_norm'):
        acc = res_ref[...] + gate_ref[...] * acc
        if epi == 'resgate_norm':
            ms = jnp.mean(acc * acc, axis=-1, keepdims=True)
            acc = (acc * lax.rsqrt(ms + NORM_EPS)) * gf_ref[...]
    o_ref[...] = acc.astype(o_ref.dtype)


def _linear(x, w, *, tm, tn, pro='cast', epi='none', norm=None, bias=None, res=None, gate=None,
            bpb=1, out_dtype=F32, second=None, rope=None, w_layer=0, final_norm=None):
    M, K = x.shape
    N = w.shape[-1]
    assert M % tm == 0 and N % tn == 0
    in_specs = [pl.BlockSpec((tm, K), lambda i, j: (i, 0))]
    args = [x]
    scratch = []
    if pro == 'normmod':
        g, mod, ksc, ksh = norm
        r = mod.shape[1]
        in_specs += [pl.BlockSpec((1, K), lambda i, j: (0, 0)),
                     pl.BlockSpec((None, r, K), lambda i, j: (i // bpb, 0, ksc)),
                     pl.BlockSpec((None, r, K), lambda i, j: (i // bpb, 0, ksh))]
        args += [g, mod, mod]
    if pro != 'cast':
        scratch = [pltpu.VMEM((tm, K), BF16)]
    if w.ndim == 3:
        in_specs.append(pl.BlockSpec((None, K, tn), lambda i, j: (w_layer, 0, j)))
    else:
        in_specs.append(pl.BlockSpec((K, tn), lambda i, j: (0, j)))
    args.append(w)
    if second is not None:
        x2, w2 = second
        K2 = x2.shape[1]
        in_specs += [pl.BlockSpec((tm, K2), lambda i, j: (i, 0)), pl.BlockSpec((K2, tn), lambda i, j: (0, j))]
        args += [x2, w2]
    if epi == 'bias':
        in_specs.append(pl.BlockSpec((1, tn), lambda i, j: (0, j)))
        args.append(bias)
    if epi in ('resgate', 'resgate_norm'):
        mod, kg = gate
        r = mod.shape[1]
        nj = N // tn
        in_specs += [pl.BlockSpec((tm, tn), lambda i, j: (i, j)),
                     pl.BlockSpec((None, r, tn), lambda i, j: (i // bpb, 0, kg * nj + j))]
        args += [res, mod]
    if epi == 'resgate_norm':
        assert tn == N
        in_specs.append(pl.BlockSpec((1, N), lambda i, j: (0, 0)))
        args.append(final_norm)
    if epi == 'rope':
        assert tn == 3 * ATT_DIM
        if rope[0].shape[0] == 1:
            in_specs += [pl.BlockSpec((1, LANES), lambda i, j: (0, 0))] * 3
        else:
            in_specs += [pl.BlockSpec((tm, LANES), lambda i, j: (i % bpb, 0))] * 3
        args += list(rope)
    return pl.pallas_call(
        functools.partial(_linear_kernel, pro=pro, epi=epi, two=second is not None),
        name="linear_%s_%s" % (pro, epi),
        grid=(M // tm, N // tn),
        in_specs=in_specs,
        out_specs=pl.BlockSpec((tm, tn), lambda i, j: (i, j)),
        out_shape=jax.ShapeDtypeStruct((M, N), out_dtype),
        scratch_shapes=scratch,
        compiler_params=_cparams("parallel", "arbitrary"),
    )(*args)


def _ssd_tail(y, xs, z, d_exp, norm_w):
    y = (y + d_exp * xs) * _silu(z)
    half = y.shape[1] // SSD_GROUPS
    outs = []
    for g in range(SSD_GROUPS):
        yg = y[:, g * half:(g + 1) * half]
        ms = jnp.mean(yg * yg, axis=-1, keepdims=True)
        outs.append(yg * lax.rsqrt(ms + NORM_EPS))
    return jnp.concatenate(outs, axis=1) * norm_w


def _ssd_prompt_kernel(z_ref, xs_ref, bc_ref, dt_ref, cw_ref, cb_ref, dtb_ref, alog_ref, dexp_ref,
                       nw_ref, tril_ref, e16_ref, y_ref, st_ref, extx, extbc, state, ybuf):
    c = pl.program_id(1)
    Q = SSD_CHUNK
    NX = SSD_HEADS * SSD_HEAD_DIM

    @pl.when(c == 0)
    def _():
        extx[0:8, :] = jnp.zeros((8, NX), F32)
        extbc[0:8, :] = jnp.zeros((8, 512), F32)
        state[...] = jnp.zeros_like(state)

    extx[8:8 + Q, :] = xs_ref[...]
    extbc[8:8 + Q, :] = bc_ref[...]
    cw = cw_ref[...]
    cb = cb_ref[...]
    xc = cb[:, 0:NX]
    bcc = cb[:, NX:NX + 512]
    for j in range(SSD_CONV):
        xc = xc + extx[pl.ds(5 + j, Q), :] * cw[j:j + 1, 0:NX]
        bcc = bcc + extbc[pl.ds(5 + j, Q), :] * cw[j:j + 1, NX:NX + 512]
    extx[0:8, :] = extx[Q:Q + 8, :]
    extbc[0:8, :] = extbc[Q:Q + 8, :]
    xs = _silu(xc)
    bcs = _silu(bcc)

    dt = _softplus(dt_ref[...] + dtb_ref[...])
    a_neg = -jnp.exp(alog_ref[...])
    acs = _dot_exact(tril_ref[...], dt * a_neg)
    acs_t = acs.T
    e16 = e16_ref[...]
    eacs = jnp.exp(acs)
    dt_exp = _dot_split(dt, e16)
    eacs_exp = _dot_split(eacs, e16)
    wend_exp = _dot_split(jnp.exp(acs[Q - 1:Q, :] - acs) * dt, e16)
    xdt = (xs * dt_exp).astype(BF16)
    xw = (xs * wend_exp).astype(BF16)
    row = lax.broadcasted_iota(jnp.int32, (Q, Q), 0)
    col = lax.broadcasted_iota(jnp.int32, (Q, Q), 1)
    causal = row >= col
    HG = SSD_HEADS // SSD_GROUPS
    GW = HG * SSD_HEAD_DIM
    for g in range(SSD_GROUPS):
        b_g = bcs[:, g * SSD_STATE:(g + 1) * SSD_STATE]
        c_g = bcs[:, 256 + g * SSD_STATE:256 + (g + 1) * SSD_STATE].astype(BF16)
        cb_g = _dot_nt(c_g, b_g.astype(BF16))
        bt_g = b_g.T.astype(BF16)
        for hg in range(HG):
            h = g * HG + hg
            seg = acs[:, h:h + 1] - acs_t[h:h + 1, :]
            decay = jnp.where(causal, jnp.exp(seg), 0.0)
            scores = (cb_g * decay).astype(BF16)
            ybuf[:, h * 64:(h + 1) * 64] = _dot(scores, xdt[:, h * 64:(h + 1) * 64])
        st_g = state[g]
        y_off = _dot(c_g, st_g.astype(BF16)) * eacs_exp[:, g * GW:(g + 1) * GW]
        ybuf[:, g * GW:(g + 1) * GW] = ybuf[:, g * GW:(g + 1) * GW] + y_off
        state[g] = st_g * eacs_exp[Q - 1:Q, g * GW:(g + 1) * GW] + _dot(bt_g, xw[:, g * GW:(g + 1) * GW])

    y_ref[...] = _ssd_tail(ybuf[...], xs, z_ref[...], dexp_ref[...], nw_ref[...]).astype(y_ref.dtype)

    @pl.when(c == pl.num_programs(1) - 1)
    def _():
        st_ref[...] = state[...]


def _ssd_prompt(u, B, L, cw, cb, dtb, alog, dexp, nw):
    Q = SSD_CHUNK
    nc = L // Q
    row = lambda b, c: b * nc + c
    const = lambda shape: pl.BlockSpec(shape, lambda b, c: (0,) * len(shape))
    y, st = pl.pallas_call(
        _ssd_prompt_kernel,
        name="ssd_prompt",
        grid=(B, nc),
        in_specs=[pl.BlockSpec((Q, 1024), lambda b, c: (row(b, c), U_Z // 1024)),
                  pl.BlockSpec((Q, 1024), lambda b, c: (row(b, c), U_XS // 1024)),
                  pl.BlockSpec((Q, 512), lambda b, c: (row(b, c), U_BC // 512)),
                  pl.BlockSpec((Q, 128), lambda b, c: (row(b, c), U_DT // 128)),
                  const((SSD_CONV, 1536)), const((1, 1536)), const((1, 128)), const((1, 128)),
                  const((1, 1024)), const((1, 1024)), const((Q, Q)), const((256, 1024))],
        out_specs=[pl.BlockSpec((Q, 1024), lambda b, c: (row(b, c), 0)),
                   pl.BlockSpec((None, SSD_GROUPS, SSD_STATE, 512), lambda b, c: (b, 0, 0, 0))],
        out_shape=[jax.ShapeDtypeStruct((B * L, 1024), BF16),
                   jax.ShapeDtypeStruct((B, SSD_GROUPS, SSD_STATE, 512), F32)],
        scratch_shapes=[pltpu.VMEM((Q + 8, 1024), F32), pltpu.VMEM((Q + 8, 512), F32),
                        pltpu.VMEM((SSD_GROUPS, SSD_STATE, 512), F32), pltpu.VMEM((Q, 1024), F32)],
        compiler_params=_cparams("parallel", "arbitrary"),
    )(u, u, u, u, cw, cb, dtb, alog, dexp, nw, _tril_ones(Q), _twice(_head_expand(SSD_HEADS, 64)))
    st = st.reshape(B, SSD_GROUPS, SSD_STATE, SSD_HEADS // SSD_GROUPS, SSD_HEAD_DIM)
    st = jnp.transpose(st, (0, 1, 3, 4, 2)).reshape(B, SSD_HEADS, SSD_HEAD_DIM, SSD_STATE)
    return y, st


def _ssd_step_kernel(z_ref, xs_ref, bc_ref, dt_ref, cx_ref, cbc_ref, s_ref, cw_ref, cb_ref, dtb_ref,
                     alog_ref, dexp_ref, nw_ref, e2_ref, y_ref, so_ref, ybuf):
    NX = SSD_HEADS * SSD_HEAD_DIM
    cw = cw_ref[...]
    cb = cb_ref[...]
    cx = cx_ref[...]
    cbc = cbc_ref[...]
    xc = cb[:, 0:NX] + xs_ref[...] * cw[3:4, 0:NX]
    bcc = cb[:, NX:NX + 512] + bc_ref[...] * cw[3:4, NX:NX + 512]
    for j in range(SSD_CONV - 1):
        xc = xc + cx[j:j + 1, :] * cw[j:j + 1, 0:NX]
        bcc = bcc + cbc[j:j + 1, :] * cw[j:j + 1, NX:NX + 512]
    xs = _silu(xc)
    bcs = _silu(bcc)
    dt = _softplus(dt_ref[...] + dtb_ref[...])
    da = jnp.exp(dt * (-jnp.exp(alog_ref[...])))
    e2 = e2_ref[...]
    lane = lax.broadcasted_iota(jnp.int32, (64, LANES), 1)
    first = lane < 64
    HG = SSD_HEADS // SSD_GROUPS
    heads = list(range(SSD_HEADS))
    diag = [e2 * xs[:, q * LANES:(q + 1) * LANES] for q in range(SSD_HEADS // 2)]
    xcol = [jnp.sum(jnp.where(first == (h % 2 == 0), diag[h // 2], 0.0), axis=1, keepdims=True)
            for h in heads]
    b_row = [bcs[:, (h // HG) * SSD_STATE:(h // HG + 1) * SSD_STATE] for h in heads]
    c_row = [bcs[:, 256 + (h // HG) * SSD_STATE:256 + (h // HG + 1) * SSD_STATE] for h in heads]
    s_new = [s_ref[h] * da[:, h:h + 1] + (xcol[h] * dt[:, h:h + 1]) * b_row[h] for h in heads]
    for h in heads:
        so_ref[h] = s_new[h]
    ycol = [jnp.sum(s_new[h] * c_row[h], axis=1, keepdims=True) for h in heads]
    for q in range(SSD_HEADS // 2):
        ypair = jnp.where(first, ycol[2 * q], ycol[2 * q + 1])
        ybuf[:, q * LANES:(q + 1) * LANES] = jnp.sum(e2 * ypair, axis=0, keepdims=True)
    y_ref[...] = _ssd_tail(ybuf[...], xs, z_ref[...], dexp_ref[...], nw_ref[...]).astype(y_ref.dtype)


def _ssd_step(u, conv_x, conv_bc, s0, cw, cb, dtb, alog, dexp, nw):
    B = u.shape[0]
    const = lambda shape: pl.BlockSpec(shape, lambda b: (0,) * len(shape))
    return pl.pallas_call(
        _ssd_step_kernel,
        name="ssd_step",
        grid=(B,),
        in_specs=[pl.BlockSpec((None, 1, 1024), lambda b: (b, 0, U_Z // 1024)),
                  pl.BlockSpec((None, 1, 1024), lambda b: (b, 0, U_XS // 1024)),
                  pl.BlockSpec((None, 1, 512), lambda b: (b, 0, U_BC // 512)),
                  pl.BlockSpec((None, 1, 128), lambda b: (b, 0, U_DT // 128)),
                  pl.BlockSpec((None, 3, 1024), lambda b: (b, 0, 0)),
                  pl.BlockSpec((None, 3, 512), lambda b: (b, 0, 0)),
                  pl.BlockSpec((None, SSD_HEADS, 64, 128), lambda b: (b, 0, 0, 0)),
                  const((SSD_CONV, 1536)), const((1, 1536)), const((1, 128)), const((1, 128)),
                  const((1, 1024)), const((1, 1024)), const((64, 128))],
        out_specs=[pl.BlockSpec((None, 1, 1024), lambda b: (b, 0, 0)),
                   pl.BlockSpec((None, SSD_HEADS, 64, 128), lambda b: (b, 0, 0, 0))],
        out_shape=[jax.ShapeDtypeStruct((B, 1, 1024), BF16),
                   jax.ShapeDtypeStruct(s0.shape, F32)],
        scratch_shapes=[pltpu.VMEM((1, 1024), F32)],
        compiler_params=_cparams("parallel"),
    )(u, u, u, u, conv_x, conv_bc, s0, cw, cb, dtb, alog, dexp, nw, _pair_eye())


def _rwkv_prep_kernel(*refs, shifted, bpb):
    refs = list(refs)
    cur = [refs.pop(0) for _ in range(4)]
    prev = [refs.pop(0) for _ in range(4)]
    if shifted:
        first = [refs.pop(0) for _ in range(4)]
    mu = [refs.pop(0) for _ in range(4)]
    (w0_ref, w2_ref, a0_ref, a2_ref, g2_ref, kk_ref, ka_ref, bo_ref) = [refs.pop(0) for _ in range(8)]
    (r_o, w_o, k_o, v_o, kn_o, kka_o, g_o) = refs
    i = pl.program_id(0)

    def mixed(n):
        x = cur[n][...]
        if shifted:
            rolled = pltpu.roll(x, 1, 0)
            before = jnp.where(i % bpb == 0, first[n][...], prev[n][7:8, :])
            rid = lax.broadcasted_iota(jnp.int32, x.shape, 0)
            p = jnp.where(rid == 0, before, rolled)
        else:
            p = prev[n][...]
        return x + (p - x) * mu[n][...]

    r, k, v, lw = mixed(0), mixed(1), mixed(2), mixed(3)
    blk = lw[:, 0:LANES]
    lane = lax.broadcasted_iota(jnp.int32, blk.shape, 1)
    tw = jnp.where(lane < 64, jnp.tanh(blk), blk).astype(BF16)
    wpre = w0_ref[...] + _dot(tw, w2_ref[...])
    apre = a0_ref[...] + _dot(tw, a2_ref[...])
    wlog = -_softplus(-wpre) - 0.5
    a = jax.nn.sigmoid(apre)
    g = _dot(jax.nn.sigmoid(lw[:, LANES:2 * LANES]).astype(BF16), g2_ref[...])
    kk = k * kk_ref[...]
    kk2 = kk * kk
    bo = bo_ref[...]
    for q in range(RWKV_HEADS // 2):
        sl = slice(q * LANES, (q + 1) * LANES)
        n2 = _dot_exact(kk2[:, sl], bo)
        kn = kk[:, sl] / jnp.maximum(jnp.sqrt(n2), 1e-12)
        kn_o[:, sl] = -kn
        kka_o[:, sl] = kn * a[:, sl]
    r_o[...] = r
    w_o[...] = -jnp.exp(wlog)
    k_o[...] = k * (1.0 + (a - 1.0) * ka_ref[...])
    v_o[...] = v
    g_o[...] = g


def _rwkv_prep(u, prev, first, mus, ws, *, tm, bpb, shifted):
    M = u.shape[0]
    cols = [(1024, U_R // 1024), (1024, U_K // 1024), (1024, U_V // 1024), (256, U_LW // 256)]
    in_specs = [pl.BlockSpec((tm, c), functools.partial(lambda i, kb: (i, kb), kb=kb)) for c, kb in cols]
    args = [u] * 4
    if shifted:
        in_specs += [pl.BlockSpec((8, c), functools.partial(
            lambda i, kb: (jnp.maximum(i * (tm // 8) - 1, 0), kb), kb=kb)) for c, kb in cols]
        args += [u] * 4
        in_specs += [pl.BlockSpec((None, 1, c), lambda i: (i // bpb, 0, 0)) for c, _ in cols]
        args += list(first)
    else:
        in_specs += [pl.BlockSpec((tm, c), lambda i: (i, 0)) for c, _ in cols]
        args += list(prev)
    in_specs += [pl.BlockSpec((1, c), lambda i: (0, 0)) for c, _ in cols]
    args += list(mus)
    wshapes = [(1, 1024), (128, 1024), (1, 1024), (128, 1024), (128, 1024), (1, 1024), (1, 1024), (128, 128)]
    in_specs += [pl.BlockSpec(s, lambda i: (0, 0)) for s in wshapes]
    args += list(ws)
    return pl.pallas_call(
        functools.partial(_rwkv_prep_kernel, shifted=shifted, bpb=bpb),
        name="rwkv_prep",
        grid=(M // tm,),
        in_specs=in_specs,
        out_specs=[pl.BlockSpec((tm, 1024), lambda i: (i, 0))] * 7,
        out_shape=[jax.ShapeDtypeStruct((M, 1024), F32)] * 7,
        compiler_params=_cparams("parallel"),
    )(*args)


def _wkv_step_kernel(r_ref, lw_ref, k_ref, v_ref, kn_ref, ka_ref, s0_ref, e2_ref, bo_ref, o_ref, sT_ref, *, nb):
    n = RWKV_HEAD_DIM
    e2 = e2_ref[...]
    bo = bo_ref[...]
    first = lax.broadcasted_iota(jnp.int32, (n, LANES), 1) < n

    def pair_sum(x):
        sa = jnp.sum(jnp.where(first, x, 0.0), axis=1, keepdims=True)
        sb = jnp.sum(jnp.where(first, 0.0, x), axis=1, keepdims=True)
        return jnp.where(first, sa, sb)

    for b in range(nb):
        for p in range(RWKV_HEADS // 2):
            sl = slice(p * LANES, (p + 1) * LANES)
            S = jnp.concatenate([s0_ref[b, 2 * p], s0_ref[b, 2 * p + 1]], axis=1)
            sa = pair_sum(S * kn_ref[b, :, sl])
            vcol = _dot_exact(e2 * v_ref[b, :, sl], bo)
            S = S * jnp.exp(lw_ref[b, :, sl]) + sa * ka_ref[b, :, sl] + vcol * k_ref[b, :, sl]
            sT_ref[b, 2 * p] = S[:, 0:n]
            sT_ref[b, 2 * p + 1] = S[:, n:2 * n]
            o = pair_sum(S * r_ref[b, :, sl])
            o_ref[b, :, sl] = jnp.sum(e2 * o, axis=0, keepdims=True)


def _wkv_step(r, lw, k, v, kn, ka, s0, *, nb):
    B = r.shape[0]
    seq = pl.BlockSpec((nb, 1, 1024), lambda b: (b, 0, 0))
    stt = pl.BlockSpec((nb,) + s0.shape[1:], lambda b: (b, 0, 0, 0))
    return pl.pallas_call(
        functools.partial(_wkv_step_kernel, nb=nb),
        grid=(B // nb,),
        in_specs=[seq] * 6 + [stt, pl.BlockSpec((64, 128), lambda b: (0, 0)),
                              pl.BlockSpec((128, 128), lambda b: (0, 0))],
        out_specs=[seq, stt],
        out_shape=[jax.ShapeDtypeStruct((B, 1, 1024), F32), jax.ShapeDtypeStruct(s0.shape, F32)],
        compiler_params=_cparams("parallel"),
        name="wkv_step",
    )(r, lw, k, v, kn, ka, s0, _pair_eye(), _block_ones())


WKV_CHUNK = 64


def _wkv_chunk_kernel(r_ref, lw_ref, k_ref, v_ref, kn_ref, ka_ref, tril_ref, o_ref, sT_ref, S_ref, *, nb):
    C = WKV_CHUNK
    c = pl.program_id(1)

    @pl.when(c == 0)
    def _():
        S_ref[...] = jnp.zeros_like(S_ref)

    tril = tril_ref[...]
    lane = lax.broadcasted_iota(jnp.int32, (C, LANES), 1)
    rowi = lax.broadcasted_iota(jnp.int32, (C, LANES), 0)
    first = lane < 64
    strict = rowi > (lane % 64)
    incl = rowi >= (lane % 64)
    r128 = lax.broadcasted_iota(jnp.int32, (LANES, LANES), 0)
    c128 = lax.broadcasted_iota(jnp.int32, (LANES, LANES), 1)
    diag_blocks = (r128 < 64) == (c128 < 64)
    eye = r128 == c128

    def bd(x):
        return jnp.concatenate([jnp.where(first, x, 0.0), jnp.where(first, 0.0, x)], axis=0)

    bf = lambda x: x.astype(BF16)
    pairs = [(b, p) for b in range(nb) for p in range(RWKV_HEADS // 2)]
    sls = [slice(p * LANES, (p + 1) * LANES) for _, p in pairs]
    load = lambda ref: [ref[b, :, sl] for (b, _), sl in zip(pairs, sls)]
    each = lambda f, *ls: [f(*a) for a in zip(*ls)]
    r_, lw, kt, vv, al, be = (load(ref) for ref in (r_ref, lw_ref, k_ref, v_ref, kn_ref, ka_ref))
    def cumulative(x):
        hi = x.astype(BF16)
        lo = (x - hi.astype(F32)).astype(BF16)
        return _dot(tril, jnp.concatenate([hi, lo], axis=0))

    cs = each(cumulative, lw)
    last = each(lambda x: x[C - 1:C, :], cs)
    e_inv = each(lambda x: jnp.exp(-x), cs)
    aq = each(lambda a, x, l: a * jnp.exp(x - l), al, cs, lw)
    rq = each(lambda a, x: a * jnp.exp(x), r_, cs)
    bk = each(jnp.multiply, be, e_inv)
    kk = each(jnp.multiply, kt, e_inv)
    g = each(lambda a, q, b_, k_: _dot_nt(
        bf(jnp.concatenate([a, q], axis=0)),
        bf(jnp.concatenate([jnp.where(first, b_, 0.0), jnp.where(first, 0.0, b_),
                            jnp.where(first, k_, 0.0), jnp.where(first, 0.0, k_)], axis=0))), aq, rq, bk, kk)
    m1 = each(lambda x: bf(jnp.where(strict, x[0:C, 0:LANES], 0.0)), g)
    m2 = each(lambda x: bf(jnp.where(strict, x[0:C, LANES:2 * LANES], 0.0)), g)
    n1 = each(lambda x: bf(jnp.where(incl, x[C:2 * C, 0:LANES], 0.0)), g)
    n2 = each(lambda x: bf(jnp.where(incl, x[C:2 * C, LANES:2 * LANES], 0.0)), g)
    s0 = [S_ref[b, p] for b, p in pairs]
    s0b = each(bf, s0)
    vbd = each(lambda x: bf(bd(x)), vv)
    x = each(lambda a, s, m, v_: _dot(bf(a), s) + _dot(m, v_), aq, s0b, m2, vbd)
    mp = m1
    steps = int(math.log2(C))
    for i in range(steps):
        x = each(lambda x_, m: x_ + _dot(m, bf(bd(x_))), x, mp)
        if i + 1 < steps:
            mp = each(lambda m: bf(_dot(m, bd(m))), mp)
    o = each(lambda q, s, a, x_, b_, v_: _dot(bf(q), s) + _dot(a, bf(bd(x_))) + _dot(b_, v_),
             rq, s0b, n1, x, n2, vbd)
    for (b, _), sl, o_ in zip(pairs, sls, o):
        o_ref[b, :, sl] = o_
    e_end = each(lambda l, x_: jnp.exp(l - x_), last, cs)
    kv_t = each(lambda b_, k_, e: jnp.concatenate([b_ * e, k_ * e], axis=0).T, be, kt, e_end)
    upd = each(lambda t, x_, v_: _dot(bf(t), bf(jnp.concatenate([x_, v_], axis=0))), kv_t, x, vv)
    gcol = each(lambda l: jnp.sum(jnp.where(eye, jnp.exp(l), 0.0), axis=1, keepdims=True), last)
    for (b, p), s, u, gc in zip(pairs, s0, upd, gcol):
        S_ref[b, p] = jnp.where(diag_blocks, gc * s + u, 0.0)

    @pl.when(c == pl.num_programs(1) - 1)
    def _():
        sT_ref[...] = S_ref[...]


def _wkv_chunked(r, lw, k, v, kn, ka, *, nb):
    B, L, _ = r.shape
    C = WKV_CHUNK
    seq = pl.BlockSpec((nb, C, 1024), lambda b, c: (b, c, 0))
    stt = pl.BlockSpec((nb, 8, LANES, LANES), lambda b, c: (b, 0, 0, 0))
    o, sT = pl.pallas_call(
        functools.partial(_wkv_chunk_kernel, nb=nb),
        grid=(B // nb, L // C),
        in_specs=[seq] * 6 + [pl.BlockSpec((C, 2 * C), lambda b, c: (0, 0))],
        out_specs=[seq, stt],
        out_shape=[jax.ShapeDtypeStruct((B, L, 1024), F32), jax.ShapeDtypeStruct((B, 8, LANES, LANES), F32)],
        scratch_shapes=[pltpu.VMEM((nb, 8, LANES, LANES), F32)],
        compiler_params=_cparams("parallel", "arbitrary"),
        name="wkv_chunked",
    )(r, lw, k, v, kn, ka, jnp.concatenate([_tril_ones(C)] * 2, axis=1).astype(BF16))
    blocks = jnp.stack([sT[:, :, 0:64, 0:64], sT[:, :, 64:128, 64:128]], axis=2)
    return o, jnp.swapaxes(blocks, -1, -2).reshape(B, 16, 64, 64)


def _rwkv_post_kernel(o_ref, r_ref, k_ref, v_ref, g_ref, lnw_ref, lnb_ref, rk_ref, bo_ref, y_ref):
    bo = bo_ref[...]
    inv = 1.0 / RWKV_HEAD_DIM
    for q in range(RWKV_HEADS // 2):
        sl = slice(q * LANES, (q + 1) * LANES)
        o = o_ref[:, sl]
        mean = _dot_split(o, bo) * inv
        d = o - mean
        var = _dot_split(d * d, bo) * inv
        on = d * lax.rsqrt(var + RWKV_LN_EPS) * lnw_ref[:, sl] + lnb_ref[:, sl]
        bonus = _dot_split(r_ref[:, sl] * k_ref[:, sl] * rk_ref[:, sl], bo) * v_ref[:, sl]
        y_ref[:, sl] = ((on + bonus) * g_ref[:, sl]).astype(y_ref.dtype)


def _rwkv_post(o, r, k, v, g, lnw, lnb, rk, tm):
    M = o.shape[0]
    blk = pl.BlockSpec((tm, 1024), lambda i: (i, 0))
    vec = pl.BlockSpec((1, 1024), lambda i: (0, 0))
    return pl.pallas_call(
        _rwkv_post_kernel,
        name="rwkv_post",
        grid=(M // tm,),
        in_specs=[blk] * 5 + [vec] * 3 + [pl.BlockSpec((256, 128), lambda i: (0, 0))],
        out_specs=blk,
        out_shape=jax.ShapeDtypeStruct((M, 1024), BF16),
        compiler_params=_cparams("parallel"),
    )(o, r, k, v, g, lnw, lnb, rk, _twice(_block_ones()))


def _rope_tables(pos):
    half = ROPE_DIM // 2
    inv = ROPE_THETA ** (-jnp.arange(half, dtype=F32) * 2.0 / ROPE_DIM)
    ang = pos.astype(F32)[:, None] * inv
    cos, sin = jnp.cos(ang), jnp.sin(ang)
    n = pos.shape[0]
    rest = ATT_HEAD_DIM - ROPE_DIM
    c = jnp.concatenate([cos, cos, jnp.ones((n, rest), F32)], axis=1)
    s_next = jnp.concatenate([-sin, jnp.zeros((n, half + rest), F32)], axis=1)
    s_prev = jnp.concatenate([jnp.zeros((n, half), F32), sin, jnp.zeros((n, rest), F32)], axis=1)
    return tuple(jnp.concatenate([t, t], axis=1) for t in (c, s_next, s_prev))


def _rope_apply(x, c, s_next, s_prev):
    n = x.shape[1]
    reps = n // LANES
    tile = lambda t: jnp.concatenate([t] * reps, axis=1)
    half = ROPE_DIM // 2
    return x * tile(c) + pltpu.roll(x, n - half, 1) * tile(s_next) + pltpu.roll(x, half, 1) * tile(s_prev)


ATT_BLOCK_ROWS = ATT_Q_BLOCK * max(d for _, d in ATT_GROUPS)
ATT_BATCH = 8


def _attn_prompt_kernel(q_ref, kc_ref, kp_ref, vc_ref, vp_ref, o_ref, lse_ref, *, d):
    i = pl.program_id(0)
    QB = ATT_Q_BLOCK
    R = q_ref.shape[0]
    row = lax.broadcasted_iota(jnp.int32, (QB, 2 * QB), 0)
    col = lax.broadcasted_iota(jnp.int32, (QB, 2 * QB), 1)
    band = (col >= row) & (col <= row + QB)
    band_first = band & ((i > 0) | (col >= QB))
    first = lax.broadcasted_iota(jnp.int32, (QB, LANES), 1) < ATT_HEAD_DIM
    first_kv = lax.broadcasted_iota(jnp.int32, (2 * QB, LANES), 1) < ATT_HEAD_DIM
    scale = ATT_HEAD_DIM ** -0.5
    span = QB * d
    blocks = [(rho, j) for rho in range(d) for j in range(R // span)]
    each = lambda f, *ls: [f(*a) for a in zip(*ls)]
    for b0 in range(0, len(blocks), ATT_BATCH):
        batch = blocks[b0:b0 + ATT_BATCH]
        ds = lambda start, n: pl.ds(start, n, stride=d) if d > 1 else pl.ds(start, n)
        qrows = [ds(rho + span * j, QB) for rho, j in batch]
        valid = [band_first if j == 0 else band for _, j in batch]

        def keys(cur_ref, prev_ref, rho, j):
            if j > 0:
                return cur_ref[ds(rho + span * (j - 1), 2 * QB), :]
            return jnp.concatenate([prev_ref[ds(R - span + rho, QB), :], cur_ref[ds(rho, QB), :]], axis=0)

        qb = [q_ref[r, :].astype(BF16) for r in qrows]
        k2 = [keys(kc_ref, kp_ref, rho, j).astype(BF16) for rho, j in batch]
        v2 = [keys(vc_ref, vp_ref, rho, j).astype(BF16) for rho, j in batch]
        halves = [slice(0, ATT_HEAD_DIM), slice(ATT_HEAD_DIM, LANES)]
        s = [[jnp.where(vm, _dot_nt(q[:, sl], k[:, sl]) * scale, -jnp.inf) for sl in halves]
             for q, k, vm in zip(qb, k2, valid)]
        m = [[jnp.max(jnp.maximum(x[:, 0:QB], x[:, QB:2 * QB]), axis=1, keepdims=True) for x in pair] for pair in s]
        p = [[jnp.exp(x - mx).astype(BF16) for x, mx in zip(ps, ms)] for ps, ms in zip(s, m)]
        va = each(lambda v: jnp.where(first_kv, v, 1.0), v2)
        vb = each(lambda v: jnp.where(first_kv, 1.0, v), v2)
        ea = each(lambda pp, v: _dot(pp[0], v), p, va)
        eb = each(lambda pp, v: _dot(pp[1], v), p, vb)
        num = each(lambda a, b_: jnp.where(first, a, b_), ea, eb)
        den = each(lambda a, b_: pltpu.roll(jnp.where(first, b_, a), ATT_HEAD_DIM, 1), ea, eb)
        for r, n_, d_, mm in zip(qrows, num, den, m):
            o_ref[r, :] = n_ / d_
            lse_ref[r, :] = jnp.where(first, mm[0], mm[1]) + jnp.log(d_)


def _attn_prompt(qkv, B, L, gi):
    window, d = ATT_GROUPS[gi]
    R = min(ATT_BLOCK_ROWS, L)
    assert window == ATT_Q_BLOCK * d and R % (ATT_Q_BLOCK * d) == 0 and L % R == 0
    nblk = L // R
    npair = ATT_DIM // LANES
    prev = lambda i: jnp.maximum(i - 1, 0)

    def col(which, f):
        return lambda i, b, hp: (b * nblk + f(i), gi * 3 * npair + which * npair + hp)

    same = lambda i: i
    blk = lambda f: pl.BlockSpec((R, LANES), f)
    out = pl.BlockSpec((R, LANES), lambda i, b, hp: (b * nblk + i, hp))
    return pl.pallas_call(
        functools.partial(_attn_prompt_kernel, d=d),
        grid=(nblk, B, npair),
        in_specs=[blk(col(0, same)), blk(col(1, same)), blk(col(1, prev)), blk(col(2, same)), blk(col(2, prev))],
        out_specs=[out, out],
        out_shape=[jax.ShapeDtypeStruct((B * L, ATT_DIM), F32)] * 2,
        compiler_params=_cparams("arbitrary", "arbitrary", "arbitrary"),
        name="attn_prompt_d%d" % d,
    )(qkv, qkv, qkv, qkv, qkv)


def _attn_out_kernel(o0, l0, o1, l1, o2, l2, w_ref, res_ref, gate_ref, out_ref, h_ref):
    @pl.when(pl.program_id(1) == 0)
    def _():
        m = jnp.maximum(jnp.maximum(l0[...], l1[...]), l2[...])
        a0, a1, a2 = jnp.exp(l0[...] - m), jnp.exp(l1[...] - m), jnp.exp(l2[...] - m)
        o = (a0 * o0[...] + a1 * o1[...] + a2 * o2[...]) / (a0 + a1 + a2)
        h_ref[...] = o.astype(BF16)

    out_ref[...] = res_ref[...] + gate_ref[...] * _dot(h_ref[...], w_ref[...])


def _attn_out(ols, w, res, mod, kg, *, tm, tn, bpb):
    M = res.shape[0]
    N = w.shape[1]
    r = mod.shape[1]
    nj = N // tn
    part = pl.BlockSpec((tm, 512), lambda i, j: (i, 0))
    return pl.pallas_call(
        _attn_out_kernel,
        name="attn_out",
        grid=(M // tm, nj),
        in_specs=[part] * 6 + [pl.BlockSpec((512, tn), lambda i, j: (0, j)),
                               pl.BlockSpec((tm, tn), lambda i, j: (i, j)),
                               pl.BlockSpec((None, r, tn), lambda i, j: (i // bpb, 0, kg * nj + j))],
        out_specs=pl.BlockSpec((tm, tn), lambda i, j: (i, j)),
        out_shape=jax.ShapeDtypeStruct((M, N), F32),
        scratch_shapes=[pltpu.VMEM((tm, 512), BF16)],
        compiler_params=_cparams("parallel", "arbitrary"),
    )(*ols, w, res, mod)


def _attn_step_kernel(q_ref, kn_ref, vn_ref, c0_ref, c1_ref, c2_ref, o_ref):
    caches = (c0_ref, c1_ref, c2_ref)
    scale = ATT_HEAD_DIM ** -0.5
    hl = lax.broadcasted_iota(jnp.int32, (ATT_HEAD_DIM, ATT_HEADS), 1)
    ng = len(ATT_GROUPS)
    combos = [(h, gi) for h in range(ATT_HEADS) for gi in range(ng)]
    each = lambda f, *ls: [f(*a) for a in zip(*ls)]
    col = lambda ref: [ref[gi][:, h:h + 1] for h, gi in combos]
    q, kn, vn = col(q_ref), col(kn_ref), col(vn_ref)
    vis = [lax.broadcasted_iota(jnp.int32, (1, w), 1) % d == 0 for w, d in ATT_GROUPS]
    s = [jnp.where(vis[gi], jnp.sum(caches[gi][0, h] * q_, axis=0, keepdims=True) * scale, -jnp.inf)
         for (h, gi), q_ in zip(combos, q)]
    s_new = each(lambda a, b: jnp.sum(a * b, axis=0, keepdims=True) * scale, q, kn)
    m = each(lambda a, b: jnp.maximum(jnp.max(a, axis=1, keepdims=True), b), s, s_new)
    p = each(lambda a, b: jnp.exp(a - b), s, m)
    p_new = each(lambda a, b: jnp.exp(a - b), s_new, m)
    l = each(lambda a, b: jnp.sum(a, axis=1, keepdims=True) + b, p, p_new)
    acc = [jnp.sum(caches[gi][1, h] * p_, axis=1, keepdims=True) + pn * v_
           for (h, gi), p_, pn, v_ in zip(combos, p, p_new, vn)]
    og = each(lambda a, b: a / b, acc, l)
    lse = each(lambda a, b: a + jnp.log(b), m, l)
    out = jnp.zeros((ATT_HEAD_DIM, ATT_HEADS), F32)
    for h in range(ATT_HEADS):
        os_, ls_ = og[h * ng:(h + 1) * ng], lse[h * ng:(h + 1) * ng]
        mm = jnp.maximum(jnp.maximum(ls_[0], ls_[1]), ls_[2])
        ws = [jnp.exp(x - mm) for x in ls_]
        o = (ws[0] * os_[0] + ws[1] * os_[1] + ws[2] * os_[2]) / (ws[0] + ws[1] + ws[2])
        out = jnp.where(hl == h, o, out)
    o_ref[...] = out


def _window_minor(c):
    return jnp.transpose(c, (0, 1, 3, 4, 5, 2))


def _attn_step(q3, kn3, vn3, caches_t, layer):
    B = q3.shape[0]
    for (window, d), c in zip(ATT_GROUPS, caches_t):
        assert c.shape[-1] == window and window // d == 128
    cols = lambda t: jnp.swapaxes(t, -1, -2)
    specs = [pl.BlockSpec((None, None) + c.shape[2:], lambda b: (layer, b, 0, 0, 0, 0)) for c in caches_t]
    new = pl.BlockSpec((None, len(ATT_GROUPS), ATT_HEAD_DIM, ATT_HEADS), lambda b: (b, 0, 0, 0))
    o = pl.pallas_call(
        _attn_step_kernel,
        grid=(B,),
        in_specs=[new] * 3 + specs,
        out_specs=pl.BlockSpec((None, ATT_HEAD_DIM, ATT_HEADS), lambda b: (b, 0, 0)),
        out_shape=jax.ShapeDtypeStruct((B, ATT_HEAD_DIM, ATT_HEADS), F32),
        compiler_params=_cparams("parallel"),
        name="attn_step",
    )(cols(q3), cols(kn3), cols(vn3), *caches_t)
    return jnp.swapaxes(o, -1, -2)


def _roll_kernel(c0, c1, c2, n0, n1, n2, o0, o1, o2):
    for c_ref, n_ref, o_ref in ((c0, n0, o0), (c1, n1, o1), (c2, n2, o2)):
        w = c_ref.shape[-1]
        rows = c_ref.shape[0] * c_ref.shape[1] * c_ref.shape[2]
        x = c_ref[...].reshape(rows, w)
        lane = lax.broadcasted_iota(jnp.int32, (rows, w), 1)
        y = jnp.where(lane == w - 1, n_ref[...].reshape(rows, 1), pltpu.roll(x, w - 1, 1))
        o_ref[...] = y.reshape(o_ref.shape)


def _roll_windows(caches_t, rows_t):
    NC, B = caches_t[0].shape[:2]
    spec = lambda a: pl.BlockSpec((None, None) + a.shape[2:], lambda i, b: (i, b, 0, 0, 0, 0))
    return pl.pallas_call(
        _roll_kernel,
        grid=(NC, B),
        in_specs=[spec(c) for c in caches_t] + [spec(r) for r in rows_t],
        out_specs=[spec(c) for c in caches_t],
        out_shape=[jax.ShapeDtypeStruct(c.shape, c.dtype) for c in caches_t],
        compiler_params=_cparams("parallel", "parallel"),
        name="roll_windows",
    )(*caches_t, *rows_t)


def _hyb_params(i, hyb_w_in, hyb_w_out, ssd_conv_w, ssd_conv_b, ssd_dt_bias, ssd_a_log, ssd_d, ssd_norm_w,
                rwkv_mu, rwkv_w0, rwkv_w2, rwkv_a0, rwkv_a2, rwkv_g2, rwkv_k_k, rwkv_k_a, rwkv_r_k,
                rwkv_ln_w, rwkv_ln_b):
    w = hyb_w_in[i]
    rw0 = 2576
    w_perm = jnp.concatenate(
        [w[:, 0:1024], w[:, 1024:2048], w[:, rw0:rw0 + 3072], w[:, 2048:2560], w[:, rw0 + 3072:rw0 + 3328],
         w[:, 2560:2576], jnp.zeros((D_MODEL, U_COLS - U_DT - 16), F32)], axis=1).astype(BF16)
    pad128 = lambda v: jnp.concatenate([v, jnp.zeros((LANES - v.shape[0],), F32)])[None, :]
    z64 = jnp.zeros((64, 1024), F32)
    mu = rwkv_mu[i]
    p = dict(
        w_in=w_perm, w_out=hyb_w_out[i].astype(BF16),
        cw=ssd_conv_w[i], cb=ssd_conv_b[i][None, :], dtb=pad128(ssd_dt_bias[i]), alog=pad128(ssd_a_log[i]),
        dexp=jnp.repeat(ssd_d[i], SSD_HEAD_DIM)[None, :], nw=ssd_norm_w[i][None, :],
        mus=[mu[None, 0:1024], mu[None, 1024:2048], mu[None, 2048:3072], mu[None, 3072:3328]],
        prep_w=[rwkv_w0[i][None, :], jnp.concatenate([rwkv_w2[i], z64]).astype(BF16),
                rwkv_a0[i][None, :], jnp.concatenate([z64, rwkv_a2[i]]).astype(BF16),
                rwkv_g2[i].astype(BF16), rwkv_k_k[i][None, :], rwkv_k_a[i][None, :], _block_ones()],
        lnw=rwkv_ln_w[i][None, :], lnb=rwkv_ln_b[i][None, :], rk=rwkv_r_k[i].reshape(1, 1024),
    )
    return p


def _raw_conv_rows(u_rows):
    return jnp.concatenate([u_rows[..., U_XS:U_XS + 1024], u_rows[..., U_BC:U_BC + 512]], axis=-1)


def _raw_rw_rows(u_rows):
    return jnp.concatenate([u_rows[..., U_R:U_R + 3072], u_rows[..., U_LW:U_LW + 256]], axis=-1)


def _run_prompt(x, mods, P, hyb, att, norm_final, B, L):
    T = B * L
    big, half = PROMPT_ROWS, PROMPT_ROWS // 2
    lin = lambda *a, tm, **kw: _linear(*a, tm=tm, bpb=L // tm, **kw)
    new = dict(ssd=[], conv=[], wkv=[], shift=[], win=[[], [], []])
    tabs = _rope_tables(jnp.arange(L))
    for l in range(DEPTH):
        mod = mods[l]
        i = l // 2
        gmix = P['norm_mix'][l][None, :]
        if l % 2 == 0:
            hp = hyb[i]
            u = lin(x, hp['w_in'], tm=big, tn=2048, pro='normmod', norm=(gmix, mod, 1, 0))
            y_ssd, s_ssd = _ssd_prompt(u, B, L, hp['cw'], hp['cb'], hp['dtb'], hp['alog'], hp['dexp'], hp['nw'])
            zeros = [jnp.zeros((B, 1, c), F32) for c in (1024, 1024, 1024, 256)]
            r, w, k, v, kn, ka, g = _rwkv_prep(u, None, zeros, hp['mus'], hp['prep_w'], tm=256, bpb=L // 256,
                                               shifted=True)
            sh = lambda t: t.reshape(B, L, 1024)
            o, s_wkv = _wkv_chunked(sh(r), sh(w), sh(k), sh(v), sh(kn), sh(ka), nb=B)
            y_rwkv = _rwkv_post(o.reshape(T, 1024), r, k, v, g, hp['lnw'], hp['lnb'], hp['rk'], 512)
            x = lin(y_ssd, hp['w_out'][:1024], tm=big, tn=1024, epi='resgate', res=x, gate=(mod, 2),
                    second=(y_rwkv, hp['w_out'][1024:]))
            u3 = u.reshape(B, L, U_COLS)
            new['ssd'].append(s_ssd)
            new['conv'].append(_raw_conv_rows(u3[:, L - (SSD_CONV - 1):]))
            new['wkv'].append(s_wkv)
            new['shift'].append(_raw_rw_rows(u3[:, L - 1]))
        else:
            ap = att[i]
            qkv = lin(x, ap['w_qkv'], tm=big, tn=3 * ATT_DIM, pro='normmod', epi='rope',
                      norm=(gmix, mod, 1, 0), rope=tabs)
            ols = []
            q3 = qkv.reshape(B, L, len(ATT_GROUPS) * 3 * ATT_DIM)
            for gi, (window, d) in enumerate(ATT_GROUPS):
                ols += _attn_prompt(qkv, B, L, gi)
                keep = min(window, L)
                k0 = (gi * 3 + 1) * ATT_DIM
                kv = q3[:, L - keep:, k0:k0 + 2 * ATT_DIM]
                new['win'][gi].append(kv.reshape(B, keep, 2, ATT_HEADS, ATT_HEAD_DIM))
            x = _attn_out(ols, ap['w_out'], x, mod, 2, tm=half, tn=1024, bpb=L // half)
        gmlp = P['norm_mlp'][l][None, :]
        hid = lin(x, P['w1'][l], tm=big, tn=2048, pro='normmod', epi='relu2', norm=(gmlp, mod, 4, 3),
                  out_dtype=BF16)
        last = l == DEPTH - 1
        x = lin(hid, P['w2'][l], tm=half, tn=1024, epi='resgate_norm' if last else 'resgate', res=x,
                gate=(mod, 5), final_norm=norm_final[None, :] if last else None)
    y = x.reshape(B, L, D_MODEL)
    return y, new


def _run_sample(x, mods, P, hyb, att, norm_final, states, B):
    state_ssd, state_conv, state_wkv, state_shift, caches = states
    tm = B
    new = dict(ssd=[], conv=[], wkv=[], shift=[], win=[[], [], []])
    tabs = _rope_tables(jnp.full((1,), PAST_LEN, jnp.int32))
    caches_t = [_window_minor(c) for c in caches]
    for l in range(DEPTH):
        mod = mods[l]
        i = l // 2
        gmix = P['norm_mix'][l][None, :]
        if l % 2 == 0:
            hp = hyb[i]
            u = _linear(x, hp['w_in'], tm=tm, tn=512, pro='normmod', norm=(gmix, mod, 1, 0))
            u3 = u.reshape(B, 1, U_COLS)
            cbuf = state_conv[i]
            y_ssd, s_ssd = _ssd_step(u3, cbuf[:, :, 0:1024], cbuf[:, :, 1024:1536], state_ssd[i],
                                     hp['cw'], hp['cb'], hp['dtb'], hp['alog'], hp['dexp'], hp['nw'])
            sb = state_shift[i]
            prev = [sb[:, 0:1024], sb[:, 1024:2048], sb[:, 2048:3072], sb[:, 3072:3328]]
            r, w, k, v, kn, ka, g = _rwkv_prep(u, prev, None, hp['mus'], hp['prep_w'], tm=tm, bpb=1,
                                               shifted=False)
            sh = lambda t: t.reshape(B, 1, 1024)
            o, s_wkv = _wkv_step(sh(r), sh(w), sh(k), sh(v), sh(kn), sh(ka), state_wkv[i], nb=2)
            y_rwkv = _rwkv_post(o.reshape(B, 1024), r, k, v, g, hp['lnw'], hp['lnb'], hp['rk'], tm)
            x = _linear(y_ssd.reshape(B, 1024), hp['w_out'][:1024], tm=tm, tn=512, epi='resgate', res=x,
                        gate=(mod, 2), second=(y_rwkv, hp['w_out'][1024:]))
            new['ssd'].append(s_ssd)
            new['conv'].append(jnp.concatenate([cbuf[:, 1:], _raw_conv_rows(u3)], axis=1))
            new['wkv'].append(s_wkv)
            new['shift'].append(_raw_rw_rows(u))
        else:
            ap = att[i]
            qkv = _linear(x, ap['w_qkv'], tm=tm, tn=3 * ATT_DIM, pro='normmod', epi='rope',
                          norm=(gmix, mod, 1, 0), rope=tabs)
            parts = qkv.reshape(B, len(ATT_GROUPS), 3, ATT_HEADS, ATT_HEAD_DIM)
            o = _attn_step(parts[:, :, 0], parts[:, :, 1], parts[:, :, 2], caches_t, i)
            x = _linear(o.reshape(B, ATT_DIM), ap['w_out'], tm=tm, tn=512, epi='resgate', res=x, gate=(mod, 2))
            for gi in range(len(ATT_GROUPS)):
                new['win'][gi].append(parts[:, gi, 1:3][:, None])
        gmlp = P['norm_mlp'][l][None, :]
        hid = _linear(x, P['w1'][l], tm=tm, tn=512, pro='normmod', epi='relu2', norm=(gmlp, mod, 4, 3),
                      out_dtype=BF16)
        last = l == DEPTH - 1
        x = _linear(hid, P['w2'][l], tm=tm, tn=1024 if last else 512, epi='resgate_norm' if last else 'resgate',
                    res=x, gate=(mod, 5), final_norm=norm_final[None, :] if last else None)
    y = x.reshape(B, 1, D_MODEL)
    return y, new


def kernel(x_prompt, x_sample, state_ssd, state_ssd_conv, state_wkv, state_wkv_shift, cache_win0, cache_win1, cache_win2, c_prompt, c_sample, norm_mix, norm_mlp, norm_final, ada_w, ada_b, mlp_w1, mlp_w2, hyb_w_in, hyb_w_out, ssd_conv_w, ssd_conv_b, ssd_dt_bias, ssd_a_log, ssd_d, ssd_norm_w, rwkv_mu, rwkv_w0, rwkv_w2, rwkv_a0, rwkv_a2, rwkv_g2, rwkv_k_k, rwkv_k_a, rwkv_r_k, rwkv_ln_w, rwkv_ln_b, att_w_qkv, att_w_out):
    Bp, L, _ = x_prompt.shape
    Bs = x_sample.shape[0]
    assert x_sample.shape[1] == 1

    nrow = Bp + Bs
    npad = -nrow % 16
    c_all = jnp.concatenate([c_prompt, c_sample, jnp.zeros((npad, D_MODEL), F32)], axis=0)
    mods_p, mods_s = [], []
    for l in range(DEPTH):
        mod = _linear(c_all, ada_w, w_layer=l, tm=nrow + npad, tn=512, pro='silu', epi='bias',
                      bias=ada_b[l][None, :])
        mods_p.append(mod[:Bp].reshape(Bp, 1, N_MOD * D_MODEL))
        mods_s.append(mod[Bp:nrow].reshape(1, Bs, N_MOD * D_MODEL))

    P = dict(norm_mix=norm_mix, norm_mlp=norm_mlp,
             w1=[mlp_w1[l].astype(BF16) for l in range(DEPTH)],
             w2=[mlp_w2[l].astype(BF16) for l in range(DEPTH)])
    hyb = [_hyb_params(i, hyb_w_in, hyb_w_out, ssd_conv_w, ssd_conv_b, ssd_dt_bias, ssd_a_log, ssd_d,
                       ssd_norm_w, rwkv_mu, rwkv_w0, rwkv_w2, rwkv_a0, rwkv_a2, rwkv_g2, rwkv_k_k,
                       rwkv_k_a, rwkv_r_k, rwkv_ln_w, rwkv_ln_b) for i in range(hyb_w_in.shape[0])]
    att = [dict(w_qkv=att_w_qkv[i].astype(BF16), w_out=att_w_out[i].astype(BF16))
           for i in range(att_w_qkv.shape[0])]

    y_p, new_p = _run_prompt(x_prompt.reshape(Bp * L, D_MODEL), mods_p, P, hyb, att, norm_final, Bp, L)
    y_s, new_s = _run_sample(x_sample.reshape(Bs, D_MODEL), mods_s, P, hyb, att, norm_final,
                             (state_ssd, state_ssd_conv, state_wkv, state_wkv_shift,
                              (cache_win0, cache_win1, cache_win2)), Bs)
    st = jnp.stack
    caches = (cache_win0, cache_win1, cache_win2)
    rolled = _roll_windows([_window_minor(c) for c in caches],
                           [_window_minor(st(new_s['win'][g])) for g in range(len(caches))])
    win_s = [jnp.transpose(t, (0, 1, 5, 2, 3, 4)) for t in rolled]
    return (y_p, y_s, st(new_p['ssd']), st(new_s['ssd']), st(new_p['conv']), st(new_s['conv']),
            st(new_p['wkv']), st(new_s['wkv']), st(new_p['shift']), st(new_s['shift']),
            st(new_p['win'][0]), win_s[0], st(new_p['win'][1]), win_s[1],
            st(new_p['win'][2]), win_s[2])
```
